```python
import math
import jax
import jax.numpy as jnp
from jax import lax
import numpy as np

D_MODEL = 1024
BATCH = 16
SEQ = 256
DEPTH = 4
DEC_BATCH = 4
DEC_SEQ = 1024
PAST_LEN = 512

GRID_W = 64
HEAD_DIM = 64
Q_BLOCK = 128
ROPE_THETA = 10000.0
EPS = 1e-6
NEG_INF = -1e30
A_HEADS = 4
A_QK = 2 * HEAD_DIM
A_V = 2 * HEAD_DIM
B_HEADS = 8
B_KV = 2
C_HEADS = 8
C_KV = 2
WINDOW = 128
N_BRANCH = 3
A_WIDTH = A_HEADS * A_V
B_WIDTH = B_HEADS * HEAD_DIM
C_WIDTH = C_HEADS * HEAD_DIM
D_FF = -(-8 * D_MODEL // (3 * 256)) * 256
N_MOD = 6
IN_SIZES = (A_HEADS * A_QK, A_HEADS * A_QK, A_HEADS * A_V,
            B_HEADS * HEAD_DIM, B_KV * HEAD_DIM, B_KV * HEAD_DIM,
            C_HEADS * HEAD_DIM, C_KV * HEAD_DIM, C_KV * HEAD_DIM,
            N_BRANCH * D_MODEL)
D_IN = sum(IN_SIZES)

kernel_name = 'hybrid_diffusion_prefix_trunk_step'


def rmsnorm(x, g):
    xf = x.astype(jnp.float32)
    y = xf * lax.rsqrt(jnp.mean(xf * xf, axis=-1, keepdims=True) + EPS)
    return (y * g.astype(jnp.float32)).astype(x.dtype)


def axial_rope(L, head_dim):
    rows = L // GRID_W
    row = jnp.repeat(jnp.arange(rows), GRID_W).astype(jnp.float32)
    col = jnp.tile(jnp.arange(GRID_W), rows).astype(jnp.float32)
    n = head_dim // 4
    inv = ROPE_THETA ** (-jnp.arange(n, dtype=jnp.float32) / n)
    ang = jnp.concatenate([row[:, None] * inv, col[:, None] * inv], axis=-1)
    return jnp.cos(ang), jnp.sin(ang)


def apply_rope(x, cos, sin):
    x1, x2 = jnp.split(x, 2, axis=-1)
    c = cos[None, :, None, :].astype(x.dtype)
    s = sin[None, :, None, :].astype(x.dtype)
    return jnp.concatenate([x1 * c - x2 * s, x2 * c + x1 * s], axis=-1)


def rope_halves(x, cos, sin):
    x1, x2 = jnp.split(x, 2, axis=-1)
    return jnp.concatenate([apply_rope(x1, cos, sin), apply_rope(x2, cos, sin)], axis=-1)


def to_groups(q, n_kv):
    B, L, H, D = q.shape
    return q.reshape(B, L, n_kv, H // n_kv, D)


def attend(q, k, v, bias=None, sink=None):
    s = jnp.einsum('bqhgd,bkhd->bhgqk', q, k).astype(jnp.float32) * (q.shape[-1] ** -0.5)
    if bias is not None:
        s = s + bias
    if sink is not None:
        sk = jnp.broadcast_to(sink.astype(jnp.float32)[None, :, :, None, None], s.shape[:-1] + (1,))
        p = jax.nn.softmax(jnp.concatenate([s, sk], axis=-1), axis=-1)[..., :-1]
    else:
        p = jax.nn.softmax(s, axis=-1)
    return jnp.einsum('bhgqk,bkhd->bqhgd', p.astype(v.dtype), v)


def diff_attend(q, k, v, lam):
    q1, q2 = jnp.split(q, 2, axis=-1)
    k1, k2 = jnp.split(k, 2, axis=-1)
    return attend(q1, k1, v) - lam.astype(v.dtype) * attend(q2, k2, v)


def sweep_queries(fn, q):
    B, L = q.shape[:2]
    nb = L // Q_BLOCK
    qb = jnp.moveaxis(q.reshape((B, nb, Q_BLOCK) + q.shape[2:]), 1, 0)
    out = lax.map(lambda a: fn(a[0], a[1]), (jnp.arange(nb), qb))
    return jnp.moveaxis(out, 0, 1).reshape((B, L) + out.shape[3:])


def project_heads(h, w_in):
    B, L, _ = h.shape
    idx = np.cumsum(IN_SIZES)[:-1].tolist()
    qa, ka, va, qb, kb, vb, qc, kc, vc, gates = jnp.split(h @ w_in, idx, axis=-1)
    return (qa.reshape(B, L, A_HEADS, A_QK), ka.reshape(B, L, A_HEADS, A_QK), va.reshape(B, L, A_HEADS, A_V),
            qb.reshape(B, L, B_HEADS, HEAD_DIM), kb.reshape(B, L, B_KV, HEAD_DIM), vb.reshape(B, L, B_KV, HEAD_DIM),
            qc.reshape(B, L, C_HEADS, HEAD_DIM), kc.reshape(B, L, C_KV, HEAD_DIM), vc.reshape(B, L, C_KV, HEAD_DIM),
            gates)


def merge_branches(out_a, out_b, out_c, gates, lp, lam_init):
    B, L = gates.shape[:2]
    oa = rmsnorm(out_a.reshape(B, L, A_HEADS, A_V), lp['a_subln_g']) * (1.0 - lam_init)
    ya = oa.reshape(B, L, A_WIDTH) @ lp['w_br_a']
    yb = out_b.reshape(B, L, B_WIDTH) @ lp['w_br_b']
    yc = out_c.reshape(B, L, C_WIDTH) @ lp['w_br_c']
    ga, gb, gc = jnp.split(jax.nn.sigmoid(gates), N_BRANCH, axis=-1)
    return (ga * ya + gb * yb + gc * yc) @ lp['w_out']


def context_mixers(h, lp, lam, lam_init):
    qa, ka, va, qb, kb, vb, qc, kc, vc, gates = project_heads(h, lp['w_in'])
    qb = rmsnorm(qb, lp['b_qnorm_g'])
    kb = rmsnorm(kb, lp['b_knorm_g'])
    sink = lp['c_sink'].reshape(C_KV, C_HEADS // C_KV)
    out_a = sweep_queries(lambda i, q: diff_attend(q, ka, va, lam), to_groups(qa, A_HEADS))
    out_b = sweep_queries(lambda i, q: attend(q, kb, vb), to_groups(qb, B_KV))
    out_c = sweep_queries(lambda i, q: attend(q, kc, vc, sink=sink), to_groups(qc, C_KV))
    merged = merge_branches(out_a, out_b, out_c, gates, lp, lam_init)
    return merged, (ka, va, kb, vb, kc, vc)


def latent_mixers(h, lp, lam, lam_init, cache, cos, sin):
    ctx_ka, ctx_va, ctx_kb, ctx_vb, ctx_kc, ctx_vc = cache
    qa, ka, va, qb, kb, vb, qc, kc, vc, gates = project_heads(h, lp['w_in'])
    L = h.shape[1]
    P = ctx_kc.shape[1]
    qa = rope_halves(qa, cos, sin)
    ka = rope_halves(ka, cos, sin)
    qb = apply_rope(rmsnorm(qb, lp['b_qnorm_g']), cos, sin)
    kb = apply_rope(rmsnorm(kb, lp['b_knorm_g']), cos, sin)
    qc = apply_rope(qc, cos, sin)
    kc = apply_rope(kc, cos, sin)
    sink = lp['c_sink'].reshape(C_KV, C_HEADS // C_KV)
    ka_all = jnp.concatenate([ctx_ka, ka], axis=1)
    va_all = jnp.concatenate([ctx_va, va], axis=1)
    kb_all = jnp.concatenate([ctx_kb, kb], axis=1)
    vb_all = jnp.concatenate([ctx_vb, vb], axis=1)
    out_a = sweep_queries(lambda i, q: diff_attend(q, ka_all, va_all, lam), to_groups(qa, A_HEADS))
    out_b = sweep_queries(lambda i, q: attend(q, kb_all, vb_all), to_groups(qb, B_KV))
    pad = ((0, 0), (WINDOW, WINDOW), (0, 0), (0, 0))
    kc_pad = jnp.pad(kc, pad)
    vc_pad = jnp.pad(vc, pad)
    KW = Q_BLOCK + 2 * WINDOW
    qi = jnp.arange(Q_BLOCK)[:, None]
    kj = jnp.arange(KW)[None, :]
    ctx_bias = jnp.zeros((Q_BLOCK, P), jnp.float32)

    def c_block(i, q):
        start = i * Q_BLOCK
        kw = lax.dynamic_slice_in_dim(kc_pad, start, KW, axis=1)
        vw = lax.dynamic_slice_in_dim(vc_pad, start, KW, axis=1)
        kpos = start - WINDOW + kj
        valid = (kj - qi >= 0) & (kj - qi <= 2 * WINDOW) & (kpos >= 0) & (kpos < L)
        bias = jnp.concatenate([ctx_bias, jnp.where(valid, 0.0, NEG_INF).astype(jnp.float32)], axis=-1)
        return attend(q, jnp.concatenate([ctx_kc, kw], axis=1), jnp.concatenate([ctx_vc, vw], axis=1),
                      bias=bias, sink=sink)

    out_c = sweep_queries(c_block, to_groups(qc, C_KV))
    merged = merge_branches(out_a, out_b, out_c, gates, lp, lam_init)
    return merged, ()


def modulation(cvec, w_mod, b_mod):
    m = jax.nn.silu(cvec) @ w_mod + b_mod
    return [t[:, None, :] for t in jnp.split(m, N_MOD, axis=-1)]


def swiglu(h, w_ffn_in, w_ffn_out):
    a, b = jnp.split(h @ w_ffn_in, 2, axis=-1)
    return (jax.nn.silu(a) * b) @ w_ffn_out


def trunk_layer(x, cvec, lp, mix_fn):
    sh1, sc1, g1, sh2, sc2, g2 = modulation(cvec, lp['w_mod'], lp['b_mod'])
    mixed, kv = mix_fn(rmsnorm(x, lp['norm1_g']) * (1.0 + sc1) + sh1)
    x = x + g1 * mixed
    x = x + g2 * swiglu(rmsnorm(x, lp['norm2_g']) * (1.0 + sc2) + sh2, lp['w_ffn_in'], lp['w_ffn_out'])
    return x, kv


def setup_inputs(seed: int = 0) -> dict:
    key = jax.random.key(seed)
    ks = iter(jax.random.split(key, 40))

    def nrm(shape, scale=1.0):
        return jax.random.normal(next(ks), shape, jnp.float32) * scale

    fd = D_MODEL ** -0.5
    return {
        'x_prompt': nrm((BATCH, SEQ, D_MODEL)),
        'x_sample': nrm((DEC_BATCH, DEC_SEQ, D_MODEL)),
        'cache_a_k': nrm((DEC_BATCH, DEPTH, PAST_LEN, A_HEADS, A_QK)),
        'cache_a_v': nrm((DEC_BATCH, DEPTH, PAST_LEN, A_HEADS, A_V)),
        'cache_b_k': nrm((DEC_BATCH, DEPTH, PAST_LEN, B_KV, HEAD_DIM)),
        'cache_b_v': nrm((DEC_BATCH, DEPTH, PAST_LEN, B_KV, HEAD_DIM)),
        'cache_c_k': nrm((DEC_BATCH, DEPTH, PAST_LEN, C_KV, HEAD_DIM)),
        'cache_c_v': nrm((DEC_BATCH, DEPTH, PAST_LEN, C_KV, HEAD_DIM)),
        'c': nrm((DEC_BATCH, D_MODEL)),
        'c_ctx': nrm((D_MODEL,)),
        'w_mod': nrm((DEPTH, D_MODEL, N_MOD * D_MODEL), fd),
        'b_mod': nrm((DEPTH, N_MOD * D_MODEL), 0.02),
        'norm1_g': 1.0 + nrm((DEPTH, D_MODEL), 0.02),
        'norm2_g': 1.0 + nrm((DEPTH, D_MODEL), 0.02),
        'w_in': nrm((DEPTH, D_MODEL, D_IN), fd),
        'a_lam_q1': nrm((DEPTH, HEAD_DIM), 0.1),
        'a_lam_k1': nrm((DEPTH, HEAD_DIM), 0.1),
        'a_lam_q2': nrm((DEPTH, HEAD_DIM), 0.1),
        'a_lam_k2': nrm((DEPTH, HEAD_DIM), 0.1),
        'a_subln_g': 1.0 + nrm((DEPTH, A_V), 0.02),
        'b_qnorm_g': 1.0 + nrm((DEPTH, HEAD_DIM), 0.02),
        'b_knorm_g': 1.0 + nrm((DEPTH, HEAD_DIM), 0.02),
        'c_sink': nrm((DEPTH, C_HEADS), 0.5),
        'w_br_a': nrm((DEPTH, A_WIDTH, D_MODEL), A_WIDTH ** -0.5),
        'w_br_b': nrm((DEPTH, B_WIDTH, D_MODEL), B_WIDTH ** -0.5),
        'w_br_c': nrm((DEPTH, C_WIDTH, D_MODEL), C_WIDTH ** -0.5),
        'w_out': nrm((DEPTH, D_MODEL, D_MODEL), fd),
        'w_ffn_in': nrm((DEPTH, D_MODEL, 2 * D_FF), fd),
        'w_ffn_out': nrm((DEPTH, D_FF, D_MODEL), D_FF ** -0.5),
        'final_g': 1.0 + nrm((D_MODEL,), 0.02),
    }


def reference(x_prompt, x_sample, cache_a_k, cache_a_v, cache_b_k, cache_b_v, cache_c_k, cache_c_v,
              c, c_ctx, w_mod, b_mod, norm1_g, norm2_g, w_in, a_lam_q1, a_lam_k1, a_lam_q2, a_lam_k2,
              a_subln_g, b_qnorm_g, b_knorm_g, c_sink, w_br_a, w_br_b, w_br_c, w_out, w_ffn_in,
              w_ffn_out, final_g):
    cos, sin = axial_rope(x_sample.shape[1], HEAD_DIM)
    xp = x_prompt
    xs = x_sample
    ctx_cvec = c_ctx[None, :]
    st_ak, st_av, st_bk, st_bv, st_ck, st_cv = [], [], [], [], [], []
    for l in range(DEPTH):
        lp = {'w_mod': w_mod[l], 'b_mod': b_mod[l], 'norm1_g': norm1_g[l], 'norm2_g': norm2_g[l],
              'w_in': w_in[l], 'a_subln_g': a_subln_g[l], 'b_qnorm_g': b_qnorm_g[l],
              'b_knorm_g': b_knorm_g[l], 'c_sink': c_sink[l], 'w_br_a': w_br_a[l], 'w_br_b': w_br_b[l],
              'w_br_c': w_br_c[l], 'w_out': w_out[l], 'w_ffn_in': w_ffn_in[l], 'w_ffn_out': w_ffn_out[l]}
        lam_init = 0.8 - 0.6 * math.exp(-0.3 * l)
        lam = (jnp.exp(jnp.sum(a_lam_q1[l].astype(jnp.float32) * a_lam_k1[l].astype(jnp.float32)))
               - jnp.exp(jnp.sum(a_lam_q2[l].astype(jnp.float32) * a_lam_k2[l].astype(jnp.float32)))
               + lam_init)
        xp, kv = trunk_layer(xp, ctx_cvec, lp, lambda h: context_mixers(h, lp, lam, lam_init))
        st_ak.append(kv[0]); st_av.append(kv[1]); st_bk.append(kv[2])
        st_bv.append(kv[3]); st_ck.append(kv[4]); st_cv.append(kv[5])
        cache = (cache_a_k[:, l], cache_a_v[:, l], cache_b_k[:, l], cache_b_v[:, l],
                 cache_c_k[:, l], cache_c_v[:, l])
        xs, _ = trunk_layer(xs, c, lp, lambda h: latent_mixers(h, lp, lam, lam_init, cache, cos, sin))
    y_prompt = rmsnorm(xp, final_g)
    y_sample = rmsnorm(xs, final_g)
    new_a_k = jnp.stack(st_ak, axis=1)
    new_a_v = jnp.stack(st_av, axis=1)
    new_b_k = jnp.stack(st_bk, axis=1)
    new_b_v = jnp.stack(st_bv, axis=1)
    new_c_k = jnp.stack(st_ck, axis=1)
    new_c_v = jnp.stack(st_cv, axis=1)
    return (y_prompt, y_sample, new_a_k, new_a_v, new_b_k, new_b_v, new_c_k, new_c_v)
```

```python
import functools
import math

import jax
import jax.numpy as jnp
from jax import lax
from jax.experimental import pallas as pl
from jax.experimental.pallas import tpu as pltpu

D_MODEL = 1024
BATCH = 16
SEQ = 256
DEPTH = 4
DEC_BATCH = 4
DEC_SEQ = 1024
PAST_LEN = 512
GRID_W = 64
HEAD_DIM = 64
ROPE_THETA = 10000.0
EPS = 1e-6
NEG_INF = -1e30
A_HEADS = 4
WINDOW = 128
D_FF = -(-8 * D_MODEL // (3 * 256)) * 256
N_MOD = 6
D_QKV = 3072
D_IN = 2 * D_QKV
D_ATT = 1536
LANES = 128
SCALE = HEAD_DIM ** -0.5

QA, KA, VA, QB, KB, VB, QC, KC, VC = 0, 512, 1024, 1536, 2048, 2176, 2304, 2816, 2944
R_LQ1, R_LK1, R_LQ2, R_LK2, R_SUBG, R_BQG, R_BKG = range(7)

F32 = jnp.float32
BF16 = jnp.bfloat16
VMEM_LIMIT = 56 * 1024 * 1024


def _cparams(sem):
    return pltpu.CompilerParams(dimension_semantics=sem, vmem_limit_bytes=VMEM_LIMIT)


MOD_TN = 1536


def _mod_kernel(cv_ref, w_ref, b_ref, o_ref):
    cv = cv_ref[...]
    s = (cv * jax.nn.sigmoid(cv)).astype(BF16)
    o_ref[...] = jnp.dot(s, w_ref[...].astype(BF16), preferred_element_type=F32) + b_ref[...]


def _modulation(cv8, w_mod, b_mod):
    n = N_MOD * D_MODEL
    return pl.pallas_call(
        _mod_kernel,
        grid=(DEPTH, n // MOD_TN),
        in_specs=[
            pl.BlockSpec((8, D_MODEL), lambda l, j: (0, 0)),
            pl.BlockSpec((None, D_MODEL, MOD_TN), lambda l, j: (l, 0, j)),
            pl.BlockSpec((None, 1, MOD_TN), lambda l, j: (l, 0, j)),
        ],
        out_specs=pl.BlockSpec((None, 8, MOD_TN), lambda l, j: (l, 0, j)),
        out_shape=jax.ShapeDtypeStruct((DEPTH, 8, n), F32),
        compiler_params=_cparams(("arbitrary", "arbitrary")),
        name="modulation",
    )(cv8, w_mod, b_mod.reshape(DEPTH, 1, n))


NORM_CHUNK = 256


def _norm_mod_rows(x_ref, h_ref, g, sc, sh, rows):
    def body(i, carry):
        r = pl.ds(pl.multiple_of(i * NORM_CHUNK, NORM_CHUNK), NORM_CHUNK)
        x = x_ref[r, :]
        y = x * lax.rsqrt(jnp.mean(x * x, axis=-1, keepdims=True) + EPS) * g
        h_ref[r, :] = (y * (1.0 + sc) + sh).astype(BF16)
        return carry
    lax.fori_loop(0, rows // NORM_CHUNK, body, 0)


def _seg64_sum(x, lane):
    t = x
    for s in (32, 16, 8, 4, 2, 1):
        up = pltpu.roll(t, LANES - s, 1)
        dn = pltpu.roll(t, s, 1)
        t = t + jnp.where((lane & s) == 0, up, dn)
    return t


def _head_rmsnorm(x, g, lane):
    return x * lax.rsqrt(_seg64_sum(x * x, lane) * (1.0 / HEAD_DIM) + EPS) * g


def _rope(x, cos, sin, lane):
    partner = jnp.where((lane & 32) == 0, pltpu.roll(x, LANES - 32, 1), pltpu.roll(x, 32, 1))
    return x * cos + partner * sin


def _lo_hi(x, lane_lo, src_hi):
    other = pltpu.roll(x, 64, 1)
    zero = jnp.zeros_like(x)
    if src_hi:
        return jnp.where(lane_lo, other, zero), jnp.where(lane_lo, zero, x)
    return jnp.where(lane_lo, x, zero), jnp.where(lane_lo, zero, other)


def _dot_nt(a, b):
    return lax.dot_general(a, b, (((1,), (1,)), ((), ())), preferred_element_type=F32)


def _softmax_parts(s, extra=None):
    m = jnp.max(s, axis=-1, keepdims=True)
    if extra is not None:
        m = jnp.maximum(m, extra)
    p = jnp.exp(s - m)
    l = jnp.sum(p, axis=-1, keepdims=True)
    if extra is not None:
        l = l + jnp.exp(extra - m)
    return p, l


def _lam(sp_ref, lam_init):
    dot1 = jnp.sum(sp_ref[R_LQ1:R_LQ1 + 1, :] * sp_ref[R_LK1:R_LK1 + 1, :], axis=-1, keepdims=True)
    dot2 = jnp.sum(sp_ref[R_LQ2:R_LQ2 + 1, :] * sp_ref[R_LK2:R_LK2 + 1, :], axis=-1, keepdims=True)
    return jnp.exp(dot1) - jnp.exp(dot2) + lam_init


def _subln(o, g, lam_init):
    return o * lax.rsqrt(jnp.mean(o * o, axis=-1, keepdims=True) + EPS) * g * (1.0 - lam_init)


def _mod_spec(layer, rows_per_mod, row0, tm):
    return pl.BlockSpec((None, None, 1, N_MOD * D_MODEL),
                        lambda m, *_: (layer, row0 + (m * tm) // rows_per_mod, 0, 0))


PROJ_TM = 2048
PROJ_TN = 512


def _proj_kernel(x_ref, mod_ref, g_ref, w_ref, o_ref, h_scr, *, tm):
    @pl.when(pl.program_id(1) == 0)
    def _():
        _norm_mod_rows(x_ref, h_scr, g_ref[...], mod_ref[:, D_MODEL:2 * D_MODEL],
                       mod_ref[:, 0:D_MODEL], tm)
    o_ref[...] = jnp.dot(h_scr[...], w_ref[...].astype(BF16), preferred_element_type=F32)


def _proj(x, mod, g, w_in, layer, rows_per_mod, row0):
    t = x.shape[0]
    tm = min(PROJ_TM, rows_per_mod)
    return pl.pallas_call(
        functools.partial(_proj_kernel, tm=tm),
        grid=(t // tm, D_IN // PROJ_TN),
        in_specs=[
            pl.BlockSpec((tm, D_MODEL), lambda m, n: (m, 0)),
            _mod_spec(layer, rows_per_mod, row0, tm),
            pl.BlockSpec((None, 1, D_MODEL), lambda m, n: (layer, 0, 0)),
            pl.BlockSpec((None, D_MODEL, PROJ_TN), lambda m, n: (layer, 0, n)),
        ],
        out_specs=pl.BlockSpec((tm, PROJ_TN), lambda m, n: (m, n)),
        out_shape=jax.ShapeDtypeStruct((t, D_IN), F32),
        scratch_shapes=[pltpu.VMEM((tm, D_MODEL), BF16)],
        compiler_params=_cparams(("arbitrary", "arbitrary")),
        name="proj",
    )(x, mod, g, w_in)


def _att_ctx_kernel(qkv_ref, sp_ref, sink_ref, att_ref, kbn_ref, *, layer, lam_init):
    lane = lax.broadcasted_iota(jnp.int32, (SEQ, LANES), 1)
    lane_lo = lane < 64
    lam = _lam(sp_ref, lam_init)
    subg = sp_ref[R_SUBG:R_SUBG + 1, :]
    bqg = sp_ref[R_BQG:R_BQG + 1, :]
    bkg = sp_ref[R_BKG:R_BKG + 1, :]

    for h in range(A_HEADS):
        q = qkv_ref[:, QA + h * LANES:QA + (h + 1) * LANES].astype(BF16)
        k = qkv_ref[:, KA + h * LANES:KA + (h + 1) * LANES]
        v = qkv_ref[:, VA + h * LANES:VA + (h + 1) * LANES].astype(BF16)
        zero = jnp.zeros_like(k)
        kst = jnp.concatenate([jnp.where(lane_lo, k, zero).astype(BF16),
                               jnp.where(lane_lo, zero, k).astype(BF16)], axis=0)
        s = _dot_nt(q, kst) * SCALE
        p1, l1 = _softmax_parts(s[:, :SEQ])
        p2, l2 = _softmax_parts(s[:, SEQ:])
        o1 = jnp.dot(p1.astype(BF16), v, preferred_element_type=F32) / l1
        o2 = jnp.dot(p2.astype(BF16), v, preferred_element_type=F32) / l2
        o = _subln(o1 - lam * o2, subg, lam_init)
        att_ref[:, h * LANES:(h + 1) * LANES] = o.astype(BF16)

    kb = _head_rmsnorm(qkv_ref[:, KB:KB + LANES], bkg, lane)
    kbn_ref[...] = kb
    for mixer in range(2):
        if mixer == 0:
            q0, k_t, v_t, o0 = QB, kb, qkv_ref[:, VB:VB + LANES], 512
        else:
            q0, k_t, v_t, o0 = QC, qkv_ref[:, KC:KC + LANES], qkv_ref[:, VC:VC + LANES], 1024
        for g in range(2):
            k_lo, k_hi = _lo_hi(k_t, lane_lo, g == 1)
            v_lo, v_hi = _lo_hi(v_t, lane_lo, g == 1)
            kst = jnp.concatenate([k_lo.astype(BF16), k_hi.astype(BF16)], axis=0)
            v_lo = v_lo.astype(BF16)
            v_hi = v_hi.astype(BF16)
            for jj in range(2):
                j = 2 * g + jj
                q = qkv_ref[:, q0 + j * LANES:q0 + (j + 1) * LANES]
                if mixer == 0:
                    q = _head_rmsnorm(q, bqg, lane)
                s = _dot_nt(q.astype(BF16), kst) * SCALE
                if mixer == 0:
                    pe, le = _softmax_parts(s[:, :SEQ])
                    po, lo_ = _softmax_parts(s[:, SEQ:])
                else:
                    pe, le = _softmax_parts(s[:, :SEQ], sink_ref[layer, 2 * j])
                    po, lo_ = _softmax_parts(s[:, SEQ:], sink_ref[layer, 2 * j + 1])
                o = (jnp.dot(pe.astype(BF16), v_lo, preferred_element_type=F32)
                     + jnp.dot(po.astype(BF16), v_hi, preferred_element_type=F32))
                o = o / jnp.where(lane_lo, le, lo_)
                att_ref[:, o0 + j * LANES:o0 + (j + 1) * LANES] = o.astype(BF16)


def _att_ctx(qkv, sp, sink, layer, lam_init):
    t = qkv.shape[0]
    return pl.pallas_call(
        functools.partial(_att_ctx_kernel, layer=layer, lam_init=lam_init),
        grid=(t // SEQ,),
        in_specs=[
            pl.BlockSpec((SEQ, D_QKV), lambda b: (b, 0)),
            pl.BlockSpec((None, 8, LANES), lambda b: (layer, 0, 0)),
            pl.BlockSpec(memory_space=pltpu.SMEM),
        ],
        out_specs=[
            pl.BlockSpec((SEQ, D_ATT), lambda b: (b, 0)),
            pl.BlockSpec((SEQ, LANES), lambda b: (b, 0)),
        ],
        out_shape=[jax.ShapeDtypeStruct((t, D_ATT), BF16),
                   jax.ShapeDtypeStruct((t, LANES), F32)],
        compiler_params=_cparams(("arbitrary",)),
        name="att_ctx",
    )(qkv, sp, sink)


LAT_TQ = 256
NK = PAST_LEN + DEC_SEQ
KCH = 512
PREP_ROWS = 256


def _online_attend(q, k_chunks, v_chunks, masks=None, sink=None):
    m = l = acc = None
    for c, (k_c, v_c) in enumerate(zip(k_chunks, v_chunks)):
        s = _dot_nt(q, k_c)
        if masks is not None and masks[c] is not None:
            s = jnp.where(masks[c], s, NEG_INF)
        mc = jnp.max(s, axis=-1, keepdims=True)
        if m is None:
            m_new = mc if sink is None else jnp.maximum(mc, sink)
            p = jnp.exp(s - m_new)
            l = jnp.sum(p, axis=-1, keepdims=True)
            acc = jnp.dot(p.astype(BF16), v_c, preferred_element_type=F32)
        else:
            m_new = jnp.maximum(m, mc)
            alpha = jnp.exp(m - m_new)
            p = jnp.exp(s - m_new)
            l = alpha * l + jnp.sum(p, axis=-1, keepdims=True)
            acc = alpha * acc + jnp.dot(p.astype(BF16), v_c, preferred_element_type=F32)
        m = m_new
    if sink is not None:
        l = l + jnp.exp(sink - m)
    return acc, l


def _att_lat_kernel(q_ref, kva_ref, kvv_ref, kv4_ref, kv5_ref,
                    cak_ref, cav_ref, cbk_ref, cbv_ref, cck_ref, ccv_ref,
                    cos_ref, sin_ref, sp_ref, sink_ref,
                    att_ref,
                    ka_scr, va_scr, kb_scr, vb_scr, kc_scr, vc_scr, *, layer, lam_init):
    u = pl.program_id(1)
    subg = sp_ref[R_SUBG:R_SUBG + 1, :]
    bqg = sp_ref[R_BQG:R_BQG + 1, :]
    bkg = sp_ref[R_BKG:R_BKG + 1, :]
    lane = lax.broadcasted_iota(jnp.int32, (LAT_TQ, LANES), 1)
    lane_lo = lane < 64

    @pl.when(u == 0)
    def _prepare_keys():
        def put(dst, a_k, a_v, pairs):
            lo_rows = pl.ds(pl.multiple_of(dst, PREP_ROWS), PREP_ROWS)
            hi_rows = pl.ds(pl.multiple_of(NK + dst, PREP_ROWS), PREP_ROWS)
            for h in range(A_HEADS):
                k = a_k[h]
                zero = jnp.zeros_like(k)
                ka_scr[h, lo_rows, :] = jnp.where(lane_lo, k, zero).astype(BF16)
                ka_scr[h, hi_rows, :] = jnp.where(lane_lo, zero, k).astype(BF16)
                va_scr[h, lo_rows, :] = a_v[h].astype(BF16)
            for x, scr in pairs:
                for g in range(2):
                    lo, hi = _lo_hi(x, lane_lo, g == 1)
                    scr[g, lo_rows, :] = lo.astype(BF16)
                    scr[g, hi_rows, :] = hi.astype(BF16)

        def cached(i, carry):
            r = pl.ds(pl.multiple_of(i * PREP_ROWS, PREP_ROWS), PREP_ROWS)
            put(i * PREP_ROWS,
                [cak_ref[r, h * LANES:(h + 1) * LANES] for h in range(A_HEADS)],
                [cav_ref[r, h * LANES:(h + 1) * LANES] for h in range(A_HEADS)],
                ((cbk_ref[r, :], kb_scr), (cbv_ref[r, :], vb_scr),
                 (cck_ref[r, :], kc_scr), (ccv_ref[r, :], vc_scr)))
            return carry
        lax.fori_loop(0, PAST_LEN // PREP_ROWS, cached, 0)

        def latent(i, carry):
            r = pl.ds(pl.multiple_of(i * PREP_ROWS, PREP_ROWS), PREP_ROWS)
            cos = cos_ref[r, :]
            sin = sin_ref[r, :]
            kb = _rope(_head_rmsnorm(kv4_ref[r, 0:LANES], bkg, lane), cos, sin, lane)
            kc = _rope(kv5_ref[r, 256:256 + LANES], cos, sin, lane)
            put(PAST_LEN + i * PREP_ROWS,
                [_rope(kva_ref[r, h * LANES:(h + 1) * LANES], cos, sin, lane) for h in range(A_HEADS)],
                [kvv_ref[r, h * LANES:(h + 1) * LANES] for h in range(A_HEADS)],
                ((kb, kb_scr), (kv4_ref[r, LANES:2 * LANES], vb_scr),
                 (kc, kc_scr), (kv5_ref[r, 384:384 + LANES], vc_scr)))
            return carry
        lax.fori_loop(0, DEC_SEQ // PREP_ROWS, latent, 0)

    lam = _lam(sp_ref, lam_init)
    chunks = lambda scr, i, base: [scr[i, base + c * KCH:base + (c + 1) * KCH, :] for c in range(NK // KCH)]

    def for_query_tiles(body):
        def step(i, carry):
            r = pl.ds(pl.multiple_of(i * LAT_TQ, LAT_TQ), LAT_TQ)
            body(i, r, cos_ref[r, :], sin_ref[r, :])
            return carry
        lax.fori_loop(0, DEC_SEQ // LAT_TQ, step, 0)

    @pl.when(u < A_HEADS)
    def _mixer_a():
        def body(i, r, cos, sin):
            q = (_rope(q_ref[r, :], cos, sin, lane) * SCALE).astype(BF16)
            v = chunks(va_scr, u, 0)
            a1, l1 = _online_attend(q, chunks(ka_scr, u, 0), v)
            a2, l2 = _online_attend(q, chunks(ka_scr, u, NK), v)
            att_ref[r, :] = _subln(a1 / l1 - lam * (a2 / l2), subg, lam_init).astype(BF16)
        for_query_tiles(body)

    @pl.when((u >= A_HEADS) & (u < A_HEADS + 4))
    def _mixer_b():
        g = lax.shift_right_logical(u - A_HEADS, 1)

        def body(i, r, cos, sin):
            q = _head_rmsnorm(q_ref[r, :], bqg, lane)
            q = (_rope(q, cos, sin, lane) * SCALE).astype(BF16)
            ae, le = _online_attend(q, chunks(kb_scr, g, 0), chunks(vb_scr, g, 0))
            ao, lo_ = _online_attend(q, chunks(kb_scr, g, NK), chunks(vb_scr, g, NK))
            att_ref[r, :] = ((ae + ao) / jnp.where(lane_lo, le, lo_)).astype(BF16)
        for_query_tiles(body)

    @pl.when(u >= A_HEADS + 4)
    def _mixer_c():
        j = u - (A_HEADS + 4)
        g = lax.shift_right_logical(j, 1)

        def body(i, r, cos, sin):
            ws = pl.multiple_of(jnp.clip(i * LAT_TQ - WINDOW, 0, DEC_SEQ - KCH), LANES)
            qpos = i * LAT_TQ + lax.broadcasted_iota(jnp.int32, (LAT_TQ, KCH), 0)
            kpos = ws + lax.broadcasted_iota(jnp.int32, (LAT_TQ, KCH), 1)
            masks = (None, jnp.abs(kpos - qpos) <= WINDOW)
            q = (_rope(q_ref[r, :], cos, sin, lane) * SCALE).astype(BF16)
            parts = []
            for half in range(2):
                base = half * NK
                win = pl.ds(pl.multiple_of(base + PAST_LEN + ws, LANES), KCH)
                parts.append(_online_attend(
                    q, (kc_scr[g, base:base + PAST_LEN, :], kc_scr[g, win, :]),
                    (vc_scr[g, base:base + PAST_LEN, :], vc_scr[g, win, :]),
                    masks=masks, sink=sink_ref[layer, 2 * j + half]))
            o = (parts[0][0] + parts[1][0]) / jnp.where(lane_lo, parts[0][1], parts[1][1])
            att_ref[r, :] = o.astype(BF16)
        for_query_tiles(body)


N_UNITS = 12


def _unit_q_col(u):
    return jnp.where(u < A_HEADS, u, jnp.where(u < A_HEADS + 4, QB // LANES - A_HEADS + u,
                                               QC // LANES - A_HEADS - 4 + u))


def _att_lat(qkv, caches, cos, sin, sp, sink, layer, lam_init):
    t = qkv.shape[0]
    kv_blk = lambda col: pl.BlockSpec((DEC_SEQ, 512), lambda b, i: (b, col))
    cache_blk = lambda w: pl.BlockSpec((None, None, PAST_LEN, w), lambda b, i: (b, layer, 0, 0))
    return pl.pallas_call(
        functools.partial(_att_lat_kernel, layer=layer, lam_init=lam_init),
        grid=(DEC_BATCH, N_UNITS),
        in_specs=[
            pl.BlockSpec((DEC_SEQ, LANES), lambda b, i: (b, _unit_q_col(i))),
            kv_blk(KA // 512), kv_blk(VA // 512), kv_blk(4), kv_blk(5),
            cache_blk(512), cache_blk(512), cache_blk(LANES), cache_blk(LANES),
            cache_blk(LANES), cache_blk(LANES),
            pl.BlockSpec((DEC_SEQ, LANES), lambda b, i: (0, 0)),
            pl.BlockSpec((DEC_SEQ, LANES), lambda b, i: (0, 0)),
            pl.BlockSpec((None, 8, LANES), lambda b, i: (layer, 0, 0)),
            pl.BlockSpec(memory_space=pltpu.SMEM),
        ],
        out_specs=pl.BlockSpec((DEC_SEQ, LANES), lambda b, i: (b, i)),
        out_shape=jax.ShapeDtypeStruct((t, D_ATT), BF16),
        scratch_shapes=[
            pltpu.VMEM((A_HEADS, 2 * NK, LANES), BF16),
            pltpu.VMEM((A_HEADS, NK, LANES), BF16),
            pltpu.VMEM((2, 2 * NK, LANES), BF16),
            pltpu.VMEM((2, 2 * NK, LANES), BF16),
            pltpu.VMEM((2, 2 * NK, LANES), BF16),
            pltpu.VMEM((2, 2 * NK, LANES), BF16),
        ],
        compiler_params=_cparams(("arbitrary", "arbitrary")),
        name="att_lat",
    )(qkv, qkv, qkv, qkv, qkv, *caches, cos, sin, sp, sink)


POST_TM = 512


def _post_kernel(att_ref, gate_ref, x_ref, mod_ref, wa_ref, wb_ref, wc_ref, wo_ref, o_ref):
    merged = None
    for i, w_ref in enumerate((wa_ref, wb_ref, wc_ref)):
        y = jnp.dot(att_ref[:, i * 512:(i + 1) * 512], w_ref[...].astype(BF16),
                    preferred_element_type=F32)
        term = jax.nn.sigmoid(gate_ref[:, i * D_MODEL:(i + 1) * D_MODEL]) * y
        merged = term if merged is None else merged + term
    mixed = jnp.dot(merged.astype(BF16), wo_ref[...].astype(BF16), preferred_element_type=F32)
    o_ref[...] = x_ref[...] + mod_ref[:, 2 * D_MODEL:3 * D_MODEL] * mixed


def _post(att, qkvg, x, mod, wa, wb, wc, wo, layer, rows_per_mod, row0):
    t = x.shape[0]
    tm = POST_TM
    full = lambda r: pl.BlockSpec((None, r, D_MODEL), lambda m: (layer, 0, 0))
    return pl.pallas_call(
        _post_kernel,
        grid=(t // tm,),
        in_specs=[
            pl.BlockSpec((tm, D_ATT), lambda m: (m, 0)),
            pl.BlockSpec((tm, D_QKV), lambda m: (m, 1)),
            pl.BlockSpec((tm, D_MODEL), lambda m: (m, 0)),
            _mod_spec(layer, rows_per_mod, row0, tm),
            full(512), full(512), full(512), full(D_MODEL),
        ],
        out_specs=pl.BlockSpec((tm, D_MODEL), lambda m: (m, 0)),
        out_shape=jax.ShapeDtypeStruct((t, D_MODEL), F32),
        compiler_params=_cparams(("arbitrary",)),
        name="post",
    )(att, qkvg, x, mod, wa, wb, wc, wo)


FFN_TM = 1024
FFN_TF = 256


def _ffn_kernel(x_ref, mod_ref, g_ref, wa_ref, wb_ref, wo_ref, o_ref, h_scr, acc_scr):
    f = pl.program_id(1)

    @pl.when(f == 0)
    def _():
        _norm_mod_rows(x_ref, h_scr, g_ref[...], mod_ref[:, 4 * D_MODEL:5 * D_MODEL],
                       mod_ref[:, 3 * D_MODEL:4 * D_MODEL], FFN_TM)
        acc_scr[...] = jnp.zeros_like(acc_scr)

    h = h_scr[...]
    a = jnp.dot(h, wa_ref[...].astype(BF16), preferred_element_type=F32)
    b = jnp.dot(h, wb_ref[...].astype(BF16), preferred_element_type=F32)
    y = (a * jax.nn.sigmoid(a) * b).astype(BF16)
    acc_scr[...] += jnp.dot(y, wo_ref[...].astype(BF16), preferred_element_type=F32)

    @pl.when(f == pl.num_programs(1) - 1)
    def _():
        o_ref[...] = x_ref[...] + mod_ref[:, 5 * D_MODEL:6 * D_MODEL] * acc_scr[...]


def _ffn(x, mod, g, w_in, w_out, layer, rows_per_mod, row0):
    t = x.shape[0]
    tm = FFN_TM
    nf = D_FF // FFN_TF
    return pl.pallas_call(
        _ffn_kernel,
        grid=(t // tm, nf),
        in_specs=[
            pl.BlockSpec((tm, D_MODEL), lambda m, f: (m, 0)),
            _mod_spec(layer, rows_per_mod, row0, tm),
            pl.BlockSpec((None, 1, D_MODEL), lambda m, f: (layer, 0, 0)),
            pl.BlockSpec((None, D_MODEL, FFN_TF), lambda m, f: (layer, 0, f)),
            pl.BlockSpec((None, D_MODEL, FFN_TF), lambda m, f: (layer, 0, f + nf)),
            pl.BlockSpec((None, FFN_TF, D_MODEL), lambda m, f: (layer, f, 0)),
        ],
        out_specs=pl.BlockSpec((tm, D_MODEL), lambda m, f: (m, 0)),
        out_shape=jax.ShapeDtypeStruct((t, D_MODEL), F32),
        scratch_shapes=[pltpu.VMEM((tm, D_MODEL), BF16), pltpu.VMEM((tm, D_MODEL), F32)],
        compiler_params=_cparams(("arbitrary", "arbitrary")),
        name="ffn",
    )(x, mod, g, w_in, w_in, w_out)


FINAL_TM = 512


def _final_kernel(x_ref, g_ref, o_ref):
    x = x_ref[...]
    o_ref[...] = x * lax.rsqrt(jnp.mean(x * x, axis=-1, keepdims=True) + EPS) * g_ref[...]


def _final_norm(x, g):
    t = x.shape[0]
    return pl.pallas_call(
        _final_kernel,
        grid=(t // FINAL_TM,),
        in_specs=[pl.BlockSpec((FINAL_TM, D_MODEL), lambda m: (m, 0)),
                  pl.BlockSpec((1, D_MODEL), lambda m: (0, 0))],
        out_specs=pl.BlockSpec((FINAL_TM, D_MODEL), lambda m: (m, 0)),
        out_shape=jax.ShapeDtypeStruct((t, D_MODEL), F32),
        compiler_params=_cparams(("arbitrary",)),
        name="final_norm",
    )(x, g)


def _rope_tables():
    rows = DEC_SEQ // GRID_W
    row = jnp.repeat(jnp.arange(rows), GRID_W).astype(F32)
    col = jnp.tile(jnp.arange(GRID_W), rows).astype(F32)
    n = HEAD_DIM // 4
    inv = ROPE_THETA ** (-jnp.arange(n, dtype=F32) / n)
    ang = jnp.concatenate([row[:, None] * inv, col[:, None] * inv], axis=-1)
    cos, sin = jnp.cos(ang), jnp.sin(ang)
    cos_t = jnp.tile(cos, (1, 4))
    sin_t = jnp.tile(jnp.concatenate([-sin, sin], axis=-1), (1, 2))
    return cos_t, sin_t


def _pack_small(a_lam_q1, a_lam_k1, a_lam_q2, a_lam_k2, a_subln_g, b_qnorm_g, b_knorm_g):
    pad = lambda v: jnp.pad(v, ((0, 0), (0, LANES - HEAD_DIM)))
    rows = [pad(a_lam_q1), pad(a_lam_k1), pad(a_lam_q2), pad(a_lam_k2), a_subln_g,
            jnp.tile(b_qnorm_g, (1, 2)), jnp.tile(b_knorm_g, (1, 2)),
            jnp.zeros((DEPTH, LANES), F32)]
    return jnp.stack(rows, axis=1)


def kernel(x_prompt, x_sample, cache_a_k, cache_a_v, cache_b_k, cache_b_v, cache_c_k, cache_c_v, c, c_ctx, w_mod, b_mod, norm1_g, norm2_g, w_in, a_lam_q1, a_lam_k1, a_lam_q2, a_lam_k2, a_subln_g, b_qnorm_g, b_knorm_g, c_sink, w_br_a, w_br_b, w_br_c, w_out, w_ffn_in, w_ffn_out, final_g):
    t_ctx = BATCH * SEQ
    t_lat = DEC_BATCH * DEC_SEQ
    xp = x_prompt.reshape(t_ctx, D_MODEL)
    xs = x_sample.reshape(t_lat, D_MODEL)
    cv8 = jnp.concatenate([c_ctx[None, :], c, jnp.zeros((8 - 1 - DEC_BATCH, D_MODEL), F32)], axis=0)
    mod = _modulation(cv8, w_mod, b_mod).reshape(DEPTH, 8, 1, N_MOD * D_MODEL)
    cos_t, sin_t = _rope_tables()
    sp = _pack_small(a_lam_q1, a_lam_k1, a_lam_q2, a_lam_k2, a_subln_g, b_qnorm_g, b_knorm_g)
    n1 = norm1_g.reshape(DEPTH, 1, D_MODEL)
    n2 = norm2_g.reshape(DEPTH, 1, D_MODEL)
    caches = (cache_a_k.reshape(DEC_BATCH, DEPTH, PAST_LEN, 512),
              cache_a_v.reshape(DEC_BATCH, DEPTH, PAST_LEN, 512),
              cache_b_k.reshape(DEC_BATCH, DEPTH, PAST_LEN, LANES),
              cache_b_v.reshape(DEC_BATCH, DEPTH, PAST_LEN, LANES),
              cache_c_k.reshape(DEC_BATCH, DEPTH, PAST_LEN, LANES),
              cache_c_v.reshape(DEC_BATCH, DEPTH, PAST_LEN, LANES))
    st = [[] for _ in range(6)]
    for l in range(DEPTH):
        lam_init = 0.8 - 0.6 * math.exp(-0.3 * l)

        qkv_c = _proj(xp, mod, n1, w_in, l, t_ctx, 0)
        att_c, kbn = _att_ctx(qkv_c, sp, c_sink, l, lam_init)
        xp = _post(att_c, qkv_c, xp, mod, w_br_a, w_br_b, w_br_c, w_out, l, t_ctx, 0)
        xp = _ffn(xp, mod, n2, w_ffn_in, w_ffn_out, l, t_ctx, 0)
        st[0].append(qkv_c[:, KA:KA + 512].reshape(BATCH, SEQ, A_HEADS, 2 * HEAD_DIM))
        st[1].append(qkv_c[:, VA:VA + 512].reshape(BATCH, SEQ, A_HEADS, 2 * HEAD_DIM))
        st[2].append(kbn.reshape(BATCH, SEQ, 2, HEAD_DIM))
        st[3].append(qkv_c[:, VB:VB + LANES].reshape(BATCH, SEQ, 2, HEAD_DIM))
        st[4].append(qkv_c[:, KC:KC + LANES].reshape(BATCH, SEQ, 2, HEAD_DIM))
        st[5].append(qkv_c[:, VC:VC + LANES].reshape(BATCH, SEQ, 2, HEAD_DIM))

        qkv_s = _proj(xs, mod, n1, w_in, l, DEC_SEQ, 1)
        att_s = _att_lat(qkv_s, caches, cos_t, sin_t, sp, c_sink, l, lam_init)
        xs = _post(att_s, qkv_s, xs, mod, w_br_a, w_br_b, w_br_c, w_out, l, DEC_SEQ, 1)
        xs = _ffn(xs, mod, n2, w_ffn_in, w_ffn_out, l, DEC_SEQ, 1)

    fg = final_g.reshape(1, D_MODEL)
    y_prompt = _final_norm(xp, fg).reshape(BATCH, SEQ, D_MODEL)
    y_sample = _final_norm(xs, fg).reshape(DEC_BATCH, DEC_SEQ, D_MODEL)
    return (y_prompt, y_sample) + tuple(jnp.stack(s, axis=1) for s in st)
```

```python
import functools
import math

import jax
import jax.numpy as jnp
from jax import lax
from jax.experimental import pallas as pl
from jax.experimental.pallas import tpu as pltpu

D_MODEL = 1024
BATCH = 16
SEQ = 256
DEPTH = 4
DEC_BATCH = 4
DEC_SEQ = 1024
PAST_LEN = 512
GRID_W = 64
HEAD_DIM = 64
ROPE_THETA = 10000.0
EPS = 1e-6
NEG_INF = -1e30
A_HEADS = 4
WINDOW = 128
D_FF = -(-8 * D_MODEL // (3 * 256)) * 256
N_MOD = 6
D_QKV = 3072
D_IN = 2 * D_QKV
D_ATT = 1536
LANES = 128
SCALE = HEAD_DIM ** -0.5

QA, KA, VA, QB, KB, VB, QC, KC, VC = 0, 512, 1024, 1536, 2048, 2176, 2304, 2816, 2944
R_LQ1, R_LK1, R_LQ2, R_LK2, R_SUBG, R_BQG, R_BKG = range(7)

F32 = jnp.float32
BF16 = jnp.bfloat16
VMEM_LIMIT = 56 * 1024 * 1024


def _cparams(sem):
    return pltpu.CompilerParams(dimension_semantics=sem, vmem_limit_bytes=VMEM_LIMIT)


MOD_TN = 1536


def _mod_kernel(cv_ref, w_ref, b_ref, o_ref):
    cv = cv_ref[...]
    s = (cv * jax.nn.sigmoid(cv)).astype(BF16)
    o_ref[...] = jnp.dot(s, w_ref[...].astype(BF16), preferred_element_type=F32) + b_ref[...]


def _modulation(cv8, w_mod, b_mod):
    n = N_MOD * D_MODEL
    return pl.pallas_call(
        _mod_kernel,
        grid=(DEPTH, n // MOD_TN),
        in_specs=[
            pl.BlockSpec((8, D_MODEL), lambda l, j: (0, 0)),
            pl.BlockSpec((None, D_MODEL, MOD_TN), lambda l, j: (l, 0, j)),
            pl.BlockSpec((None, 1, MOD_TN), lambda l, j: (l, 0, j)),
        ],
        out_specs=pl.BlockSpec((None, 8, MOD_TN), lambda l, j: (l, 0, j)),
        out_shape=jax.ShapeDtypeStruct((DEPTH, 8, n), F32),
        compiler_params=_cparams(("arbitrary", "arbitrary")),
        name="modulation",
    )(cv8, w_mod, b_mod.reshape(DEPTH, 1, n))


NORM_CHUNK = 256


def _norm_mod_rows(x_ref, h_ref, g, sc, sh, rows):
    def body(i, carry):
        r = pl.ds(pl.multiple_of(i * NORM_CHUNK, NORM_CHUNK), NORM_CHUNK)
        x = x_ref[r, :]
        y = x * lax.rsqrt(jnp.mean(x * x, axis=-1, keepdims=True) + EPS) * g
        h_ref[r, :] = (y * (1.0 + sc) + sh).astype(BF16)
        return carry
    lax.fori_loop(0, rows // NORM_CHUNK, body, 0)


def _seg64_sum(x, lane):
    t = x
    for s in (32, 16, 8, 4, 2, 1):
        up = pltpu.roll(t, LANES - s, 1)
        dn = pltpu.roll(t, s, 1)
        t = t + jnp.where((lane & s) == 0, up, dn)
    return t


def _head_rmsnorm(x, g, lane):
    return x * lax.rsqrt(_seg64_sum(x * x, lane) * (1.0 / HEAD_DIM) + EPS) * g


def _rope(x, cos, sin, lane):
    partner = jnp.where((lane & 32) == 0, pltpu.roll(x, LANES - 32, 1), pltpu.roll(x, 32, 1))
    return x * cos + partner * sin


def _lo_hi(x, lane_lo, src_hi):
    other = pltpu.roll(x, 64, 1)
    zero = jnp.zeros_like(x)
    if src_hi:
        return jnp.where(lane_lo, other, zero), jnp.where(lane_lo, zero, x)
    return jnp.where(lane_lo, x, zero), jnp.where(lane_lo, zero, other)


def _dot_nt(a, b):
    return lax.dot_general(a, b, (((1,), (1,)), ((), ())), preferred_element_type=F32)


def _softmax_parts(s, extra=None):
    m = jnp.max(s, axis=-1, keepdims=True)
    if extra is not None:
        m = jnp.maximum(m, extra)
    p = jnp.exp(s - m)
    l = jnp.sum(p, axis=-1, keepdims=True)
    if extra is not None:
        l = l + jnp.exp(extra - m)
    return p, l


def _lam(sp_ref, lam_init):
    dot1 = jnp.sum(sp_ref[R_LQ1:R_LQ1 + 1, :] * sp_ref[R_LK1:R_LK1 + 1, :], axis=-1, keepdims=True)
    dot2 = jnp.sum(sp_ref[R_LQ2:R_LQ2 + 1, :] * sp_ref[R_LK2:R_LK2 + 1, :], axis=-1, keepdims=True)
    return jnp.exp(dot1) - jnp.exp(dot2) + lam_init


def _subln(o, g, lam_init):
    return o * lax.rsqrt(jnp.mean(o * o, axis=-1, keepdims=True) + EPS) * g * (1.0 - lam_init)


def _mod_spec(layer, rows_per_mod, row0, tm):
    return pl.BlockSpec((None, None, 1, N_MOD * D_MODEL),
                        lambda m, *_: (layer, row0 + (m * tm) // rows_per_mod, 0, 0))


PROJ_TM = 2048
PROJ_TN = 512


def _proj_kernel(x_ref, mod_ref, g_ref, w_ref, o_ref, h_scr, *, tm):
    @pl.when(pl.program_id(1) == 0)
    def _():
        _norm_mod_rows(x_ref, h_scr, g_ref[...], mod_ref[:, D_MODEL:2 * D_MODEL],
                       mod_ref[:, 0:D_MODEL], tm)
    o_ref[...] = jnp.dot(h_scr[...], w_ref[...].astype(BF16), preferred_element_type=F32)


def _proj(x, mod, g, w_in, layer, rows_per_mod, row0):
    t = x.shape[0]
    tm = min(PROJ_TM, rows_per_mod)
    return pl.pallas_call(
        functools.partial(_proj_kernel, tm=tm),
        grid=(t // tm, D_IN // PROJ_TN),
        in_specs=[
            pl.BlockSpec((tm, D_MODEL), lambda m, n: (m, 0)),
            _mod_spec(layer, rows_per_mod, row0, tm),
            pl.BlockSpec((None, 1, D_MODEL), lambda m, n: (layer, 0, 0)),
            pl.BlockSpec((None, D_MODEL, PROJ_TN), lambda m, n: (layer, 0, n)),
        ],
        out_specs=pl.BlockSpec((tm, PROJ_TN), lambda m, n: (m, n)),
        out_shape=jax.ShapeDtypeStruct((t, D_IN), F32),
        scratch_shapes=[pltpu.VMEM((tm, D_MODEL), BF16)],
        compiler_params=_cparams(("arbitrary", "arbitrary")),
        name="proj",
    )(x, mod, g, w_in)


def _att_ctx_kernel(qkv_ref, sp_ref, sink_ref, att_ref, kbn_ref, *, layer, lam_init):
    lane = lax.broadcasted_iota(jnp.int32, (SEQ, LANES), 1)
    lane_lo = lane < 64
    lam = _lam(sp_ref, lam_init)
    subg = sp_ref[R_SUBG:R_SUBG + 1, :]
    bqg = sp_ref[R_BQG:R_BQG + 1, :]
    bkg = sp_ref[R_BKG:R_BKG + 1, :]

    for h in range(A_HEADS):
        q = qkv_ref[:, QA + h * LANES:QA + (h + 1) * LANES].astype(BF16)
        k = qkv_ref[:, KA + h * LANES:KA + (h + 1) * LANES]
        v = qkv_ref[:, VA + h * LANES:VA + (h + 1) * LANES].astype(BF16)
        zero = jnp.zeros_like(k)
        kst = jnp.concatenate([jnp.where(lane_lo, k, zero).astype(BF16),
                               jnp.where(lane_lo, zero, k).astype(BF16)], axis=0)
        s = _dot_nt(q, kst) * SCALE
        p1, l1 = _softmax_parts(s[:, :SEQ])
        p2, l2 = _softmax_parts(s[:, SEQ:])
        o1 = jnp.dot(p1.astype(BF16), v, preferred_element_type=F32) / l1
        o2 = jnp.dot(p2.astype(BF16), v, preferred_element_type=F32) / l2
        o = _subln(o1 - lam * o2, subg, lam_init)
        att_ref[:, h * LANES:(h + 1) * LANES] = o.astype(BF16)

    kb = _head_rmsnorm(qkv_ref[:, KB:KB + LANES], bkg, lane)
    kbn_ref[...] = kb
    for mixer in range(2):
        if mixer == 0:
            q0, k_t, v_t, o0 = QB, kb, qkv_ref[:, VB:VB + LANES], 512
        else:
            q0, k_t, v_t, o0 = QC, qkv_ref[:, KC:KC + LANES], qkv_ref[:, VC:VC + LANES], 1024
        for g in range(2):
            k_lo, k_hi = _lo_hi(k_t, lane_lo, g == 1)
            v_lo, v_hi = _lo_hi(v_t, lane_lo, g == 1)
            kst = jnp.concatenate([k_lo.astype(BF16), k_hi.astype(BF16)], axis=0)
            v_lo = v_lo.astype(BF16)
            v_hi = v_hi.astype(BF16)
            for jj in range(2):
                j = 2 * g + jj
                q = qkv_ref[:, q0 + j * LANES:q0 + (j + 1) * LANES]
                if mixer == 0:
                    q = _head_rmsnorm(q, bqg, lane)
                s = _dot_nt(q.astype(BF16), kst) * SCALE
                if mixer == 0:
                    pe, le = _softmax_parts(s[:, :SEQ])
                    po, lo_ = _softmax_parts(s[:, SEQ:])
                else:
                    pe, le = _softmax_parts(s[:, :SEQ], sink_ref[layer, 2 * j])
                    po, lo_ = _softmax_parts(s[:, SEQ:], sink_ref[layer, 2 * j + 1])
                o = (jnp.dot(pe.astype(BF16), v_lo, preferred_element_type=F32)
                     + jnp.dot(po.astype(BF16), v_hi, preferred_element_type=F32))
                o = o / jnp.where(lane_lo, le, lo_)
                att_ref[:, o0 + j * LANES:o0 + (j + 1) * LANES] = o.astype(BF16)


def _att_ctx(qkv, sp, sink, layer, lam_init):
    t = qkv.shape[0]
    return pl.pallas_call(
        functools.partial(_att_ctx_kernel, layer=layer, lam_init=lam_init),
        grid=(t // SEQ,),
        in_specs=[
            pl.BlockSpec((SEQ, D_QKV), lambda b: (b, 0)),
            pl.BlockSpec((None, 8, LANES), lambda b: (layer, 0, 0)),
            pl.BlockSpec(memory_space=pltpu.SMEM),
        ],
        out_specs=[
            pl.BlockSpec((SEQ, D_ATT), lambda b: (b, 0)),
            pl.BlockSpec((SEQ, LANES), lambda b: (b, 0)),
        ],
        out_shape=[jax.ShapeDtypeStruct((t, D_ATT), BF16),
                   jax.ShapeDtypeStruct((t, LANES), F32)],
        compiler_params=_cparams(("arbitrary",)),
        name="att_ctx",
    )(qkv, sp, sink)


LAT_TQ = 256
NK = PAST_LEN + DEC_SEQ
KCH = 512
PREP_ROWS = 256


def _attend_all_queries(q, k_chunks, v_chunks, s_scr, masks=None, sink=None):
    m = None
    for c, k_c in enumerate(k_chunks):
        s = _dot_nt(q, k_c)
        if masks is not None and masks[c] is not None:
            s = jnp.where(masks[c], s, NEG_INF)
        s_scr[:, c * KCH:(c + 1) * KCH] = s
        mc = jnp.max(s, axis=-1, keepdims=True)
        m = mc if m is None else jnp.maximum(m, mc)
    if sink is not None:
        m = jnp.maximum(m, sink)
    l = acc = None
    for c, v_c in enumerate(v_chunks):
        p = jnp.exp(s_scr[:, c * KCH:(c + 1) * KCH] - m)
        lc = jnp.sum(p, axis=-1, keepdims=True)
        pv = jnp.dot(p.astype(BF16), v_c, preferred_element_type=F32)
        l = lc if l is None else l + lc
        acc = pv if acc is None else acc + pv
    if sink is not None:
        l = l + jnp.exp(sink - m)
    return acc, l


def _att_lat_kernel(q_ref, kva_ref, kvv_ref, kv4_ref, kv5_ref,
                    cak_ref, cav_ref, cbk_ref, cbv_ref, cck_ref, ccv_ref,
                    cos_ref, sin_ref, sp_ref, sink_ref,
                    att_ref,
                    ka_scr, va_scr, kb_scr, vb_scr, kc_scr, vc_scr, q_scr, s_scr,
                    *, layer, lam_init):
    u = pl.program_id(1)
    subg = sp_ref[R_SUBG:R_SUBG + 1, :]
    bqg = sp_ref[R_BQG:R_BQG + 1, :]
    bkg = sp_ref[R_BKG:R_BKG + 1, :]
    lane = lax.broadcasted_iota(jnp.int32, (LAT_TQ, LANES), 1)
    lane_lo = lane < 64

    @pl.when(u == 0)
    def _prepare_keys():
        def put(dst, a_k, a_v, pairs):
            lo_rows = pl.ds(pl.multiple_of(dst, PREP_ROWS), PREP_ROWS)
            hi_rows = pl.ds(pl.multiple_of(NK + dst, PREP_ROWS), PREP_ROWS)
            for h in range(A_HEADS):
                k = a_k[h]
                zero = jnp.zeros_like(k)
                ka_scr[h, lo_rows, :] = jnp.where(lane_lo, k, zero).astype(BF16)
                ka_scr[h, hi_rows, :] = jnp.where(lane_lo, zero, k).astype(BF16)
                va_scr[h, lo_rows, :] = a_v[h].astype(BF16)
            for x, scr in pairs:
                for g in range(2):
                    lo, hi = _lo_hi(x, lane_lo, g == 1)
                    scr[g, lo_rows, :] = lo.astype(BF16)
                    scr[g, hi_rows, :] = hi.astype(BF16)

        def cached(i, carry):
            r = pl.ds(pl.multiple_of(i * PREP_ROWS, PREP_ROWS), PREP_ROWS)
            put(i * PREP_ROWS,
                [cak_ref[r, h * LANES:(h + 1) * LANES] for h in range(A_HEADS)],
                [cav_ref[r, h * LANES:(h + 1) * LANES] for h in range(A_HEADS)],
                ((cbk_ref[r, :], kb_scr), (cbv_ref[r, :], vb_scr),
                 (cck_ref[r, :], kc_scr), (ccv_ref[r, :], vc_scr)))
            return carry
        lax.fori_loop(0, PAST_LEN // PREP_ROWS, cached, 0)

        def latent(i, carry):
            r = pl.ds(pl.multiple_of(i * PREP_ROWS, PREP_ROWS), PREP_ROWS)
            cos = cos_ref[r, :]
            sin = sin_ref[r, :]
            kb = _rope(_head_rmsnorm(kv4_ref[r, 0:LANES], bkg, lane), cos, sin, lane)
            kc = _rope(kv5_ref[r, 256:256 + LANES], cos, sin, lane)
            put(PAST_LEN + i * PREP_ROWS,
                [_rope(kva_ref[r, h * LANES:(h + 1) * LANES], cos, sin, lane) for h in range(A_HEADS)],
                [kvv_ref[r, h * LANES:(h + 1) * LANES] for h in range(A_HEADS)],
                ((kb, kb_scr), (kv4_ref[r, LANES:2 * LANES], vb_scr),
                 (kc, kc_scr), (kv5_ref[r, 384:384 + LANES], vc_scr)))
            return carry
        lax.fori_loop(0, DEC_SEQ // PREP_ROWS, latent, 0)

    lam = _lam(sp_ref, lam_init)
    lane_lo_all = lax.broadcasted_iota(jnp.int32, (DEC_SEQ, LANES), 1) < 64
    chunks = lambda scr, i, base: [scr[i, base + c * KCH:base + (c + 1) * KCH, :] for c in range(NK // KCH)]

    def prepare_queries(normalise):
        def step(i, carry):
            r = pl.ds(pl.multiple_of(i * LAT_TQ, LAT_TQ), LAT_TQ)
            q = q_ref[r, :]
            if normalise:
                q = _head_rmsnorm(q, bqg, lane)
            q_scr[r, :] = (_rope(q, cos_ref[r, :], sin_ref[r, :], lane) * SCALE).astype(BF16)
            return carry
        lax.fori_loop(0, DEC_SEQ // LAT_TQ, step, 0)

    @pl.when(u < A_HEADS)
    def _mixer_a():
        prepare_queries(False)
        q = q_scr[...]
        v = chunks(va_scr, u, 0)
        a1, l1 = _attend_all_queries(q, chunks(ka_scr, u, 0), v, s_scr.at[0])
        a2, l2 = _attend_all_queries(q, chunks(ka_scr, u, NK), v, s_scr.at[1])
        att_ref[...] = _subln(a1 / l1 - lam * (a2 / l2), subg, lam_init).astype(BF16)

    @pl.when((u >= A_HEADS) & (u < A_HEADS + 4))
    def _mixer_b():
        g = lax.shift_right_logical(u - A_HEADS, 1)
        prepare_queries(True)
        q = q_scr[...]
        ae, le = _attend_all_queries(q, chunks(kb_scr, g, 0), chunks(vb_scr, g, 0), s_scr.at[0])
        ao, lo_ = _attend_all_queries(q, chunks(kb_scr, g, NK), chunks(vb_scr, g, NK), s_scr.at[1])
        att_ref[...] = ((ae + ao) / jnp.where(lane_lo_all, le, lo_)).astype(BF16)

    @pl.when(u >= A_HEADS + 4)
    def _mixer_c():
        j = u - (A_HEADS + 4)
        g = lax.shift_right_logical(j, 1)
        prepare_queries(False)
        q = q_scr[...]
        qpos = lax.broadcasted_iota(jnp.int32, (DEC_SEQ, KCH), 0)
        kcol = lax.broadcasted_iota(jnp.int32, (DEC_SEQ, KCH), 1)
        masks = (None,) + tuple(jnp.abs(kcol + (c * KCH - PAST_LEN) - qpos) <= WINDOW
                                for c in range(PAST_LEN // KCH, NK // KCH))
        parts = [_attend_all_queries(q, chunks(kc_scr, g, half * NK), chunks(vc_scr, g, half * NK),
                                     s_scr.at[half], masks=masks, sink=sink_ref[layer, 2 * j + half])
                 for half in range(2)]
        o = (parts[0][0] + parts[1][0]) / jnp.where(lane_lo_all, parts[0][1], parts[1][1])
        att_ref[...] = o.astype(BF16)


N_UNITS = 12


def _unit_q_col(u):
    return jnp.where(u < A_HEADS, u, jnp.where(u < A_HEADS + 4, QB // LANES - A_HEADS + u,
                                               QC // LANES - A_HEADS - 4 + u))


def _att_lat(qkv, caches, cos, sin, sp, sink, layer, lam_init):
    t = qkv.shape[0]
    once = pl.Buffered(1)
    kv_blk = lambda col: pl.BlockSpec((DEC_SEQ, 512), lambda b, i: (b, col), pipeline_mode=once)
    cache_blk = lambda w: pl.BlockSpec((None, None, PAST_LEN, w), lambda b, i: (b, layer, 0, 0),
                                       pipeline_mode=once)
    return pl.pallas_call(
        functools.partial(_att_lat_kernel, layer=layer, lam_init=lam_init),
        grid=(DEC_BATCH, N_UNITS),
        in_specs=[
            pl.BlockSpec((DEC_SEQ, LANES), lambda b, i: (b, _unit_q_col(i))),
            kv_blk(KA // 512), kv_blk(VA // 512), kv_blk(4), kv_blk(5),
            cache_blk(512), cache_blk(512), cache_blk(LANES), cache_blk(LANES),
            cache_blk(LANES), cache_blk(LANES),
            pl.BlockSpec((DEC_SEQ, LANES), lambda b, i: (0, 0)),
            pl.BlockSpec((DEC_SEQ, LANES), lambda b, i: (0, 0)),
            pl.BlockSpec((None, 8, LANES), lambda b, i: (layer, 0, 0)),
            pl.BlockSpec(memory_space=pltpu.SMEM),
        ],
        out_specs=pl.BlockSpec((DEC_SEQ, LANES), lambda b, i: (b, i)),
        out_shape=jax.ShapeDtypeStruct((t, D_ATT), BF16),
        scratch_shapes=[
            pltpu.VMEM((A_HEADS, 2 * NK, LANES), BF16),
            pltpu.VMEM((A_HEADS, NK, LANES), BF16),
            pltpu.VMEM((2, 2 * NK, LANES), BF16),
            pltpu.VMEM((2, 2 * NK, LANES), BF16),
            pltpu.VMEM((2, 2 * NK, LANES), BF16),
            pltpu.VMEM((2, 2 * NK, LANES), BF16),
            pltpu.VMEM((DEC_SEQ, LANES), BF16),
            pltpu.VMEM((2, DEC_SEQ, NK), F32),
        ],
        compiler_params=_cparams(("arbitrary", "arbitrary")),
        name="att_lat",
    )(qkv, qkv, qkv, qkv, qkv, *caches, cos, sin, sp, sink)


POST_TM = 512


def _post_kernel(att_ref, gate_ref, x_ref, mod_ref, wa_ref, wb_ref, wc_ref, wo_ref, o_ref):
    merged = None
    for i, w_ref in enumerate((wa_ref, wb_ref, wc_ref)):
        y = jnp.dot(att_ref[:, i * 512:(i + 1) * 512], w_ref[...].astype(BF16),
                    preferred_element_type=F32)
        term = jax.nn.sigmoid(gate_ref[:, i * D_MODEL:(i + 1) * D_MODEL]) * y
        merged = term if merged is None else merged + term
    mixed = jnp.dot(merged.astype(BF16), wo_ref[...].astype(BF16), preferred_element_type=F32)
    o_ref[...] = x_ref[...] + mod_ref[:, 2 * D_MODEL:3 * D_MODEL] * mixed


def _post(att, qkvg, x, mod, wa, wb, wc, wo, layer, rows_per_mod, row0):
    t = x.shape[0]
    tm = POST_TM
    full = lambda r: pl.BlockSpec((None, r, D_MODEL), lambda m: (layer, 0, 0))
    return pl.pallas_call(
        _post_kernel,
        grid=(t // tm,),
        in_specs=[
            pl.BlockSpec((tm, D_ATT), lambda m: (m, 0)),
            pl.BlockSpec((tm, D_QKV), lambda m: (m, 1)),
            pl.BlockSpec((tm, D_MODEL), lambda m: (m, 0)),
            _mod_spec(layer, rows_per_mod, row0, tm),
            full(512), full(512), full(512), full(D_MODEL),
        ],
        out_specs=pl.BlockSpec((tm, D_MODEL), lambda m: (m, 0)),
        out_shape=jax.ShapeDtypeStruct((t, D_MODEL), F32),
        compiler_params=_cparams(("arbitrary",)),
        name="post",
    )(att, qkvg, x, mod, wa, wb, wc, wo)


FFN_TM = 1024
FFN_TF = 256


def _ffn_kernel(x_ref, mod_ref, g_ref, wa_ref, wb_ref, wo_ref, o_ref, h_scr, acc_scr):
    f = pl.program_id(1)

    @pl.when(f == 0)
    def _():
        _norm_mod_rows(x_ref, h_scr, g_ref[...], mod_ref[:, 4 * D_MODEL:5 * D_MODEL],
                       mod_ref[:, 3 * D_MODEL:4 * D_MODEL], FFN_TM)
        acc_scr[...] = jnp.zeros_like(acc_scr)

    h = h_scr[...]
    a = jnp.dot(h, wa_ref[...].astype(BF16), preferred_element_type=F32)
    b = jnp.dot(h, wb_ref[...].astype(BF16), preferred_element_type=F32)
    y = (a * jax.nn.sigmoid(a) * b).astype(BF16)
    acc_scr[...] += jnp.dot(y, wo_ref[...].astype(BF16), preferred_element_type=F32)

    @pl.when(f == pl.num_programs(1) - 1)
    def _():
        o_ref[...] = x_ref[...] + mod_ref[:, 5 * D_MODEL:6 * D_MODEL] * acc_scr[...]


def _ffn(x, mod, g, w_in, w_out, layer, rows_per_mod, row0):
    t = x.shape[0]
    tm = FFN_TM
    nf = D_FF // FFN_TF
    return pl.pallas_call(
        _ffn_kernel,
        grid=(t // tm, nf),
        in_specs=[
            pl.BlockSpec((tm, D_MODEL), lambda m, f: (m, 0)),
            _mod_spec(layer, rows_per_mod, row0, tm),
            pl.BlockSpec((None, 1, D_MODEL), lambda m, f: (layer, 0, 0)),
            pl.BlockSpec((None, D_MODEL, FFN_TF), lambda m, f: (layer, 0, f)),
            pl.BlockSpec((None, D_MODEL, FFN_TF), lambda m, f: (layer, 0, f + nf)),
            pl.BlockSpec((None, FFN_TF, D_MODEL), lambda m, f: (layer, f, 0)),
        ],
        out_specs=pl.BlockSpec((tm, D_MODEL), lambda m, f: (m, 0)),
        out_shape=jax.ShapeDtypeStruct((t, D_MODEL), F32),
        scratch_shapes=[pltpu.VMEM((tm, D_MODEL), BF16), pltpu.VMEM((tm, D_MODEL), F32)],
        compiler_params=_cparams(("arbitrary", "arbitrary")),
        name="ffn",
    )(x, mod, g, w_in, w_in, w_out)


FINAL_TM = 512


def _final_kernel(x_ref, g_ref, o_ref):
    x = x_ref[...]
    o_ref[...] = x * lax.rsqrt(jnp.mean(x * x, axis=-1, keepdims=True) + EPS) * g_ref[...]


def _final_norm(x, g):
    t = x.shape[0]
    return pl.pallas_call(
        _final_kernel,
        grid=(t // FINAL_TM,),
        in_specs=[pl.BlockSpec((FINAL_TM, D_MODEL), lambda m: (m, 0)),
                  pl.BlockSpec((1, D_MODEL), lambda m: (0, 0))],
        out_specs=pl.BlockSpec((FINAL_TM, D_MODEL), lambda m: (m, 0)),
        out_shape=jax.ShapeDtypeStruct((t, D_MODEL), F32),
        compiler_params=_cparams(("arbitrary",)),
        name="final_norm",
    )(x, g)


def _rope_tables():
    rows = DEC_SEQ // GRID_W
    row = jnp.repeat(jnp.arange(rows), GRID_W).astype(F32)
    col = jnp.tile(jnp.arange(GRID_W), rows).astype(F32)
    n = HEAD_DIM // 4
    inv = ROPE_THETA ** (-jnp.arange(n, dtype=F32) / n)
    ang = jnp.concatenate([row[:, None] * inv, col[:, None] * inv], axis=-1)
    cos, sin = jnp.cos(ang), jnp.sin(ang)
    cos_t = jnp.tile(cos, (1, 4))
    sin_t = jnp.tile(jnp.concatenate([-sin, sin], axis=-1), (1, 2))
    return cos_t, sin_t


def _pack_small(a_lam_q1, a_lam_k1, a_lam_q2, a_lam_k2, a_subln_g, b_qnorm_g, b_knorm_g):
    pad = lambda v: jnp.pad(v, ((0, 0), (0, LANES - HEAD_DIM)))
    rows = [pad(a_lam_q1), pad(a_lam_k1), pad(a_lam_q2), pad(a_lam_k2), a_subln_g,
            jnp.tile(b_qnorm_g, (1, 2)), jnp.tile(b_knorm_g, (1, 2)),
            jnp.zeros((DEPTH, LANES), F32)]
    return jnp.stack(rows, axis=1)


def kernel(x_prompt, x_sample, cache_a_k, cache_a_v, cache_b_k, cache_b_v, cache_c_k, cache_c_v, c, c_ctx, w_mod, b_mod, norm1_g, norm2_g, w_in, a_lam_q1, a_lam_k1, a_lam_q2, a_lam_k2, a_subln_g, b_qnorm_g, b_knorm_g, c_sink, w_br_a, w_br_b, w_br_c, w_out, w_ffn_in, w_ffn_out, final_g):
    t_ctx = BATCH * SEQ
    t_lat = DEC_BATCH * DEC_SEQ
    xp = x_prompt.reshape(t_ctx, D_MODEL)
    xs = x_sample.reshape(t_lat, D_MODEL)
    cv8 = jnp.concatenate([c_ctx[None, :], c, jnp.zeros((8 - 1 - DEC_BATCH, D_MODEL), F32)], axis=0)
    mod = _modulation(cv8, w_mod, b_mod).reshape(DEPTH, 8, 1, N_MOD * D_MODEL)
    cos_t, sin_t = _rope_tables()
    sp = _pack_small(a_lam_q1, a_lam_k1, a_lam_q2, a_lam_k2, a_subln_g, b_qnorm_g, b_knorm_g)
    n1 = norm1_g.reshape(DEPTH, 1, D_MODEL)
    n2 = norm2_g.reshape(DEPTH, 1, D_MODEL)
    caches = (cache_a_k.reshape(DEC_BATCH, DEPTH, PAST_LEN, 512),
              cache_a_v.reshape(DEC_BATCH, DEPTH, PAST_LEN, 512),
              cache_b_k.reshape(DEC_BATCH, DEPTH, PAST_LEN, LANES),
              cache_b_v.reshape(DEC_BATCH, DEPTH, PAST_LEN, LANES),
              cache_c_k.reshape(DEC_BATCH, DEPTH, PAST_LEN, LANES),
              cache_c_v.reshape(DEC_BATCH, DEPTH, PAST_LEN, LANES))
    st = [[] for _ in range(6)]
    for l in range(DEPTH):
        lam_init = 0.8 - 0.6 * math.exp(-0.3 * l)

        qkv_c = _proj(xp, mod, n1, w_in, l, t_ctx, 0)
        att_c, kbn = _att_ctx(qkv_c, sp, c_sink, l, lam_init)
        xp = _post(att_c, qkv_c, xp, mod, w_br_a, w_br_b, w_br_c, w_out, l, t_ctx, 0)
        xp = _ffn(xp, mod, n2, w_ffn_in, w_ffn_out, l, t_ctx, 0)
        st[0].append(qkv_c[:, KA:KA + 512].reshape(BATCH, SEQ, A_HEADS, 2 * HEAD_DIM))
        st[1].append(qkv_c[:, VA:VA + 512].reshape(BATCH, SEQ, A_HEADS, 2 * HEAD_DIM))
        st[2].append(kbn.reshape(BATCH, SEQ, 2, HEAD_DIM))
        st[3].append(qkv_c[:, VB:VB + LANES].reshape(BATCH, SEQ, 2, HEAD_DIM))
        st[4].append(qkv_c[:, KC:KC + LANES].reshape(BATCH, SEQ, 2, HEAD_DIM))
        st[5].append(qkv_c[:, VC:VC + LANES].reshape(BATCH, SEQ, 2, HEAD_DIM))

        qkv_s = _proj(xs, mod, n1, w_in, l, DEC_SEQ, 1)
        att_s = _att_lat(qkv_s, caches, cos_t, sin_t, sp, c_sink, l, lam_init)
        xs = _post(att_s, qkv_s, xs, mod, w_br_a, w_br_b, w_br_c, w_out, l, DEC_SEQ, 1)
        xs = _ffn(xs, mod, n2, w_ffn_in, w_ffn_out, l, DEC_SEQ, 1)

    fg = final_g.reshape(1, D_MODEL)
    y_prompt = _final_norm(xp, fg).reshape(BATCH, SEQ, D_MODEL)
    y_sample = _final_norm(xs, fg).reshape(DEC_BATCH, DEC_SEQ, D_MODEL)
    return (y_prompt, y_sample) + tuple(jnp.stack(s, axis=1) for s in st)
```

```python
import functools
import math

import jax
import jax.numpy as jnp
from jax import lax
from jax.experimental import pallas as pl
from jax.experimental.pallas import tpu as pltpu

D_MODEL = 1024
BATCH = 16
SEQ = 256
DEPTH = 4
DEC_BATCH = 4
DEC_SEQ = 1024
PAST_LEN = 512
GRID_W = 64
HEAD_DIM = 64
ROPE_THETA = 10000.0
EPS = 1e-6
NEG_INF = -1e30
A_HEADS = 4
WINDOW = 128
D_FF = -(-8 * D_MODEL // (3 * 256)) * 256
N_MOD = 6
D_QKV = 3072
D_GATE = 3072
D_ATT = 1536
LANES = 128
LOG2E = math.log2(math.e)
QSCALE = HEAD_DIM ** -0.5 * LOG2E

QA, KA, VA, QB, KB, VB, QC, KC, VC = 0, 512, 1024, 1536, 2048, 2176, 2304, 2816, 2944
R_LQ1, R_LK1, R_LQ2, R_LK2, R_SUBG, R_BQG, R_BKG = range(7)
M_SH1, M_SC1, M_G1, M_SH2, M_SC2, M_G2 = range(6)

F32 = jnp.float32
BF16 = jnp.bfloat16
VMEM_LIMIT = 56 * 1024 * 1024


def _cparams(sem):
    return pltpu.CompilerParams(dimension_semantics=sem, vmem_limit_bytes=VMEM_LIMIT)


MOD_TN = 1536


def _mod_kernel(cv_ref, w_ref, b_ref, o_ref):
    cv = cv_ref[...]
    s = (cv * jax.nn.sigmoid(cv)).astype(BF16)
    o_ref[...] = jnp.dot(s, w_ref[...].astype(BF16), preferred_element_type=F32) + b_ref[...]


def _modulation(cv8, w_mod, b_mod):
    n = N_MOD * D_MODEL
    return pl.pallas_call(
        _mod_kernel,
        grid=(DEPTH, n // MOD_TN),
        in_specs=[
            pl.BlockSpec((8, D_MODEL), lambda l, j: (0, 0)),
            pl.BlockSpec((None, D_MODEL, MOD_TN), lambda l, j: (l, 0, j)),
            pl.BlockSpec((None, 1, MOD_TN), lambda l, j: (l, 0, j)),
        ],
        out_specs=pl.BlockSpec((None, 8, MOD_TN), lambda l, j: (l, 0, j)),
        out_shape=jax.ShapeDtypeStruct((DEPTH, 8, n), F32),
        compiler_params=_cparams(("arbitrary", "arbitrary")),
        name="modulation",
    )(cv8, w_mod, b_mod.reshape(DEPTH, 1, n))


NORM_CHUNK = 256


def _mod_block(mod_ref, row, blk):
    return mod_ref[pl.ds(row, 1), blk * D_MODEL:(blk + 1) * D_MODEL]


def _norm_mod_rows(x_ref, h_ref, g, mod_ref, sc_blk, sh_blk, rows, mod_row0, rows_per_mod):
    def body(i, carry):
        r = pl.ds(pl.multiple_of(i * NORM_CHUNK, NORM_CHUNK), NORM_CHUNK)
        row = mod_row0 + lax.div(i * NORM_CHUNK, rows_per_mod)
        x = x_ref[r, :]
        y = x * lax.rsqrt(jnp.mean(x * x, axis=-1, keepdims=True) + EPS) * g
        h_ref[r, :] = (y * (1.0 + _mod_block(mod_ref, row, sc_blk))
                       + _mod_block(mod_ref, row, sh_blk)).astype(BF16)
        return carry
    lax.fori_loop(0, rows // NORM_CHUNK, body, 0)


def _head_rmsnorm(x, g, lane_lo):
    x2 = x * x
    zero = jnp.zeros_like(x2)
    lo = jnp.sum(jnp.where(lane_lo, x2, zero), axis=-1, keepdims=True)
    hi = jnp.sum(jnp.where(lane_lo, zero, x2), axis=-1, keepdims=True)
    ms = jnp.where(lane_lo, lo, hi) * (1.0 / HEAD_DIM)
    return x * lax.rsqrt(ms + EPS) * g


def _rope(x, cos, sin, lane):
    partner = jnp.where((lane & 32) == 0, pltpu.roll(x, LANES - 32, 1), pltpu.roll(x, 32, 1))
    return x * cos + partner * sin


def _lo_hi(x, lane, src_hi, ones_lane=False):
    lane_lo = lane < 64
    other = pltpu.roll(x, 64, 1)
    zero = jnp.zeros_like(x)
    lo = jnp.where(lane_lo, other if src_hi else x, zero)
    hi = jnp.where(lane_lo, zero, x if src_hi else other)
    if ones_lane:
        lo = jnp.where(lane == 64, 1.0, lo)
        hi = jnp.where(lane == 0, 1.0, hi)
    return lo, hi


def _dot_nt(a, b):
    return lax.dot_general(a, b, (((1,), (1,)), ((), ())), preferred_element_type=F32)


def _probs(s, extra=None):
    m = jnp.max(s, axis=-1, keepdims=True)
    if extra is not None:
        m = jnp.maximum(m, extra)
    return jnp.exp2(s - m), m


def _lam(sp_ref, lam_init):
    dot1 = jnp.sum(sp_ref[R_LQ1:R_LQ1 + 1, :] * sp_ref[R_LK1:R_LK1 + 1, :], axis=-1, keepdims=True)
    dot2 = jnp.sum(sp_ref[R_LQ2:R_LQ2 + 1, :] * sp_ref[R_LK2:R_LK2 + 1, :], axis=-1, keepdims=True)
    return jnp.exp(dot1) - jnp.exp(dot2) + lam_init


def _subln(o, g, lam_init):
    return o * lax.rsqrt(jnp.mean(o * o, axis=-1, keepdims=True) + EPS) * g * (1.0 - lam_init)


def _mod_spec(layer):
    return pl.BlockSpec((None, 8, N_MOD * D_MODEL), lambda *_: (layer, 0, 0))


PROJ_TM = 2048
PROJ_TN = 512
N_QKV_TILES = D_QKV // PROJ_TN


def _proj_kernel(x_ref, mod_ref, g_ref, w_ref, qkv_ref, gate_ref, h_scr, *, tm, rows_per_mod, row0):
    m = pl.program_id(0)
    n = pl.program_id(1)

    @pl.when(n == 0)
    def _():
        _norm_mod_rows(x_ref, h_scr, g_ref[...], mod_ref, M_SC1, M_SH1, tm,
                       row0 + lax.div(m * tm, rows_per_mod), rows_per_mod)

    y = jnp.dot(h_scr[...], w_ref[...].astype(BF16), preferred_element_type=F32)

    @pl.when(n < N_QKV_TILES)
    def _():
        qkv_ref[...] = y

    @pl.when(n >= N_QKV_TILES)
    def _():
        gate_ref[...] = y.astype(BF16)


def _proj(x, mod, g, w_in, layer, rows_per_mod, row0):
    t = x.shape[0]
    tm = PROJ_TM
    return pl.pallas_call(
        functools.partial(_proj_kernel, tm=tm, rows_per_mod=rows_per_mod, row0=row0),
        grid=(t // tm, (D_QKV + D_GATE) // PROJ_TN),
        in_specs=[
            pl.BlockSpec((tm, D_MODEL), lambda m, n: (m, 0)),
            _mod_spec(layer),
            pl.BlockSpec((None, 1, D_MODEL), lambda m, n: (layer, 0, 0)),
            pl.BlockSpec((None, D_MODEL, PROJ_TN), lambda m, n: (layer, 0, n)),
        ],
        out_specs=[
            pl.BlockSpec((tm, PROJ_TN), lambda m, n: (m, jnp.minimum(n, N_QKV_TILES - 1))),
            pl.BlockSpec((tm, PROJ_TN), lambda m, n: (m, jnp.maximum(n - N_QKV_TILES, 0))),
        ],
        out_shape=[jax.ShapeDtypeStruct((t, D_QKV), F32),
                   jax.ShapeDtypeStruct((t, D_GATE), BF16)],
        scratch_shapes=[pltpu.VMEM((tm, D_MODEL), BF16)],
        compiler_params=_cparams(("arbitrary", "arbitrary")),
        name="proj",
    )(x, mod, g, w_in)


def _att_ctx_kernel(qkv_ref, sp_ref, sink_ref, att_ref, ka_ref, va_ref, kbn_ref, *, layer, lam_init):
    lane = lax.broadcasted_iota(jnp.int32, (SEQ, LANES), 1)
    lane_lo = lane < 64
    lam = _lam(sp_ref, lam_init)
    subg = sp_ref[R_SUBG:R_SUBG + 1, :]
    bqg = sp_ref[R_BQG:R_BQG + 1, :]
    bkg = sp_ref[R_BKG:R_BKG + 1, :]

    for h in range(A_HEADS):
        q = (qkv_ref[:, QA + h * LANES:QA + (h + 1) * LANES] * QSCALE).astype(BF16)
        k = qkv_ref[:, KA + h * LANES:KA + (h + 1) * LANES]
        v = qkv_ref[:, VA + h * LANES:VA + (h + 1) * LANES]
        ka_ref[pl.ds(h, SEQ, stride=A_HEADS), :] = k
        va_ref[pl.ds(h, SEQ, stride=A_HEADS), :] = v
        v = v.astype(BF16)
        zero = jnp.zeros_like(k)
        kst = jnp.concatenate([jnp.where(lane_lo, k, zero).astype(BF16),
                               jnp.where(lane_lo, zero, k).astype(BF16)], axis=0)
        s = _dot_nt(q, kst)
        p1, _ = _probs(s[:, :SEQ])
        p2, _ = _probs(s[:, SEQ:])
        o1 = jnp.dot(p1.astype(BF16), v, preferred_element_type=F32) / jnp.sum(p1, axis=-1, keepdims=True)
        o2 = jnp.dot(p2.astype(BF16), v, preferred_element_type=F32) / jnp.sum(p2, axis=-1, keepdims=True)
        o = _subln(o1 - lam * o2, subg, lam_init)
        att_ref[:, h * LANES:(h + 1) * LANES] = o.astype(BF16)

    kb = _head_rmsnorm(qkv_ref[:, KB:KB + LANES], bkg, lane_lo)
    kbn_ref[...] = kb
    for mixer in range(2):
        if mixer == 0:
            q0, k_t, v_t, o0 = QB, kb, qkv_ref[:, VB:VB + LANES], 512
        else:
            q0, k_t, v_t, o0 = QC, qkv_ref[:, KC:KC + LANES], qkv_ref[:, VC:VC + LANES], 1024
        for g in range(2):
            k_lo, k_hi = _lo_hi(k_t, lane, g == 1)
            v_lo, v_hi = _lo_hi(v_t, lane, g == 1, ones_lane=True)
            kst = jnp.concatenate([k_lo.astype(BF16), k_hi.astype(BF16)], axis=0)
            v_lo = v_lo.astype(BF16)
            v_hi = v_hi.astype(BF16)
            for jj in range(2):
                j = 2 * g + jj
                q = qkv_ref[:, q0 + j * LANES:q0 + (j + 1) * LANES]
                if mixer == 0:
                    q = _head_rmsnorm(q, bqg, lane_lo)
                s = _dot_nt((q * QSCALE).astype(BF16), kst)
                if mixer == 0:
                    pe, _ = _probs(s[:, :SEQ])
                    po, _ = _probs(s[:, SEQ:])
                else:
                    sink_e = sink_ref[layer, 2 * j] * LOG2E
                    sink_o = sink_ref[layer, 2 * j + 1] * LOG2E
                    pe, me = _probs(s[:, :SEQ], sink_e)
                    po, mo = _probs(s[:, SEQ:], sink_o)
                oe = jnp.dot(pe.astype(BF16), v_lo, preferred_element_type=F32)
                oo = jnp.dot(po.astype(BF16), v_hi, preferred_element_type=F32)
                le = oe[:, 64:65]
                lo_ = oo[:, 0:1]
                if mixer == 1:
                    le = le + jnp.exp2(sink_e - me)
                    lo_ = lo_ + jnp.exp2(sink_o - mo)
                o = jnp.where(lane_lo, oe, oo) / jnp.where(lane_lo, le, lo_)
                att_ref[:, o0 + j * LANES:o0 + (j + 1) * LANES] = o.astype(BF16)


def _att_ctx(qkv, sp, sink, layer, lam_init):
    t = qkv.shape[0]
    nb = t // SEQ
    return pl.pallas_call(
        functools.partial(_att_ctx_kernel, layer=layer, lam_init=lam_init),
        grid=(nb,),
        in_specs=[
            pl.BlockSpec((SEQ, D_QKV), lambda b: (b, 0)),
            pl.BlockSpec((None, 8, LANES), lambda b: (layer, 0, 0)),
            pl.BlockSpec(memory_space=pltpu.SMEM),
        ],
        out_specs=[
            pl.BlockSpec((SEQ, D_ATT), lambda b: (b, 0)),
            pl.BlockSpec((None, SEQ * A_HEADS, LANES), lambda b: (b, 0, 0)),
            pl.BlockSpec((None, SEQ * A_HEADS, LANES), lambda b: (b, 0, 0)),
            pl.BlockSpec((SEQ, LANES), lambda b: (b, 0)),
        ],
        out_shape=[jax.ShapeDtypeStruct((t, D_ATT), BF16),
                   jax.ShapeDtypeStruct((nb, SEQ * A_HEADS, LANES), F32),
                   jax.ShapeDtypeStruct((nb, SEQ * A_HEADS, LANES), F32),
                   jax.ShapeDtypeStruct((t, LANES), F32)],
        compiler_params=_cparams(("arbitrary",)),
        name="att_ctx",
    )(qkv, sp, sink)


LAT_TQ = 256
NK = PAST_LEN + DEC_SEQ
KCH = 512
PREP_ROWS = 256
N_UNITS = 12


def _attend_all_queries(q, k_chunks, v_chunks, masks=None, sink=None, l_lane=None):
    m = l = acc = None
    for c, (k_c, v_c) in enumerate(zip(k_chunks, v_chunks)):
        s = _dot_nt(q, k_c)
        if masks is not None and masks[c] is not None:
            s = jnp.where(masks[c], s, NEG_INF)
        mc = jnp.max(s, axis=-1, keepdims=True)
        if m is None:
            m_new = mc if sink is None else jnp.maximum(mc, sink)
            p = jnp.exp2(s - m_new)
            acc = jnp.dot(p.astype(BF16), v_c, preferred_element_type=F32)
            if l_lane is None:
                l = jnp.sum(p, axis=-1, keepdims=True)
        else:
            m_new = jnp.maximum(m, mc)
            alpha = jnp.exp2(m - m_new)
            p = jnp.exp2(s - m_new)
            acc = alpha * acc + jnp.dot(p.astype(BF16), v_c, preferred_element_type=F32)
            if l_lane is None:
                l = alpha * l + jnp.sum(p, axis=-1, keepdims=True)
        m = m_new
    if l_lane is not None:
        l = acc[:, l_lane:l_lane + 1]
    if sink is not None:
        l = l + jnp.exp2(sink - m)
    return acc, l


def _att_lat_kernel(q_ref, kva_ref, kvv_ref, kv4_ref, kv5_ref,
                    cak_ref, cav_ref, cbk_ref, cbv_ref, cck_ref, ccv_ref,
                    cos_ref, sin_ref, sp_ref, sink_ref,
                    att_ref,
                    ka_scr, va_scr, kb_scr, vb_scr, kc_scr, vc_scr, q_scr,
                    *, layer, lam_init):
    u = pl.program_id(1)
    subg = sp_ref[R_SUBG:R_SUBG + 1, :]
    bqg = sp_ref[R_BQG:R_BQG + 1, :]
    bkg = sp_ref[R_BKG:R_BKG + 1, :]
    lane = lax.broadcasted_iota(jnp.int32, (LAT_TQ, LANES), 1)
    lane_lo = lane < 64

    @pl.when(u == 0)
    def _prepare_keys():
        def put(dst, a_k, a_v, pairs):
            lo_rows = pl.ds(pl.multiple_of(dst, PREP_ROWS), PREP_ROWS)
            hi_rows = pl.ds(pl.multiple_of(NK + dst, PREP_ROWS), PREP_ROWS)
            for h in range(A_HEADS):
                k = a_k[h]
                zero = jnp.zeros_like(k)
                ka_scr[h, lo_rows, :] = jnp.where(lane_lo, k, zero).astype(BF16)
                ka_scr[h, hi_rows, :] = jnp.where(lane_lo, zero, k).astype(BF16)
                va_scr[h, lo_rows, :] = a_v[h].astype(BF16)
            for x, scr, is_value in pairs:
                for g in range(2):
                    lo, hi = _lo_hi(x, lane, g == 1, ones_lane=is_value)
                    scr[g, lo_rows, :] = lo.astype(BF16)
                    scr[g, hi_rows, :] = hi.astype(BF16)

        def cached(i, carry):
            r = pl.ds(pl.multiple_of(i * PREP_ROWS, PREP_ROWS), PREP_ROWS)
            put(i * PREP_ROWS,
                [cak_ref[r, h * LANES:(h + 1) * LANES] for h in range(A_HEADS)],
                [cav_ref[r, h * LANES:(h + 1) * LANES] for h in range(A_HEADS)],
                ((cbk_ref[r, :], kb_scr, False), (cbv_ref[r, :], vb_scr, True),
                 (cck_ref[r, :], kc_scr, False), (ccv_ref[r, :], vc_scr, True)))
            return carry
        lax.fori_loop(0, PAST_LEN // PREP_ROWS, cached, 0)

        def latent(i, carry):
            r = pl.ds(pl.multiple_of(i * PREP_ROWS, PREP_ROWS), PREP_ROWS)
            cos = cos_ref[r, :]
            sin = sin_ref[r, :]
            kb = _rope(_head_rmsnorm(kv4_ref[r, 0:LANES], bkg, lane_lo), cos, sin, lane)
            kc = _rope(kv5_ref[r, 256:256 + LANES], cos, sin, lane)
            put(PAST_LEN + i * PREP_ROWS,
                [_rope(kva_ref[r, h * LANES:(h + 1) * LANES], cos, sin, lane) for h in range(A_HEADS)],
                [kvv_ref[r, h * LANES:(h + 1) * LANES] for h in range(A_HEADS)],
                ((kb, kb_scr, False), (kv4_ref[r, LANES:2 * LANES], vb_scr, True),
                 (kc, kc_scr, False), (kv5_ref[r, 384:384 + LANES], vc_scr, True)))
            return carry
        lax.fori_loop(0, DEC_SEQ // PREP_ROWS, latent, 0)

    lam = _lam(sp_ref, lam_init)
    lane_lo_all = lax.broadcasted_iota(jnp.int32, (DEC_SEQ, LANES), 1) < 64
    chunks = lambda scr, i, base: [scr[i, base + c * KCH:base + (c + 1) * KCH, :] for c in range(NK // KCH)]

    def prepare_queries(normalise):
        def step(i, carry):
            r = pl.ds(pl.multiple_of(i * LAT_TQ, LAT_TQ), LAT_TQ)
            q = q_ref[r, :]
            if normalise:
                q = _head_rmsnorm(q, bqg, lane_lo)
            q_scr[r, :] = (_rope(q, cos_ref[r, :], sin_ref[r, :], lane) * QSCALE).astype(BF16)
            return carry
        lax.fori_loop(0, DEC_SEQ // LAT_TQ, step, 0)

    @pl.when(u < A_HEADS)
    def _mixer_a():
        prepare_queries(False)
        q = q_scr[...]
        v = chunks(va_scr, u, 0)
        a1, l1 = _attend_all_queries(q, chunks(ka_scr, u, 0), v)
        a2, l2 = _attend_all_queries(q, chunks(ka_scr, u, NK), v)
        att_ref[...] = _subln(a1 / l1 - lam * (a2 / l2), subg, lam_init).astype(BF16)

    def pair_unit(k_scr, v_scr, g, masks=None, sinks=(None, None)):
        q = q_scr[...]
        ae, le = _attend_all_queries(q, chunks(k_scr, g, 0), chunks(v_scr, g, 0),
                                     masks=masks, sink=sinks[0], l_lane=64)
        ao, lo_ = _attend_all_queries(q, chunks(k_scr, g, NK), chunks(v_scr, g, NK),
                                      masks=masks, sink=sinks[1], l_lane=0)
        o = jnp.where(lane_lo_all, ae, ao) / jnp.where(lane_lo_all, le, lo_)
        att_ref[...] = o.astype(BF16)

    @pl.when((u >= A_HEADS) & (u < A_HEADS + 4))
    def _mixer_b():
        prepare_queries(True)
        pair_unit(kb_scr, vb_scr, lax.shift_right_logical(u - A_HEADS, 1))

    @pl.when(u >= A_HEADS + 4)
    def _mixer_c():
        j = u - (A_HEADS + 4)
        prepare_queries(False)
        qpos = lax.broadcasted_iota(jnp.int32, (DEC_SEQ, KCH), 0)
        kcol = lax.broadcasted_iota(jnp.int32, (DEC_SEQ, KCH), 1)
        masks = (None,) + tuple(jnp.abs(kcol + (c * KCH - PAST_LEN) - qpos) <= WINDOW
                                for c in range(PAST_LEN // KCH, NK // KCH))
        pair_unit(kc_scr, vc_scr, lax.shift_right_logical(j, 1), masks=masks,
                  sinks=(sink_ref[layer, 2 * j] * LOG2E, sink_ref[layer, 2 * j + 1] * LOG2E))


def _unit_q_col(u):
    return jnp.where(u < A_HEADS, u, jnp.where(u < A_HEADS + 4, QB // LANES - A_HEADS + u,
                                               QC // LANES - A_HEADS - 4 + u))


def _att_lat(qkv, caches, cos, sin, sp, sink, layer, lam_init):
    t = qkv.shape[0]
    once = pl.Buffered(1)
    kv_blk = lambda col: pl.BlockSpec((DEC_SEQ, 512), lambda b, i: (b, col), pipeline_mode=once)
    cache_blk = lambda w: pl.BlockSpec((None, None, PAST_LEN, w), lambda b, i: (b, layer, 0, 0),
                                       pipeline_mode=once)
    return pl.pallas_call(
        functools.partial(_att_lat_kernel, layer=layer, lam_init=lam_init),
        grid=(DEC_BATCH, N_UNITS),
        in_specs=[
            pl.BlockSpec((DEC_SEQ, LANES), lambda b, i: (b, _unit_q_col(i))),
            kv_blk(KA // 512), kv_blk(VA // 512), kv_blk(4), kv_blk(5),
            cache_blk(512), cache_blk(512), cache_blk(LANES), cache_blk(LANES),
            cache_blk(LANES), cache_blk(LANES),
            pl.BlockSpec((DEC_SEQ, LANES), lambda b, i: (0, 0)),
            pl.BlockSpec((DEC_SEQ, LANES), lambda b, i: (0, 0)),
            pl.BlockSpec((None, 8, LANES), lambda b, i: (layer, 0, 0)),
            pl.BlockSpec(memory_space=pltpu.SMEM),
        ],
        out_specs=pl.BlockSpec((DEC_SEQ, LANES), lambda b, i: (b, i)),
        out_shape=jax.ShapeDtypeStruct((t, D_ATT), BF16),
        scratch_shapes=[
            pltpu.VMEM((A_HEADS, 2 * NK, LANES), BF16),
            pltpu.VMEM((A_HEADS, NK, LANES), BF16),
            pltpu.VMEM((2, 2 * NK, LANES), BF16),
            pltpu.VMEM((2, 2 * NK, LANES), BF16),
            pltpu.VMEM((2, 2 * NK, LANES), BF16),
            pltpu.VMEM((2, 2 * NK, LANES), BF16),
            pltpu.VMEM((DEC_SEQ, LANES), BF16),
        ],
        compiler_params=_cparams(("arbitrary", "arbitrary")),
        name="att_lat",
    )(qkv, qkv, qkv, qkv, qkv, *caches, cos, sin, sp, sink)


POST_TM = 512


def _post_kernel(att_ref, gate_ref, x_ref, mod_ref, wa_ref, wb_ref, wc_ref, wo_ref, o_ref,
                 *, tm, rows_per_mod, row0):
    merged = None
    for i, w_ref in enumerate((wa_ref, wb_ref, wc_ref)):
        y = jnp.dot(att_ref[:, i * 512:(i + 1) * 512], w_ref[...].astype(BF16),
                    preferred_element_type=F32)
        term = jax.nn.sigmoid(gate_ref[:, i * D_MODEL:(i + 1) * D_MODEL].astype(F32)) * y
        merged = term if merged is None else merged + term
    mixed = jnp.dot(merged.astype(BF16), wo_ref[...].astype(BF16), preferred_element_type=F32)
    row = row0 + lax.div(pl.program_id(0) * tm, rows_per_mod)
    o_ref[...] = x_ref[...] + _mod_block(mod_ref, row, M_G1) * mixed


def _post(att, gates, x, mod, wa, wb, wc, wo, layer, rows_per_mod, row0):
    t = x.shape[0]
    tm = POST_TM
    full = lambda r: pl.BlockSpec((None, r, D_MODEL), lambda m: (layer, 0, 0))
    return pl.pallas_call(
        functools.partial(_post_kernel, tm=tm, rows_per_mod=rows_per_mod, row0=row0),
        grid=(t // tm,),
        in_specs=[
            pl.BlockSpec((tm, D_ATT), lambda m: (m, 0)),
            pl.BlockSpec((tm, D_GATE), lambda m: (m, 0)),
            pl.BlockSpec((tm, D_MODEL), lambda m: (m, 0)),
            _mod_spec(layer),
            full(512), full(512), full(512), full(D_MODEL),
        ],
        out_specs=pl.BlockSpec((tm, D_MODEL), lambda m: (m, 0)),
        out_shape=jax.ShapeDtypeStruct((t, D_MODEL), F32),
        compiler_params=_cparams(("arbitrary",)),
        name="post",
    )(att, gates, x, mod, wa, wb, wc, wo)


FFN_TM = 1024
FFN_TF = 256


def _ffn_kernel(x_ref, mod_ref, g_ref, wa_ref, wb_ref, wo_ref, o_ref, h_scr,
                *, tm, rows_per_mod, row0):
    f = pl.program_id(1)
    row = row0 + lax.div(pl.program_id(0) * tm, rows_per_mod)

    @pl.when(f == 0)
    def _():
        _norm_mod_rows(x_ref, h_scr, g_ref[...], mod_ref, M_SC2, M_SH2, tm, row, rows_per_mod)

    h = h_scr[...]
    a = jnp.dot(h, wa_ref[...].astype(BF16), preferred_element_type=F32)
    b = jnp.dot(h, wb_ref[...].astype(BF16), preferred_element_type=F32)
    y = (a * jax.nn.sigmoid(a) * b).astype(BF16)
    part = jnp.dot(y, wo_ref[...].astype(BF16), preferred_element_type=F32)

    @pl.when(f == 0)
    def _():
        o_ref[...] = part

    @pl.when(f > 0)
    def _():
        o_ref[...] += part

    @pl.when(f == pl.num_programs(1) - 1)
    def _():
        o_ref[...] = x_ref[...] + _mod_block(mod_ref, row, M_G2) * o_ref[...]


def _ffn(x, mod, g, w_in, w_out, layer, rows_per_mod, row0):
    t = x.shape[0]
    tm = FFN_TM
    assert rows_per_mod % tm == 0
    nf = D_FF // FFN_TF
    return pl.pallas_call(
        functools.partial(_ffn_kernel, tm=tm, rows_per_mod=rows_per_mod, row0=row0),
        grid=(t // tm, nf),
        in_specs=[
            pl.BlockSpec((tm, D_MODEL), lambda m, f: (m, 0)),
            _mod_spec(layer),
            pl.BlockSpec((None, 1, D_MODEL), lambda m, f: (layer, 0, 0)),
            pl.BlockSpec((None, D_MODEL, FFN_TF), lambda m, f: (layer, 0, f)),
            pl.BlockSpec((None, D_MODEL, FFN_TF), lambda m, f: (layer, 0, f + nf)),
            pl.BlockSpec((None, FFN_TF, D_MODEL), lambda m, f: (layer, f, 0)),
        ],
        out_specs=pl.BlockSpec((tm, D_MODEL), lambda m, f: (m, 0)),
        out_shape=jax.ShapeDtypeStruct((t, D_MODEL), F32),
        scratch_shapes=[pltpu.VMEM((tm, D_MODEL), BF16)],
        compiler_params=_cparams(("arbitrary", "arbitrary")),
        name="ffn",
    )(x, mod, g, w_in, w_in, w_out)


FINAL_TM = 512


def _final_kernel(x_ref, g_ref, o_ref):
    x = x_ref[...]
    o_ref[...] = x * lax.rsqrt(jnp.mean(x * x, axis=-1, keepdims=True) + EPS) * g_ref[...]


def _final_norm(x, g):
    t = x.shape[0]
    return pl.pallas_call(
        _final_kernel,
        grid=(t // FINAL_TM,),
        in_specs=[pl.BlockSpec((FINAL_TM, D_MODEL), lambda m: (m, 0)),
                  pl.BlockSpec((1, D_MODEL), lambda m: (0, 0))],
        out_specs=pl.BlockSpec((FINAL_TM, D_MODEL), lambda m: (m, 0)),
        out_shape=jax.ShapeDtypeStruct((t, D_MODEL), F32),
        compiler_params=_cparams(("arbitrary",)),
        name="final_norm",
    )(x, g)


def _rope_tables():
    rows = DEC_SEQ // GRID_W
    row = jnp.repeat(jnp.arange(rows), GRID_W).astype(F32)
    col = jnp.tile(jnp.arange(GRID_W), rows).astype(F32)
    n = HEAD_DIM // 4
    inv = ROPE_THETA ** (-jnp.arange(n, dtype=F32) / n)
    ang = jnp.concatenate([row[:, None] * inv, col[:, None] * inv], axis=-1)
    cos, sin = jnp.cos(ang), jnp.sin(ang)
    cos_t = jnp.tile(cos, (1, 4))
    sin_t = jnp.tile(jnp.concatenate([-sin, sin], axis=-1), (1, 2))
    return cos_t, sin_t


def _pack_small(a_lam_q1, a_lam_k1, a_lam_q2, a_lam_k2, a_subln_g, b_qnorm_g, b_knorm_g):
    pad = lambda v: jnp.pad(v, ((0, 0), (0, LANES - HEAD_DIM)))
    rows = [pad(a_lam_q1), pad(a_lam_k1), pad(a_lam_q2), pad(a_lam_k2), a_subln_g,
            jnp.tile(b_qnorm_g, (1, 2)), jnp.tile(b_knorm_g, (1, 2)),
            jnp.zeros((DEPTH, LANES), F32)]
    return jnp.stack(rows, axis=1)


def kernel(x_prompt, x_sample, cache_a_k, cache_a_v, cache_b_k, cache_b_v, cache_c_k, cache_c_v, c, c_ctx, w_mod, b_mod, norm1_g, norm2_g, w_in, a_lam_q1, a_lam_k1, a_lam_q2, a_lam_k2, a_subln_g, b_qnorm_g, b_knorm_g, c_sink, w_br_a, w_br_b, w_br_c, w_out, w_ffn_in, w_ffn_out, final_g):
    t_ctx = BATCH * SEQ
    t_lat = DEC_BATCH * DEC_SEQ
    xp = x_prompt.reshape(t_ctx, D_MODEL)
    xs = x_sample.reshape(t_lat, D_MODEL)
    cv8 = jnp.concatenate([c_ctx[None, :], c, jnp.zeros((8 - 1 - DEC_BATCH, D_MODEL), F32)], axis=0)
    mod = _modulation(cv8, w_mod, b_mod)
    cos_t, sin_t = _rope_tables()
    sp = _pack_small(a_lam_q1, a_lam_k1, a_lam_q2, a_lam_k2, a_subln_g, b_qnorm_g, b_knorm_g)
    n1 = norm1_g.reshape(DEPTH, 1, D_MODEL)
    n2 = norm2_g.reshape(DEPTH, 1, D_MODEL)
    caches = (cache_a_k.reshape(DEC_BATCH, DEPTH, PAST_LEN, 512),
              cache_a_v.reshape(DEC_BATCH, DEPTH, PAST_LEN, 512),
              cache_b_k.reshape(DEC_BATCH, DEPTH, PAST_LEN, LANES),
              cache_b_v.reshape(DEC_BATCH, DEPTH, PAST_LEN, LANES),
              cache_c_k.reshape(DEC_BATCH, DEPTH, PAST_LEN, LANES),
              cache_c_v.reshape(DEC_BATCH, DEPTH, PAST_LEN, LANES))
    st = [[] for _ in range(6)]
    for l in range(DEPTH):
        lam_init = 0.8 - 0.6 * math.exp(-0.3 * l)

        qkv_c, gates_c = _proj(xp, mod, n1, w_in, l, t_ctx, 0)
        att_c, ka, va, kbn = _att_ctx(qkv_c, sp, c_sink, l, lam_init)
        xp = _post(att_c, gates_c, xp, mod, w_br_a, w_br_b, w_br_c, w_out, l, t_ctx, 0)
        xp = _ffn(xp, mod, n2, w_ffn_in, w_ffn_out, l, t_ctx, 0)
        st[0].append(ka)
        st[1].append(va)
        st[2].append(kbn.reshape(BATCH, SEQ, 2, HEAD_DIM))
        st[3].append(qkv_c[:, VB:VB + LANES].reshape(BATCH, SEQ, 2, HEAD_DIM))
        st[4].append(qkv_c[:, KC:KC + LANES].reshape(BATCH, SEQ, 2, HEAD_DIM))
        st[5].append(qkv_c[:, VC:VC + LANES].reshape(BATCH, SEQ, 2, HEAD_DIM))

        qkv_s, gates_s = _proj(xs, mod, n1, w_in, l, DEC_SEQ, 1)
        att_s = _att_lat(qkv_s, caches, cos_t, sin_t, sp, c_sink, l, lam_init)
        xs = _post(att_s, gates_s, xs, mod, w_br_a, w_br_b, w_br_c, w_out, l, DEC_SEQ, 1)
        xs = _ffn(xs, mod, n2, w_ffn_in, w_ffn_out, l, DEC_SEQ, 1)

    fg = final_g.reshape(1, D_MODEL)
    y_prompt = _final_norm(xp, fg).reshape(BATCH, SEQ, D_MODEL)
    y_sample = _final_norm(xs, fg).reshape(DEC_BATCH, DEC_SEQ, D_MODEL)
    wide = tuple(jnp.stack(s, axis=1).reshape(BATCH, DEPTH, SEQ, A_HEADS, 2 * HEAD_DIM) for s in st[:2])
    return (y_prompt, y_sample) + wide + tuple(jnp.stack(s, axis=1) for s in st[2:])
```

```python
import functools
import math

import jax
import jax.numpy as jnp
from jax import lax
from jax.experimental import pallas as pl
from jax.experimental.pallas import tpu as pltpu

D_MODEL = 1024
BATCH = 16
SEQ = 256
DEPTH = 4
DEC_BATCH = 4
DEC_SEQ = 1024
PAST_LEN = 512
GRID_W = 64
HEAD_DIM = 64
ROPE_THETA = 10000.0
EPS = 1e-6
NEG_INF = -1e30
A_HEADS = 4
WINDOW = 128
D_FF = -(-8 * D_MODEL // (3 * 256)) * 256
N_MOD = 6
D_QKV = 3072
D_GATE = 3072
D_ATT = 1536
LANES = 128
LOG2E = math.log2(math.e)
QSCALE = HEAD_DIM ** -0.5 * LOG2E

QA, KA, VA, QB, KB, VB, QC, KC, VC = 0, 512, 1024, 1536, 2048, 2176, 2304, 2816, 2944
R_LQ1, R_LK1, R_LQ2, R_LK2, R_SUBG, R_BQG, R_BKG = range(7)
M_SH1, M_SC1, M_G1, M_SH2, M_SC2, M_G2 = range(6)

F32 = jnp.float32
BF16 = jnp.bfloat16
VMEM_LIMIT = 56 * 1024 * 1024


def _cparams(sem):
    return pltpu.CompilerParams(dimension_semantics=sem, vmem_limit_bytes=VMEM_LIMIT)


MOD_TN = 1536


def _mod_kernel(cv_ref, w_ref, b_ref, o_ref):
    cv = cv_ref[...]
    s = (cv * jax.nn.sigmoid(cv)).astype(BF16)
    o_ref[...] = jnp.dot(s, w_ref[...].astype(BF16), preferred_element_type=F32) + b_ref[...]


def _modulation(cv8, w_mod, b_mod):
    n = N_MOD * D_MODEL
    return pl.pallas_call(
        _mod_kernel,
        grid=(DEPTH, n // MOD_TN),
        in_specs=[
            pl.BlockSpec((8, D_MODEL), lambda l, j: (0, 0)),
            pl.BlockSpec((None, D_MODEL, MOD_TN), lambda l, j: (l, 0, j)),
            pl.BlockSpec((None, 1, MOD_TN), lambda l, j: (l, 0, j)),
        ],
        out_specs=pl.BlockSpec((None, 8, MOD_TN), lambda l, j: (l, 0, j)),
        out_shape=jax.ShapeDtypeStruct((DEPTH, 8, n), F32),
        compiler_params=_cparams(("arbitrary", "arbitrary")),
        name="modulation",
    )(cv8, w_mod, b_mod.reshape(DEPTH, 1, n))


NORM_CHUNK = 256


def _mod_block(mod_ref, row, blk):
    return mod_ref[pl.ds(row, 1), blk * D_MODEL:(blk + 1) * D_MODEL]


def _norm_mod_rows(x_ref, h_ref, g, mod_ref, sc_blk, sh_blk, rows, mod_row0, rows_per_mod):
    def body(i, carry):
        r = pl.ds(pl.multiple_of(i * NORM_CHUNK, NORM_CHUNK), NORM_CHUNK)
        row = mod_row0 + lax.div(i * NORM_CHUNK, rows_per_mod)
        x = x_ref[r, :]
        y = x * lax.rsqrt(jnp.mean(x * x, axis=-1, keepdims=True) + EPS) * g
        h_ref[r, :] = (y * (1.0 + _mod_block(mod_ref, row, sc_blk))
                       + _mod_block(mod_ref, row, sh_blk)).astype(BF16)
        return carry
    lax.fori_loop(0, rows // NORM_CHUNK, body, 0)


def _head_rmsnorm(x, g, lane_lo):
    x2 = x * x
    zero = jnp.zeros_like(x2)
    lo = jnp.sum(jnp.where(lane_lo, x2, zero), axis=-1, keepdims=True)
    hi = jnp.sum(jnp.where(lane_lo, zero, x2), axis=-1, keepdims=True)
    ms = jnp.where(lane_lo, lo, hi) * (1.0 / HEAD_DIM)
    return x * lax.rsqrt(ms + EPS) * g


def _rope(x, cos, sin, lane):
    partner = jnp.where((lane & 32) == 0, pltpu.roll(x, LANES - 32, 1), pltpu.roll(x, 32, 1))
    return x * cos + partner * sin


def _lo_hi(x, lane, src_hi, ones_lane=False):
    lane_lo = lane < 64
    other = pltpu.roll(x, 64, 1)
    zero = jnp.zeros_like(x)
    lo = jnp.where(lane_lo, other if src_hi else x, zero)
    hi = jnp.where(lane_lo, zero, x if src_hi else other)
    if ones_lane:
        lo = jnp.where(lane == 64, 1.0, lo)
        hi = jnp.where(lane == 0, 1.0, hi)
    return lo, hi


def _dot_nt(a, b):
    return lax.dot_general(a, b, (((1,), (1,)), ((), ())), preferred_element_type=F32)


def _probs(s, extra=None):
    m = jnp.max(s, axis=-1, keepdims=True)
    if extra is not None:
        m = jnp.maximum(m, extra)
    return jnp.exp2(s - m), m


def _lam(sp_ref, lam_init):
    dot1 = jnp.sum(sp_ref[R_LQ1:R_LQ1 + 1, :] * sp_ref[R_LK1:R_LK1 + 1, :], axis=-1, keepdims=True)
    dot2 = jnp.sum(sp_ref[R_LQ2:R_LQ2 + 1, :] * sp_ref[R_LK2:R_LK2 + 1, :], axis=-1, keepdims=True)
    return jnp.exp(dot1) - jnp.exp(dot2) + lam_init


def _subln(o, g, lam_init):
    return o * lax.rsqrt(jnp.mean(o * o, axis=-1, keepdims=True) + EPS) * g * (1.0 - lam_init)


def _mod_spec(layer):
    return pl.BlockSpec((None, 8, N_MOD * D_MODEL), lambda *_: (layer, 0, 0))


PROJ_TM = 2048
PROJ_TN = 512
N_QKV_TILES = D_QKV // PROJ_TN


def _proj_kernel(x_ref, mod_ref, g_ref, w_ref, qkv_ref, gate_ref, h_scr, w_scr, *, tm, rows_per_mod, row0):
    n = pl.program_id(0)
    m = pl.program_id(1)

    @pl.when(n == 0)
    def _():
        _norm_mod_rows(x_ref, h_scr.at[m], g_ref[...], mod_ref, M_SC1, M_SH1, tm,
                       row0 + lax.div(m * tm, rows_per_mod), rows_per_mod)

    @pl.when(m == 0)
    def _():
        w_scr[...] = w_ref[...].astype(BF16)

    @pl.when(n < N_QKV_TILES)
    def _():
        qkv_ref[...] = jnp.dot(h_scr[m], w_scr[...], preferred_element_type=F32)

    @pl.when(n >= N_QKV_TILES)
    def _():
        gate_ref[...] = jnp.dot(h_scr[m], w_scr[...], preferred_element_type=F32).astype(BF16)


def _proj(x, mod, g, w_in, layer, rows_per_mod, row0):
    t = x.shape[0]
    tm = PROJ_TM
    nm = t // tm
    return pl.pallas_call(
        functools.partial(_proj_kernel, tm=tm, rows_per_mod=rows_per_mod, row0=row0),
        grid=((D_QKV + D_GATE) // PROJ_TN, nm),
        in_specs=[
            pl.BlockSpec((tm, D_MODEL), lambda n, m: (jnp.where(n == 0, m, nm - 1), 0)),
            _mod_spec(layer),
            pl.BlockSpec((None, 1, D_MODEL), lambda n, m: (layer, 0, 0)),
            pl.BlockSpec((None, D_MODEL, PROJ_TN), lambda n, m: (layer, 0, n)),
        ],
        out_specs=[
            pl.BlockSpec((tm, PROJ_TN), lambda n, m: (jnp.where(n < N_QKV_TILES, m, nm - 1),
                                                      jnp.minimum(n, N_QKV_TILES - 1))),
            pl.BlockSpec((tm, PROJ_TN), lambda n, m: (jnp.where(n < N_QKV_TILES, 0, m),
                                                      jnp.maximum(n - N_QKV_TILES, 0))),
        ],
        out_shape=[jax.ShapeDtypeStruct((t, D_QKV), F32),
                   jax.ShapeDtypeStruct((t, D_GATE), BF16)],
        scratch_shapes=[pltpu.VMEM((nm, tm, D_MODEL), BF16), pltpu.VMEM((D_MODEL, PROJ_TN), BF16)],
        compiler_params=_cparams(("arbitrary", "arbitrary")),
        name="proj",
    )(x, mod, g, w_in)


def _att_ctx_kernel(qkv_ref, sp_ref, sink_ref, att_ref, ka_ref, va_ref, kbn_ref, *, layer, lam_init):
    lane = lax.broadcasted_iota(jnp.int32, (SEQ, LANES), 1)
    lane_lo = lane < 64
    lam = _lam(sp_ref, lam_init)
    subg = sp_ref[R_SUBG:R_SUBG + 1, :]
    bqg = sp_ref[R_BQG:R_BQG + 1, :]
    bkg = sp_ref[R_BKG:R_BKG + 1, :]

    for h in range(A_HEADS):
        q = (qkv_ref[:, QA + h * LANES:QA + (h + 1) * LANES] * QSCALE).astype(BF16)
        k = qkv_ref[:, KA + h * LANES:KA + (h + 1) * LANES]
        v = qkv_ref[:, VA + h * LANES:VA + (h + 1) * LANES]
        ka_ref[pl.ds(h, SEQ, stride=A_HEADS), :] = k
        va_ref[pl.ds(h, SEQ, stride=A_HEADS), :] = v
        v = v.astype(BF16)
        zero = jnp.zeros_like(k)
        kst = jnp.concatenate([jnp.where(lane_lo, k, zero).astype(BF16),
                               jnp.where(lane_lo, zero, k).astype(BF16)], axis=0)
        s = _dot_nt(q, kst)
        p1, _ = _probs(s[:, :SEQ])
        p2, _ = _probs(s[:, SEQ:])
        o1 = jnp.dot(p1.astype(BF16), v, preferred_element_type=F32) / jnp.sum(p1, axis=-1, keepdims=True)
        o2 = jnp.dot(p2.astype(BF16), v, preferred_element_type=F32) / jnp.sum(p2, axis=-1, keepdims=True)
        o = _subln(o1 - lam * o2, subg, lam_init)
        att_ref[:, h * LANES:(h + 1) * LANES] = o.astype(BF16)

    kb = _head_rmsnorm(qkv_ref[:, KB:KB + LANES], bkg, lane_lo)
    kbn_ref[...] = kb
    for mixer in range(2):
        if mixer == 0:
            q0, k_t, v_t, o0 = QB, kb, qkv_ref[:, VB:VB + LANES], 512
        else:
            q0, k_t, v_t, o0 = QC, qkv_ref[:, KC:KC + LANES], qkv_ref[:, VC:VC + LANES], 1024
        for g in range(2):
            k_lo, k_hi = _lo_hi(k_t, lane, g == 1)
            v_lo, v_hi = _lo_hi(v_t, lane, g == 1, ones_lane=True)
            kst = jnp.concatenate([k_lo.astype(BF16), k_hi.astype(BF16)], axis=0)
            v_lo = v_lo.astype(BF16)
            v_hi = v_hi.astype(BF16)
            for jj in range(2):
                j = 2 * g + jj
                q = qkv_ref[:, q0 + j * LANES:q0 + (j + 1) * LANES]
                if mixer == 0:
                    q = _head_rmsnorm(q, bqg, lane_lo)
                s = _dot_nt((q * QSCALE).astype(BF16), kst)
                if mixer == 0:
                    pe, _ = _probs(s[:, :SEQ])
                    po, _ = _probs(s[:, SEQ:])
                else:
                    sink_e = sink_ref[layer, 2 * j] * LOG2E
                    sink_o = sink_ref[layer, 2 * j + 1] * LOG2E
                    pe, me = _probs(s[:, :SEQ], sink_e)
                    po, mo = _probs(s[:, SEQ:], sink_o)
                oe = jnp.dot(pe.astype(BF16), v_lo, preferred_element_type=F32)
                oo = jnp.dot(po.astype(BF16), v_hi, preferred_element_type=F32)
                le = oe[:, 64:65]
                lo_ = oo[:, 0:1]
                if mixer == 1:
                    le = le + jnp.exp2(sink_e - me)
                    lo_ = lo_ + jnp.exp2(sink_o - mo)
                o = jnp.where(lane_lo, oe, oo) / jnp.where(lane_lo, le, lo_)
                att_ref[:, o0 + j * LANES:o0 + (j + 1) * LANES] = o.astype(BF16)


def _att_ctx(qkv, sp, sink, layer, lam_init):
    t = qkv.shape[0]
    nb = t // SEQ
    return pl.pallas_call(
        functools.partial(_att_ctx_kernel, layer=layer, lam_init=lam_init),
        grid=(nb,),
        in_specs=[
            pl.BlockSpec((SEQ, D_QKV), lambda b: (b, 0)),
            pl.BlockSpec((None, 8, LANES), lambda b: (layer, 0, 0)),
            pl.BlockSpec(memory_space=pltpu.SMEM),
        ],
        out_specs=[
            pl.BlockSpec((SEQ, D_ATT), lambda b: (b, 0)),
            pl.BlockSpec((None, SEQ * A_HEADS, LANES), lambda b: (b, 0, 0)),
            pl.BlockSpec((None, SEQ * A_HEADS, LANES), lambda b: (b, 0, 0)),
            pl.BlockSpec((SEQ, LANES), lambda b: (b, 0)),
        ],
        out_shape=[jax.ShapeDtypeStruct((t, D_ATT), BF16),
                   jax.ShapeDtypeStruct((nb, SEQ * A_HEADS, LANES), F32),
                   jax.ShapeDtypeStruct((nb, SEQ * A_HEADS, LANES), F32),
                   jax.ShapeDtypeStruct((t, LANES), F32)],
        compiler_params=_cparams(("arbitrary",)),
        name="att_ctx",
    )(qkv, sp, sink)


LAT_TQ = 256
NK = PAST_LEN + DEC_SEQ
KCH = 512
PREP_ROWS = 256
N_UNITS = 12


def _attend_all_queries(q, k_chunks, v_chunks, masks=None, sink=None, l_lane=None):
    m = l = acc = None
    for c, (k_c, v_c) in enumerate(zip(k_chunks, v_chunks)):
        s = _dot_nt(q, k_c)
        if masks is not None and masks[c] is not None:
            s = jnp.where(masks[c], s, NEG_INF)
        mc = jnp.max(s, axis=-1, keepdims=True)
        if m is None:
            m_new = mc if sink is None else jnp.maximum(mc, sink)
            p = jnp.exp2(s - m_new)
            acc = jnp.dot(p.astype(BF16), v_c, preferred_element_type=F32)
            if l_lane is None:
                l = jnp.sum(p, axis=-1, keepdims=True)
        else:
            m_new = jnp.maximum(m, mc)
            alpha = jnp.exp2(m - m_new)
            p = jnp.exp2(s - m_new)
            acc = alpha * acc + jnp.dot(p.astype(BF16), v_c, preferred_element_type=F32)
            if l_lane is None:
                l = alpha * l + jnp.sum(p, axis=-1, keepdims=True)
        m = m_new
    if l_lane is not None:
        l = acc[:, l_lane:l_lane + 1]
    if sink is not None:
        l = l + jnp.exp2(sink - m)
    return acc, l


def _att_lat_kernel(q_ref, kva_ref, kvv_ref, kv4_ref, kv5_ref,
                    cak_ref, cav_ref, cbk_ref, cbv_ref, cck_ref, ccv_ref,
                    cos_ref, sin_ref, sp_ref, sink_ref,
                    att_ref,
                    ka_scr, va_scr, kb_scr, vb_scr, kc_scr, vc_scr, q_scr,
                    *, layer, lam_init):
    u = pl.program_id(1)
    subg = sp_ref[R_SUBG:R_SUBG + 1, :]
    bqg = sp_ref[R_BQG:R_BQG + 1, :]
    bkg = sp_ref[R_BKG:R_BKG + 1, :]
    lane = lax.broadcasted_iota(jnp.int32, (LAT_TQ, LANES), 1)
    lane_lo = lane < 64

    @pl.when(u == 0)
    def _prepare_keys():
        def put(dst, a_k, a_v, pairs):
            lo_rows = pl.ds(pl.multiple_of(dst, PREP_ROWS), PREP_ROWS)
            hi_rows = pl.ds(pl.multiple_of(NK + dst, PREP_ROWS), PREP_ROWS)
            for h in range(A_HEADS):
                k = a_k[h]
                zero = jnp.zeros_like(k)
                ka_scr[h, lo_rows, :] = jnp.where(lane_lo, k, zero).astype(BF16)
                ka_scr[h, hi_rows, :] = jnp.where(lane_lo, zero, k).astype(BF16)
                va_scr[h, lo_rows, :] = a_v[h].astype(BF16)
            for x, scr, is_value in pairs:
                for g in range(2):
                    lo, hi = _lo_hi(x, lane, g == 1, ones_lane=is_value)
                    scr[g, lo_rows, :] = lo.astype(BF16)
                    scr[g, hi_rows, :] = hi.astype(BF16)

        def cached(i, carry):
            r = pl.ds(pl.multiple_of(i * PREP_ROWS, PREP_ROWS), PREP_ROWS)
            head_rows = lambda h: pl.ds(i * (PREP_ROWS * A_HEADS) + h, PREP_ROWS, stride=A_HEADS)
            put(i * PREP_ROWS,
                [cak_ref[head_rows(h), :] for h in range(A_HEADS)],
                [cav_ref[head_rows(h), :] for h in range(A_HEADS)],
                ((cbk_ref[r, :], kb_scr, False), (cbv_ref[r, :], vb_scr, True),
                 (cck_ref[r, :], kc_scr, False), (ccv_ref[r, :], vc_scr, True)))
            return carry
        lax.fori_loop(0, PAST_LEN // PREP_ROWS, cached, 0)

        def latent(i, carry):
            r = pl.ds(pl.multiple_of(i * PREP_ROWS, PREP_ROWS), PREP_ROWS)
            cos = cos_ref[r, :]
            sin = sin_ref[r, :]
            kb = _rope(_head_rmsnorm(kv4_ref[r, 0:LANES], bkg, lane_lo), cos, sin, lane)
            kc = _rope(kv5_ref[r, 256:256 + LANES], cos, sin, lane)
            put(PAST_LEN + i * PREP_ROWS,
                [_rope(kva_ref[r, h * LANES:(h + 1) * LANES], cos, sin, lane) for h in range(A_HEADS)],
                [kvv_ref[r, h * LANES:(h + 1) * LANES] for h in range(A_HEADS)],
                ((kb, kb_scr, False), (kv4_ref[r, LANES:2 * LANES], vb_scr, True),
                 (kc, kc_scr, False), (kv5_ref[r, 384:384 + LANES], vc_scr, True)))
            return carry
        lax.fori_loop(0, DEC_SEQ // PREP_ROWS, latent, 0)

    lam = _lam(sp_ref, lam_init)
    lane_lo_all = lax.broadcasted_iota(jnp.int32, (DEC_SEQ, LANES), 1) < 64
    chunks = lambda scr, i, base: [scr[i, base + c * KCH:base + (c + 1) * KCH, :] for c in range(NK // KCH)]

    def prepare_queries(normalise):
        def step(i, carry):
            r = pl.ds(pl.multiple_of(i * LAT_TQ, LAT_TQ), LAT_TQ)
            q = q_ref[r, :]
            if normalise:
                q = _head_rmsnorm(q, bqg, lane_lo)
            q_scr[r, :] = (_rope(q, cos_ref[r, :], sin_ref[r, :], lane) * QSCALE).astype(BF16)
            return carry
        lax.fori_loop(0, DEC_SEQ // LAT_TQ, step, 0)

    @pl.when(u < A_HEADS)
    def _mixer_a():
        prepare_queries(False)
        q = q_scr[...]
        v = chunks(va_scr, u, 0)
        a1, l1 = _attend_all_queries(q, chunks(ka_scr, u, 0), v)
        a2, l2 = _attend_all_queries(q, chunks(ka_scr, u, NK), v)
        att_ref[...] = _subln(a1 / l1 - lam * (a2 / l2), subg, lam_init).astype(BF16)

    def pair_unit(k_scr, v_scr, g, masks=None, sinks=(None, None)):
        q = q_scr[...]
        ae, le = _attend_all_queries(q, chunks(k_scr, g, 0), chunks(v_scr, g, 0),
                                     masks=masks, sink=sinks[0], l_lane=64)
        ao, lo_ = _attend_all_queries(q, chunks(k_scr, g, NK), chunks(v_scr, g, NK),
                                      masks=masks, sink=sinks[1], l_lane=0)
        o = jnp.where(lane_lo_all, ae, ao) / jnp.where(lane_lo_all, le, lo_)
        att_ref[...] = o.astype(BF16)

    @pl.when((u >= A_HEADS) & (u < A_HEADS + 4))
    def _mixer_b():
        prepare_queries(True)
        pair_unit(kb_scr, vb_scr, lax.shift_right_logical(u - A_HEADS, 1))

    @pl.when(u >= A_HEADS + 4)
    def _mixer_c():
        j = u - (A_HEADS + 4)
        prepare_queries(False)
        qpos = lax.broadcasted_iota(jnp.int32, (DEC_SEQ, KCH), 0)
        kcol = lax.broadcasted_iota(jnp.int32, (DEC_SEQ, KCH), 1)
        masks = (None,) + tuple(jnp.abs(kcol + (c * KCH - PAST_LEN) - qpos) <= WINDOW
                                for c in range(PAST_LEN // KCH, NK // KCH))
        pair_unit(kc_scr, vc_scr, lax.shift_right_logical(j, 1), masks=masks,
                  sinks=(sink_ref[layer, 2 * j] * LOG2E, sink_ref[layer, 2 * j + 1] * LOG2E))


def _unit_q_col(u):
    return jnp.where(u < A_HEADS, u, jnp.where(u < A_HEADS + 4, QB // LANES - A_HEADS + u,
                                               QC // LANES - A_HEADS - 4 + u))


def _att_lat(qkv, caches, cos, sin, sp, sink, layer, lam_init):
    t = qkv.shape[0]
    once = pl.Buffered(1)
    kv_blk = lambda col: pl.BlockSpec((DEC_SEQ, 512), lambda b, i: (b, col), pipeline_mode=once)
    cache_blk = lambda r: pl.BlockSpec((None, None, r, LANES), lambda b, i: (b, layer, 0, 0),
                                       pipeline_mode=once)
    return pl.pallas_call(
        functools.partial(_att_lat_kernel, layer=layer, lam_init=lam_init),
        grid=(DEC_BATCH, N_UNITS),
        in_specs=[
            pl.BlockSpec((DEC_SEQ, LANES), lambda b, i: (b, _unit_q_col(i))),
            kv_blk(KA // 512), kv_blk(VA // 512), kv_blk(4), kv_blk(5),
            cache_blk(PAST_LEN * A_HEADS), cache_blk(PAST_LEN * A_HEADS),
            cache_blk(PAST_LEN), cache_blk(PAST_LEN), cache_blk(PAST_LEN), cache_blk(PAST_LEN),
            pl.BlockSpec((DEC_SEQ, LANES), lambda b, i: (0, 0)),
            pl.BlockSpec((DEC_SEQ, LANES), lambda b, i: (0, 0)),
            pl.BlockSpec((None, 8, LANES), lambda b, i: (layer, 0, 0)),
            pl.BlockSpec(memory_space=pltpu.SMEM),
        ],
        out_specs=pl.BlockSpec((DEC_SEQ, LANES), lambda b, i: (b, i)),
        out_shape=jax.ShapeDtypeStruct((t, D_ATT), BF16),
        scratch_shapes=[
            pltpu.VMEM((A_HEADS, 2 * NK, LANES), BF16),
            pltpu.VMEM((A_HEADS, NK, LANES), BF16),
            pltpu.VMEM((2, 2 * NK, LANES), BF16),
            pltpu.VMEM((2, 2 * NK, LANES), BF16),
            pltpu.VMEM((2, 2 * NK, LANES), BF16),
            pltpu.VMEM((2, 2 * NK, LANES), BF16),
            pltpu.VMEM((DEC_SEQ, LANES), BF16),
        ],
        compiler_params=_cparams(("arbitrary", "arbitrary")),
        name="att_lat",
    )(qkv, qkv, qkv, qkv, qkv, *caches, cos, sin, sp, sink)


POST_TM = 512


def _post_kernel(att_ref, gate_ref, x_ref, mod_ref, wa_ref, wb_ref, wc_ref, wo_ref, o_ref,
                 wbr_scr, wo_scr, *, tm, rows_per_mod, row0):
    @pl.when(pl.program_id(0) == 0)
    def _():
        for i, w_ref in enumerate((wa_ref, wb_ref, wc_ref)):
            wbr_scr[i] = w_ref[...].astype(BF16)
        wo_scr[...] = wo_ref[...].astype(BF16)

    merged = None
    for i in range(3):
        y = jnp.dot(att_ref[:, i * 512:(i + 1) * 512], wbr_scr[i], preferred_element_type=F32)
        term = jax.nn.sigmoid(gate_ref[:, i * D_MODEL:(i + 1) * D_MODEL].astype(F32)) * y
        merged = term if merged is None else merged + term
    mixed = jnp.dot(merged.astype(BF16), wo_scr[...], preferred_element_type=F32)
    row = row0 + lax.div(pl.program_id(0) * tm, rows_per_mod)
    o_ref[...] = x_ref[...] + _mod_block(mod_ref, row, M_G1) * mixed


def _post(att, gates, x, mod, wa, wb, wc, wo, layer, rows_per_mod, row0):
    t = x.shape[0]
    tm = POST_TM
    full = lambda r: pl.BlockSpec((None, r, D_MODEL), lambda m: (layer, 0, 0),
                                  pipeline_mode=pl.Buffered(1))
    return pl.pallas_call(
        functools.partial(_post_kernel, tm=tm, rows_per_mod=rows_per_mod, row0=row0),
        grid=(t // tm,),
        in_specs=[
            pl.BlockSpec((tm, D_ATT), lambda m: (m, 0)),
            pl.BlockSpec((tm, D_GATE), lambda m: (m, 0)),
            pl.BlockSpec((tm, D_MODEL), lambda m: (m, 0)),
            _mod_spec(layer),
            full(512), full(512), full(512), full(D_MODEL),
        ],
        out_specs=pl.BlockSpec((tm, D_MODEL), lambda m: (m, 0)),
        out_shape=jax.ShapeDtypeStruct((t, D_MODEL), F32),
        scratch_shapes=[pltpu.VMEM((3, 512, D_MODEL), BF16), pltpu.VMEM((D_MODEL, D_MODEL), BF16)],
        compiler_params=_cparams(("arbitrary",)),
        name="post",
    )(att, gates, x, mod, wa, wb, wc, wo)


FFN_TM = 1024
FFN_TF = 256


FFN_GROUP = 2


def _ffn_kernel(x_ref, mod_ref, g_ref, wa_ref, wb_ref, wo_ref, o_ref,
                h_scr, acc_scr, wa_scr, wb_scr, wo_scr, *, tm, rows_per_mod, row0):
    grp = pl.program_id(0)
    f = pl.program_id(1)
    m = pl.program_id(2)
    row = row0 + lax.div((grp * FFN_GROUP + m) * tm, rows_per_mod)

    @pl.when(f == 0)
    def _():
        _norm_mod_rows(x_ref, h_scr.at[m], g_ref[...], mod_ref, M_SC2, M_SH2, tm, row, rows_per_mod)
        acc_scr[m] = jnp.zeros((tm, D_MODEL), F32)

    @pl.when(m == 0)
    def _():
        wa_scr[...] = wa_ref[...].astype(BF16)
        wb_scr[...] = wb_ref[...].astype(BF16)
        wo_scr[...] = wo_ref[...].astype(BF16)

    h = h_scr[m]
    a = jnp.dot(h, wa_scr[...], preferred_element_type=F32)
    b = jnp.dot(h, wb_scr[...], preferred_element_type=F32)
    y = (a * jax.nn.sigmoid(a) * b).astype(BF16)
    acc_scr[m] += jnp.dot(y, wo_scr[...], preferred_element_type=F32)

    @pl.when(f == pl.num_programs(1) - 1)
    def _():
        o_ref[...] = x_ref[...] + _mod_block(mod_ref, row, M_G2) * acc_scr[m]


def _ffn(x, mod, g, w_in, w_out, layer, rows_per_mod, row0):
    t = x.shape[0]
    tm = FFN_TM
    assert rows_per_mod % tm == 0
    nf = D_FF // FFN_TF
    last = FFN_GROUP - 1
    x_idx = lambda grp, f, m: (grp * FFN_GROUP + jnp.where((f == 0) | (f == nf - 1), m, last), 0)
    o_idx = lambda grp, f, m: (grp * FFN_GROUP + jnp.where(f == nf - 1, m, 0), 0)
    return pl.pallas_call(
        functools.partial(_ffn_kernel, tm=tm, rows_per_mod=rows_per_mod, row0=row0),
        grid=(t // (tm * FFN_GROUP), nf, FFN_GROUP),
        in_specs=[
            pl.BlockSpec((tm, D_MODEL), x_idx),
            _mod_spec(layer),
            pl.BlockSpec((None, 1, D_MODEL), lambda grp, f, m: (layer, 0, 0)),
            pl.BlockSpec((None, D_MODEL, FFN_TF), lambda grp, f, m: (layer, 0, f)),
            pl.BlockSpec((None, D_MODEL, FFN_TF), lambda grp, f, m: (layer, 0, f + nf)),
            pl.BlockSpec((None, FFN_TF, D_MODEL), lambda grp, f, m: (layer, f, 0)),
        ],
        out_specs=pl.BlockSpec((tm, D_MODEL), o_idx),
        out_shape=jax.ShapeDtypeStruct((t, D_MODEL), F32),
        scratch_shapes=[pltpu.VMEM((FFN_GROUP, tm, D_MODEL), BF16),
                        pltpu.VMEM((FFN_GROUP, tm, D_MODEL), F32),
                        pltpu.VMEM((D_MODEL, FFN_TF), BF16),
                        pltpu.VMEM((D_MODEL, FFN_TF), BF16),
                        pltpu.VMEM((FFN_TF, D_MODEL), BF16)],
        compiler_params=_cparams(("arbitrary", "arbitrary", "arbitrary")),
        name="ffn",
    )(x, mod, g, w_in, w_in, w_out)


FINAL_TM = 512


def _final_kernel(x_ref, g_ref, o_ref):
    x = x_ref[...]
    o_ref[...] = x * lax.rsqrt(jnp.mean(x * x, axis=-1, keepdims=True) + EPS) * g_ref[...]


def _final_norm(x, g):
    t = x.shape[0]
    return pl.pallas_call(
        _final_kernel,
        grid=(t // FINAL_TM,),
        in_specs=[pl.BlockSpec((FINAL_TM, D_MODEL), lambda m: (m, 0)),
                  pl.BlockSpec((1, D_MODEL), lambda m: (0, 0))],
        out_specs=pl.BlockSpec((FINAL_TM, D_MODEL), lambda m: (m, 0)),
        out_shape=jax.ShapeDtypeStruct((t, D_MODEL), F32),
        compiler_params=_cparams(("arbitrary",)),
        name="final_norm",
    )(x, g)


def _rope_tables():
    rows = DEC_SEQ // GRID_W
    row = jnp.repeat(jnp.arange(rows), GRID_W).astype(F32)
    col = jnp.tile(jnp.arange(GRID_W), rows).astype(F32)
    n = HEAD_DIM // 4
    inv = ROPE_THETA ** (-jnp.arange(n, dtype=F32) / n)
    ang = jnp.concatenate([row[:, None] * inv, col[:, None] * inv], axis=-1)
    cos, sin = jnp.cos(ang), jnp.sin(ang)
    cos_t = jnp.tile(cos, (1, 4))
    sin_t = jnp.tile(jnp.concatenate([-sin, sin], axis=-1), (1, 2))
    return cos_t, sin_t


def _pack_small(a_lam_q1, a_lam_k1, a_lam_q2, a_lam_k2, a_subln_g, b_qnorm_g, b_knorm_g):
    pad = lambda v: jnp.pad(v, ((0, 0), (0, LANES - HEAD_DIM)))
    rows = [pad(a_lam_q1), pad(a_lam_k1), pad(a_lam_q2), pad(a_lam_k2), a_subln_g,
            jnp.tile(b_qnorm_g, (1, 2)), jnp.tile(b_knorm_g, (1, 2)),
            jnp.zeros((DEPTH, LANES), F32)]
    return jnp.stack(rows, axis=1)


def kernel(x_prompt, x_sample, cache_a_k, cache_a_v, cache_b_k, cache_b_v, cache_c_k, cache_c_v, c, c_ctx, w_mod, b_mod, norm1_g, norm2_g, w_in, a_lam_q1, a_lam_k1, a_lam_q2, a_lam_k2, a_subln_g, b_qnorm_g, b_knorm_g, c_sink, w_br_a, w_br_b, w_br_c, w_out, w_ffn_in, w_ffn_out, final_g):
    t_ctx = BATCH * SEQ
    t_lat = DEC_BATCH * DEC_SEQ
    xp = x_prompt.reshape(t_ctx, D_MODEL)
    xs = x_sample.reshape(t_lat, D_MODEL)
    cv8 = jnp.concatenate([c_ctx[None, :], c, jnp.zeros((8 - 1 - DEC_BATCH, D_MODEL), F32)], axis=0)
    mod = _modulation(cv8, w_mod, b_mod)
    cos_t, sin_t = _rope_tables()
    sp = _pack_small(a_lam_q1, a_lam_k1, a_lam_q2, a_lam_k2, a_subln_g, b_qnorm_g, b_knorm_g)
    n1 = norm1_g.reshape(DEPTH, 1, D_MODEL)
    n2 = norm2_g.reshape(DEPTH, 1, D_MODEL)
    caches = (cache_a_k.reshape(DEC_BATCH, DEPTH, PAST_LEN * A_HEADS, LANES),
              cache_a_v.reshape(DEC_BATCH, DEPTH, PAST_LEN * A_HEADS, LANES),
              cache_b_k.reshape(DEC_BATCH, DEPTH, PAST_LEN, LANES),
              cache_b_v.reshape(DEC_BATCH, DEPTH, PAST_LEN, LANES),
              cache_c_k.reshape(DEC_BATCH, DEPTH, PAST_LEN, LANES),
              cache_c_v.reshape(DEC_BATCH, DEPTH, PAST_LEN, LANES))
    st = [[] for _ in range(6)]
    for l in range(DEPTH):
        lam_init = 0.8 - 0.6 * math.exp(-0.3 * l)

        qkv_c, gates_c = _proj(xp, mod, n1, w_in, l, t_ctx, 0)
        att_c, ka, va, kbn = _att_ctx(qkv_c, sp, c_sink, l, lam_init)
        xp = _post(att_c, gates_c, xp, mod, w_br_a, w_br_b, w_br_c, w_out, l, t_ctx, 0)
        xp = _ffn(xp, mod, n2, w_ffn_in, w_ffn_out, l, t_ctx, 0)
        st[0].append(ka)
        st[1].append(va)
        st[2].append(kbn.reshape(BATCH, SEQ, 2, HEAD_DIM))
        st[3].append(qkv_c[:, VB:VB + LANES].reshape(BATCH, SEQ, 2, HEAD_DIM))
        st[4].append(qkv_c[:, KC:KC + LANES].reshape(BATCH, SEQ, 2, HEAD_DIM))
        st[5].append(qkv_c[:, VC:VC + LANES].reshape(BATCH, SEQ, 2, HEAD_DIM))

        qkv_s, gates_s = _proj(xs, mod, n1, w_in, l, DEC_SEQ, 1)
        att_s = _att_lat(qkv_s, caches, cos_t, sin_t, sp, c_sink, l, lam_init)
        xs = _post(att_s, gates_s, xs, mod, w_br_a, w_br_b, w_br_c, w_out, l, DEC_SEQ, 1)
        xs = _ffn(xs, mod, n2, w_ffn_in, w_ffn_out, l, DEC_SEQ, 1)

    fg = final_g.reshape(1, D_MODEL)
    y_prompt = _final_norm(xp, fg).reshape(BATCH, SEQ, D_MODEL)
    y_sample = _final_norm(xs, fg).reshape(DEC_BATCH, DEC_SEQ, D_MODEL)
    wide = tuple(jnp.stack(s, axis=1).reshape(BATCH, DEPTH, SEQ, A_HEADS, 2 * HEAD_DIM) for s in st[:2])
    return (y_prompt, y_sample) + wide + tuple(jnp.stack(s, axis=1) for s in st[2:])
```

```python
import functools
import math

import jax
import jax.numpy as jnp
from jax import lax
from jax.experimental import pallas as pl
from jax.experimental.pallas import tpu as pltpu

D_MODEL = 1024
BATCH = 16
SEQ = 256
DEPTH = 4
DEC_BATCH = 4
DEC_SEQ = 1024
PAST_LEN = 512
GRID_W = 64
HEAD_DIM = 64
ROPE_THETA = 10000.0
EPS = 1e-6
NEG_INF = -1e30
A_HEADS = 4
WINDOW = 128
D_FF = -(-8 * D_MODEL // (3 * 256)) * 256
N_MOD = 6
D_QKV = 3072
D_GATE = 3072
D_ATT = 1536
LANES = 128
LOG2E = math.log2(math.e)
QSCALE = HEAD_DIM ** -0.5 * LOG2E

QA, KA, VA, QB, KB, VB, QC, KC, VC = 0, 512, 1024, 1536, 2048, 2176, 2304, 2816, 2944
R_LQ1, R_LK1, R_LQ2, R_LK2, R_SUBG, R_BQG, R_BKG = range(7)
M_SH1, M_SC1, M_G1, M_SH2, M_SC2, M_G2 = range(6)

F32 = jnp.float32
BF16 = jnp.bfloat16
VMEM_LIMIT = 56 * 1024 * 1024


def _cparams(sem):
    return pltpu.CompilerParams(dimension_semantics=sem, vmem_limit_bytes=VMEM_LIMIT)


MOD_TN = 1536


def _mod_kernel(cv_ref, w_ref, b_ref, o_ref):
    cv = cv_ref[...]
    s = (cv * jax.nn.sigmoid(cv)).astype(BF16)
    o_ref[...] = jnp.dot(s, w_ref[...].astype(BF16), preferred_element_type=F32) + b_ref[...]


def _modulation(cv8, w_mod, b_mod):
    n = N_MOD * D_MODEL
    return pl.pallas_call(
        _mod_kernel,
        grid=(DEPTH, n // MOD_TN),
        in_specs=[
            pl.BlockSpec((8, D_MODEL), lambda l, j: (0, 0)),
            pl.BlockSpec((None, D_MODEL, MOD_TN), lambda l, j: (l, 0, j)),
            pl.BlockSpec((None, 1, MOD_TN), lambda l, j: (l, 0, j)),
        ],
        out_specs=pl.BlockSpec((None, 8, MOD_TN), lambda l, j: (l, 0, j)),
        out_shape=jax.ShapeDtypeStruct((DEPTH, 8, n), F32),
        compiler_params=_cparams(("arbitrary", "arbitrary")),
        name="modulation",
    )(cv8, w_mod, b_mod.reshape(DEPTH, 1, n))


NORM_CHUNK = 256


def _mod_block(mod_ref, row, blk):
    return mod_ref[pl.ds(row, 1), blk * D_MODEL:(blk + 1) * D_MODEL]


def _norm_mod_rows(x_ref, h_ref, g, mod_ref, sc_blk, sh_blk, rows, mod_row0, rows_per_mod):
    def body(i, carry):
        r = pl.ds(pl.multiple_of(i * NORM_CHUNK, NORM_CHUNK), NORM_CHUNK)
        row = mod_row0 + lax.div(i * NORM_CHUNK, rows_per_mod)
        x = x_ref[r, :]
        y = x * lax.rsqrt(jnp.mean(x * x, axis=-1, keepdims=True) + EPS) * g
        h_ref[r, :] = (y * (1.0 + _mod_block(mod_ref, row, sc_blk))
                       + _mod_block(mod_ref, row, sh_blk)).astype(BF16)
        return carry
    lax.fori_loop(0, rows // NORM_CHUNK, body, 0)


def _head_rmsnorm(x, g, lane_lo):
    x2 = x * x
    zero = jnp.zeros_like(x2)
    lo = jnp.sum(jnp.where(lane_lo, x2, zero), axis=-1, keepdims=True)
    hi = jnp.sum(jnp.where(lane_lo, zero, x2), axis=-1, keepdims=True)
    ms = jnp.where(lane_lo, lo, hi) * (1.0 / HEAD_DIM)
    return x * lax.rsqrt(ms + EPS) * g


def _rope(x, cos, sin, lane):
    partner = jnp.where((lane & 32) == 0, pltpu.roll(x, LANES - 32, 1), pltpu.roll(x, 32, 1))
    return x * cos + partner * sin


def _lo_hi(x, lane, src_hi, ones_lane=False):
    lane_lo = lane < 64
    other = pltpu.roll(x, 64, 1)
    zero = jnp.zeros_like(x)
    lo = jnp.where(lane_lo, other if src_hi else x, zero)
    hi = jnp.where(lane_lo, zero, x if src_hi else other)
    if ones_lane:
        lo = jnp.where(lane == 64, 1.0, lo)
        hi = jnp.where(lane == 0, 1.0, hi)
    return lo, hi


def _dot_nt(a, b):
    return lax.dot_general(a, b, (((1,), (1,)), ((), ())), preferred_element_type=F32)


def _probs(s, extra=None):
    m = jnp.max(s, axis=-1, keepdims=True)
    if extra is not None:
        m = jnp.maximum(m, extra)
    return jnp.exp2(s - m), m


def _lam(sp_ref, lam_init):
    dot1 = jnp.sum(sp_ref[R_LQ1:R_LQ1 + 1, :] * sp_ref[R_LK1:R_LK1 + 1, :], axis=-1, keepdims=True)
    dot2 = jnp.sum(sp_ref[R_LQ2:R_LQ2 + 1, :] * sp_ref[R_LK2:R_LK2 + 1, :], axis=-1, keepdims=True)
    return jnp.exp(dot1) - jnp.exp(dot2) + lam_init


def _subln(o, g, lam_init):
    return o * lax.rsqrt(jnp.mean(o * o, axis=-1, keepdims=True) + EPS) * g * (1.0 - lam_init)


def _mod_spec(layer):
    return pl.BlockSpec((None, 8, N_MOD * D_MODEL), lambda *_: (layer, 0, 0))


PROJ_TM = 2048
PROJ_TN = 512
N_QKV_TILES = D_QKV // PROJ_TN


def _proj_kernel(x_ref, mod_ref, g_ref, w_ref, qkv_ref, gate_ref, h_scr, w_scr, *, tm, rows_per_mod, row0):
    n = pl.program_id(0)
    m = pl.program_id(1)

    @pl.when(n == 0)
    def _():
        _norm_mod_rows(x_ref, h_scr.at[m], g_ref[...], mod_ref, M_SC1, M_SH1, tm,
                       row0 + lax.div(m * tm, rows_per_mod), rows_per_mod)

    @pl.when(m == 0)
    def _():
        w_scr[...] = w_ref[...].astype(BF16)

    @pl.when(n < N_QKV_TILES)
    def _():
        qkv_ref[...] = jnp.dot(h_scr[m], w_scr[...], preferred_element_type=F32)

    @pl.when(n >= N_QKV_TILES)
    def _():
        gate_ref[...] = jnp.dot(h_scr[m], w_scr[...], preferred_element_type=F32).astype(BF16)


def _proj(x, mod, g, w_in, layer, rows_per_mod, row0):
    t = x.shape[0]
    tm = PROJ_TM
    nm = t // tm
    return pl.pallas_call(
        functools.partial(_proj_kernel, tm=tm, rows_per_mod=rows_per_mod, row0=row0),
        grid=((D_QKV + D_GATE) // PROJ_TN, nm),
        in_specs=[
            pl.BlockSpec((tm, D_MODEL), lambda n, m: (jnp.where(n == 0, m, nm - 1), 0)),
            _mod_spec(layer),
            pl.BlockSpec((None, 1, D_MODEL), lambda n, m: (layer, 0, 0)),
            pl.BlockSpec((None, D_MODEL, PROJ_TN), lambda n, m: (layer, 0, n)),
        ],
        out_specs=[
            pl.BlockSpec((tm, PROJ_TN), lambda n, m: (jnp.where(n < N_QKV_TILES, m, nm - 1),
                                                      jnp.minimum(n, N_QKV_TILES - 1))),
            pl.BlockSpec((tm, PROJ_TN), lambda n, m: (jnp.where(n < N_QKV_TILES, 0, m),
                                                      jnp.maximum(n - N_QKV_TILES, 0))),
        ],
        out_shape=[jax.ShapeDtypeStruct((t, D_QKV), F32),
                   jax.ShapeDtypeStruct((t, D_GATE), BF16)],
        scratch_shapes=[pltpu.VMEM((nm, tm, D_MODEL), BF16), pltpu.VMEM((D_MODEL, PROJ_TN), BF16)],
        compiler_params=_cparams(("arbitrary", "arbitrary")),
        name="proj",
    )(x, mod, g, w_in)


CTX_NB = 1
SMALL_KB, SMALL_VB, SMALL_KC, SMALL_VC = (i * LANES for i in range(4))


def _att_ctx_kernel(qkv_ref, sp_ref, sink_ref, ka_in, va_in, small_in,
                    att_ref, ka_ref, va_ref, small_ref, *, layer, lam_init):
    del ka_in, va_in, small_in
    lane = lax.broadcasted_iota(jnp.int32, (SEQ, LANES), 1)
    lane_lo = lane < 64
    lam = _lam(sp_ref, lam_init)
    subg = sp_ref[R_SUBG:R_SUBG + 1, :]
    bqg = sp_ref[R_BQG:R_BQG + 1, :]
    bkg = sp_ref[R_BKG:R_BKG + 1, :]

    for i in range(CTX_NB):
        rows = slice(i * SEQ, (i + 1) * SEQ)
        tile = lambda c: qkv_ref[rows, c:c + LANES]

        for h in range(A_HEADS):
            q = (tile(QA + h * LANES) * QSCALE).astype(BF16)
            k = tile(KA + h * LANES)
            v = tile(VA + h * LANES)
            ka_ref[i, pl.ds(h, SEQ, stride=A_HEADS), :] = k
            va_ref[i, pl.ds(h, SEQ, stride=A_HEADS), :] = v
            v = v.astype(BF16)
            zero = jnp.zeros_like(k)
            kst = jnp.concatenate([jnp.where(lane_lo, k, zero).astype(BF16),
                                   jnp.where(lane_lo, zero, k).astype(BF16)], axis=0)
            s = _dot_nt(q, kst)
            p1, _ = _probs(s[:, :SEQ])
            p2, _ = _probs(s[:, SEQ:])
            o1 = jnp.dot(p1.astype(BF16), v, preferred_element_type=F32) / jnp.sum(p1, axis=-1, keepdims=True)
            o2 = jnp.dot(p2.astype(BF16), v, preferred_element_type=F32) / jnp.sum(p2, axis=-1, keepdims=True)
            o = _subln(o1 - lam * o2, subg, lam_init)
            att_ref[rows, h * LANES:(h + 1) * LANES] = o.astype(BF16)

        kb = _head_rmsnorm(tile(KB), bkg, lane_lo)
        vb, kc, vc = tile(VB), tile(KC), tile(VC)
        for x, c in ((kb, SMALL_KB), (vb, SMALL_VB), (kc, SMALL_KC), (vc, SMALL_VC)):
            small_ref[i, :, c:c + LANES] = x
        for mixer, (q0, k_t, v_t, o0) in enumerate(((QB, kb, vb, 512), (QC, kc, vc, 1024))):
            for g in range(2):
                k_lo, k_hi = _lo_hi(k_t, lane, g == 1)
                v_lo, v_hi = _lo_hi(v_t, lane, g == 1, ones_lane=True)
                kst = jnp.concatenate([k_lo.astype(BF16), k_hi.astype(BF16)], axis=0)
                v_lo = v_lo.astype(BF16)
                v_hi = v_hi.astype(BF16)
                for jj in range(2):
                    j = 2 * g + jj
                    q = tile(q0 + j * LANES)
                    if mixer == 0:
                        q = _head_rmsnorm(q, bqg, lane_lo)
                    s = _dot_nt((q * QSCALE).astype(BF16), kst)
                    if mixer == 0:
                        pe, _ = _probs(s[:, :SEQ])
                        po, _ = _probs(s[:, SEQ:])
                    else:
                        sink_e = sink_ref[layer, 2 * j] * LOG2E
                        sink_o = sink_ref[layer, 2 * j + 1] * LOG2E
                        pe, me = _probs(s[:, :SEQ], sink_e)
                        po, mo = _probs(s[:, SEQ:], sink_o)
                    oe = jnp.dot(pe.astype(BF16), v_lo, preferred_element_type=F32)
                    oo = jnp.dot(po.astype(BF16), v_hi, preferred_element_type=F32)
                    le = oe[:, 64:65]
                    lo_ = oo[:, 0:1]
                    if mixer == 1:
                        le = le + jnp.exp2(sink_e - me)
                        lo_ = lo_ + jnp.exp2(sink_o - mo)
                    o = jnp.where(lane_lo, oe, oo) / jnp.where(lane_lo, le, lo_)
                    att_ref[rows, o0 + j * LANES:o0 + (j + 1) * LANES] = o.astype(BF16)


def _att_ctx(qkv, sp, sink, ka_all, va_all, small_all, layer, lam_init):
    t = qkv.shape[0]
    nb = t // SEQ
    rows = CTX_NB * SEQ
    wide = pl.BlockSpec((CTX_NB, None, SEQ * A_HEADS, LANES), lambda b: (b, layer, 0, 0))
    small = pl.BlockSpec((CTX_NB, None, SEQ, 4 * LANES), lambda b: (b, layer, 0, 0))
    passthrough = pl.BlockSpec(memory_space=pl.ANY)
    return pl.pallas_call(
        functools.partial(_att_ctx_kernel, layer=layer, lam_init=lam_init),
        grid=(nb // CTX_NB,),
        in_specs=[
            pl.BlockSpec((rows, D_QKV), lambda b: (b, 0)),
            pl.BlockSpec((None, 8, LANES), lambda b: (layer, 0, 0)),
            pl.BlockSpec(memory_space=pltpu.SMEM),
            passthrough, passthrough, passthrough,
        ],
        out_specs=[pl.BlockSpec((rows, D_ATT), lambda b: (b, 0)), wide, wide, small],
        out_shape=[jax.ShapeDtypeStruct((t, D_ATT), BF16),
                   jax.ShapeDtypeStruct(ka_all.shape, F32),
                   jax.ShapeDtypeStruct(va_all.shape, F32),
                   jax.ShapeDtypeStruct(small_all.shape, F32)],
        input_output_aliases={3: 1, 4: 2, 5: 3},
        compiler_params=_cparams(("arbitrary",)),
        name="att_ctx",
    )(qkv, sp, sink, ka_all, va_all, small_all)


LAT_TQ = 256
NK = PAST_LEN + DEC_SEQ
KCH = 512
PREP_ROWS = 256
N_UNITS = 12


def _attend_all_queries(q, k_chunks, v_chunks, masks=None, sink=None, l_lane=None):
    m = l = acc = None
    for c, (k_c, v_c) in enumerate(zip(k_chunks, v_chunks)):
        s = _dot_nt(q, k_c)
        if masks is not None and masks[c] is not None:
            s = jnp.where(masks[c], s, NEG_INF)
        mc = jnp.max(s, axis=-1, keepdims=True)
        if m is None:
            m_new = mc if sink is None else jnp.maximum(mc, sink)
            p = jnp.exp2(s - m_new)
            acc = jnp.dot(p.astype(BF16), v_c, preferred_element_type=F32)
            if l_lane is None:
                l = jnp.sum(p, axis=-1, keepdims=True)
        else:
            m_new = jnp.maximum(m, mc)
            alpha = jnp.exp2(m - m_new)
            p = jnp.exp2(s - m_new)
            acc = alpha * acc + jnp.dot(p.astype(BF16), v_c, preferred_element_type=F32)
            if l_lane is None:
                l = alpha * l + jnp.sum(p, axis=-1, keepdims=True)
        m = m_new
    if l_lane is not None:
        l = acc[:, l_lane:l_lane + 1]
    if sink is not None:
        l = l + jnp.exp2(sink - m)
    return acc, l


def _att_lat_kernel(q_ref, kva_ref, kvv_ref, kv4_ref, kv5_ref,
                    cak_ref, cav_ref, cbk_ref, cbv_ref, cck_ref, ccv_ref,
                    cos_ref, sin_ref, sp_ref, sink_ref,
                    att_ref,
                    ka_scr, va_scr, kb_scr, vb_scr, kc_scr, vc_scr, q_scr,
                    *, layer, lam_init):
    u = pl.program_id(1)
    subg = sp_ref[R_SUBG:R_SUBG + 1, :]
    bqg = sp_ref[R_BQG:R_BQG + 1, :]
    bkg = sp_ref[R_BKG:R_BKG + 1, :]
    lane = lax.broadcasted_iota(jnp.int32, (LAT_TQ, LANES), 1)
    lane_lo = lane < 64

    @pl.when(u == 0)
    def _prepare_keys():
        def put(dst, a_k, a_v, pairs):
            lo_rows = pl.ds(pl.multiple_of(dst, PREP_ROWS), PREP_ROWS)
            hi_rows = pl.ds(pl.multiple_of(NK + dst, PREP_ROWS), PREP_ROWS)
            for h in range(A_HEADS):
                k = a_k[h]
                zero = jnp.zeros_like(k)
                ka_scr[h, lo_rows, :] = jnp.where(lane_lo, k, zero).astype(BF16)
                ka_scr[h, hi_rows, :] = jnp.where(lane_lo, zero, k).astype(BF16)
                va_scr[h, lo_rows, :] = a_v[h].astype(BF16)
            for x, scr, is_value in pairs:
                for g in range(2):
                    lo, hi = _lo_hi(x, lane, g == 1, ones_lane=is_value)
                    scr[g, lo_rows, :] = lo.astype(BF16)
                    scr[g, hi_rows, :] = hi.astype(BF16)

        def cached(i, carry):
            r = pl.ds(pl.multiple_of(i * PREP_ROWS, PREP_ROWS), PREP_ROWS)
            head_rows = lambda h: pl.ds(i * (PREP_ROWS * A_HEADS) + h, PREP_ROWS, stride=A_HEADS)
            put(i * PREP_ROWS,
                [cak_ref[head_rows(h), :] for h in range(A_HEADS)],
                [cav_ref[head_rows(h), :] for h in range(A_HEADS)],
                ((cbk_ref[r, :], kb_scr, False), (cbv_ref[r, :], vb_scr, True),
                 (cck_ref[r, :], kc_scr, False), (ccv_ref[r, :], vc_scr, True)))
            return carry
        lax.fori_loop(0, PAST_LEN // PREP_ROWS, cached, 0)

        def latent(i, carry):
            r = pl.ds(pl.multiple_of(i * PREP_ROWS, PREP_ROWS), PREP_ROWS)
            cos = cos_ref[r, :]
            sin = sin_ref[r, :]
            kb = _rope(_head_rmsnorm(kv4_ref[r, 0:LANES], bkg, lane_lo), cos, sin, lane)
            kc = _rope(kv5_ref[r, 256:256 + LANES], cos, sin, lane)
            put(PAST_LEN + i * PREP_ROWS,
                [_rope(kva_ref[r, h * LANES:(h + 1) * LANES], cos, sin, lane) for h in range(A_HEADS)],
                [kvv_ref[r, h * LANES:(h + 1) * LANES] for h in range(A_HEADS)],
                ((kb, kb_scr, False), (kv4_ref[r, LANES:2 * LANES], vb_scr, True),
                 (kc, kc_scr, False), (kv5_ref[r, 384:384 + LANES], vc_scr, True)))
            return carry
        lax.fori_loop(0, DEC_SEQ // PREP_ROWS, latent, 0)

    lam = _lam(sp_ref, lam_init)
    lane_lo_all = lax.broadcasted_iota(jnp.int32, (DEC_SEQ, LANES), 1) < 64
    chunks = lambda scr, i, base: [scr[i, base + c * KCH:base + (c + 1) * KCH, :] for c in range(NK // KCH)]

    def prepare_queries(normalise):
        def step(i, carry):
            r = pl.ds(pl.multiple_of(i * LAT_TQ, LAT_TQ), LAT_TQ)
            q = q_ref[r, :]
            if normalise:
                q = _head_rmsnorm(q, bqg, lane_lo)
            q_scr[r, :] = (_rope(q, cos_ref[r, :], sin_ref[r, :], lane) * QSCALE).astype(BF16)
            return carry
        lax.fori_loop(0, DEC_SEQ // LAT_TQ, step, 0)

    @pl.when(u < A_HEADS)
    def _mixer_a():
        prepare_queries(False)
        q = q_scr[...]
        v = chunks(va_scr, u, 0)
        a1, l1 = _attend_all_queries(q, chunks(ka_scr, u, 0), v)
        a2, l2 = _attend_all_queries(q, chunks(ka_scr, u, NK), v)
        att_ref[...] = _subln(a1 / l1 - lam * (a2 / l2), subg, lam_init).astype(BF16)

    def pair_unit(k_scr, v_scr, g, masks=None, sinks=(None, None)):
        q = q_scr[...]
        ae, le = _attend_all_queries(q, chunks(k_scr, g, 0), chunks(v_scr, g, 0),
                                     masks=masks, sink=sinks[0], l_lane=64)
        ao, lo_ = _attend_all_queries(q, chunks(k_scr, g, NK), chunks(v_scr, g, NK),
                                      masks=masks, sink=sinks[1], l_lane=0)
        o = jnp.where(lane_lo_all, ae, ao) / jnp.where(lane_lo_all, le, lo_)
        att_ref[...] = o.astype(BF16)

    @pl.when((u >= A_HEADS) & (u < A_HEADS + 4))
    def _mixer_b():
        prepare_queries(True)
        pair_unit(kb_scr, vb_scr, lax.shift_right_logical(u - A_HEADS, 1))

    @pl.when(u >= A_HEADS + 4)
    def _mixer_c():
        j = u - (A_HEADS + 4)
        prepare_queries(False)
        qpos = lax.broadcasted_iota(jnp.int32, (DEC_SEQ, KCH), 0)
        kcol = lax.broadcasted_iota(jnp.int32, (DEC_SEQ, KCH), 1)
        masks = (None,) + tuple(jnp.abs(kcol + (c * KCH - PAST_LEN) - qpos) <= WINDOW
                                for c in range(PAST_LEN // KCH, NK // KCH))
        pair_unit(kc_scr, vc_scr, lax.shift_right_logical(j, 1), masks=masks,
                  sinks=(sink_ref[layer, 2 * j] * LOG2E, sink_ref[layer, 2 * j + 1] * LOG2E))


def _unit_q_col(u):
    return jnp.where(u < A_HEADS, u, jnp.where(u < A_HEADS + 4, QB // LANES - A_HEADS + u,
                                               QC // LANES - A_HEADS - 4 + u))


def _att_lat(qkv, caches, cos, sin, sp, sink, layer, lam_init):
    t = qkv.shape[0]
    once = pl.Buffered(1)
    kv_blk = lambda col: pl.BlockSpec((DEC_SEQ, 512), lambda b, i: (b, col), pipeline_mode=once)
    cache_blk = lambda r: pl.BlockSpec((None, None, r, LANES), lambda b, i: (b, layer, 0, 0),
                                       pipeline_mode=once)
    return pl.pallas_call(
        functools.partial(_att_lat_kernel, layer=layer, lam_init=lam_init),
        grid=(DEC_BATCH, N_UNITS),
        in_specs=[
            pl.BlockSpec((DEC_SEQ, LANES), lambda b, i: (b, _unit_q_col(i))),
            kv_blk(KA // 512), kv_blk(VA // 512), kv_blk(4), kv_blk(5),
            cache_blk(PAST_LEN * A_HEADS), cache_blk(PAST_LEN * A_HEADS),
            cache_blk(PAST_LEN), cache_blk(PAST_LEN), cache_blk(PAST_LEN), cache_blk(PAST_LEN),
            pl.BlockSpec((DEC_SEQ, LANES), lambda b, i: (0, 0)),
            pl.BlockSpec((DEC_SEQ, LANES), lambda b, i: (0, 0)),
            pl.BlockSpec((None, 8, LANES), lambda b, i: (layer, 0, 0)),
            pl.BlockSpec(memory_space=pltpu.SMEM),
        ],
        out_specs=pl.BlockSpec((DEC_SEQ, LANES), lambda b, i: (b, i)),
        out_shape=jax.ShapeDtypeStruct((t, D_ATT), BF16),
        scratch_shapes=[
            pltpu.VMEM((A_HEADS, 2 * NK, LANES), BF16),
            pltpu.VMEM((A_HEADS, NK, LANES), BF16),
            pltpu.VMEM((2, 2 * NK, LANES), BF16),
            pltpu.VMEM((2, 2 * NK, LANES), BF16),
            pltpu.VMEM((2, 2 * NK, LANES), BF16),
            pltpu.VMEM((2, 2 * NK, LANES), BF16),
            pltpu.VMEM((DEC_SEQ, LANES), BF16),
        ],
        compiler_params=_cparams(("arbitrary", "arbitrary")),
        name="att_lat",
    )(qkv, qkv, qkv, qkv, qkv, *caches, cos, sin, sp, sink)


POST_TM = 512


def _post_kernel(att_ref, gate_ref, x_ref, mod_ref, wa_ref, wb_ref, wc_ref, wo_ref, o_ref,
                 wbr_scr, wo_scr, *, tm, rows_per_mod, row0):
    @pl.when(pl.program_id(0) == 0)
    def _():
        for i, w_ref in enumerate((wa_ref, wb_ref, wc_ref)):
            wbr_scr[i] = w_ref[...].astype(BF16)
        wo_scr[...] = wo_ref[...].astype(BF16)

    merged = None
    for i in range(3):
        y = jnp.dot(att_ref[:, i * 512:(i + 1) * 512], wbr_scr[i], preferred_element_type=F32)
        term = jax.nn.sigmoid(gate_ref[:, i * D_MODEL:(i + 1) * D_MODEL].astype(F32)) * y
        merged = term if merged is None else merged + term
    mixed = jnp.dot(merged.astype(BF16), wo_scr[...], preferred_element_type=F32)
    row = row0 + lax.div(pl.program_id(0) * tm, rows_per_mod)
    o_ref[...] = x_ref[...] + _mod_block(mod_ref, row, M_G1) * mixed


def _post(att, gates, x, mod, wa, wb, wc, wo, layer, rows_per_mod, row0):
    t = x.shape[0]
    tm = POST_TM
    full = lambda r: pl.BlockSpec((None, r, D_MODEL), lambda m: (layer, 0, 0),
                                  pipeline_mode=pl.Buffered(1))
    return pl.pallas_call(
        functools.partial(_post_kernel, tm=tm, rows_per_mod=rows_per_mod, row0=row0),
        grid=(t // tm,),
        in_specs=[
            pl.BlockSpec((tm, D_ATT), lambda m: (m, 0)),
            pl.BlockSpec((tm, D_GATE), lambda m: (m, 0)),
            pl.BlockSpec((tm, D_MODEL), lambda m: (m, 0)),
            _mod_spec(layer),
            full(512), full(512), full(512), full(D_MODEL),
        ],
        out_specs=pl.BlockSpec((tm, D_MODEL), lambda m: (m, 0)),
        out_shape=jax.ShapeDtypeStruct((t, D_MODEL), F32),
        scratch_shapes=[pltpu.VMEM((3, 512, D_MODEL), BF16), pltpu.VMEM((D_MODEL, D_MODEL), BF16)],
        compiler_params=_cparams(("arbitrary",)),
        name="post",
    )(att, gates, x, mod, wa, wb, wc, wo)


FFN_TM = 1024
FFN_TF = 256


FFN_GROUP = 2


def _ffn_kernel(x_ref, mod_ref, g_ref, fg_ref, wa_ref, wb_ref, wo_ref, o_ref,
                h_scr, acc_scr, wa_scr, wb_scr, wo_scr, *, tm, rows_per_mod, row0, final_norm):
    grp = pl.program_id(0)
    f = pl.program_id(1)
    m = pl.program_id(2)
    row = row0 + lax.div((grp * FFN_GROUP + m) * tm, rows_per_mod)

    @pl.when(f == 0)
    def _():
        _norm_mod_rows(x_ref, h_scr.at[m], g_ref[...], mod_ref, M_SC2, M_SH2, tm, row, rows_per_mod)
        acc_scr[m] = jnp.zeros((tm, D_MODEL), F32)

    @pl.when(m == 0)
    def _():
        wa_scr[...] = wa_ref[...].astype(BF16)
        wb_scr[...] = wb_ref[...].astype(BF16)
        wo_scr[...] = wo_ref[...].astype(BF16)

    h = h_scr[m]
    a = jnp.dot(h, wa_scr[...], preferred_element_type=F32)
    b = jnp.dot(h, wb_scr[...], preferred_element_type=F32)
    y = (a * jax.nn.sigmoid(a) * b).astype(BF16)
    acc_scr[m] += jnp.dot(y, wo_scr[...], preferred_element_type=F32)

    @pl.when(f == pl.num_programs(1) - 1)
    def _():
        y = x_ref[...] + _mod_block(mod_ref, row, M_G2) * acc_scr[m]
        if final_norm:
            y = y * lax.rsqrt(jnp.mean(y * y, axis=-1, keepdims=True) + EPS) * fg_ref[...]
        o_ref[...] = y


def _ffn(x, mod, g, final_g, w_in, w_out, layer, rows_per_mod, row0):
    t = x.shape[0]
    tm = FFN_TM
    assert rows_per_mod % tm == 0
    nf = D_FF // FFN_TF
    last = FFN_GROUP - 1
    x_idx = lambda grp, f, m: (grp * FFN_GROUP + jnp.where((f == 0) | (f == nf - 1), m, last), 0)
    o_idx = lambda grp, f, m: (grp * FFN_GROUP + jnp.where(f == nf - 1, m, 0), 0)
    return pl.pallas_call(
        functools.partial(_ffn_kernel, tm=tm, rows_per_mod=rows_per_mod, row0=row0,
                          final_norm=layer == DEPTH - 1),
        grid=(t // (tm * FFN_GROUP), nf, FFN_GROUP),
        in_specs=[
            pl.BlockSpec((tm, D_MODEL), x_idx),
            _mod_spec(layer),
            pl.BlockSpec((None, 1, D_MODEL), lambda grp, f, m: (layer, 0, 0)),
            pl.BlockSpec((1, D_MODEL), lambda grp, f, m: (0, 0)),
            pl.BlockSpec((None, D_MODEL, FFN_TF), lambda grp, f, m: (layer, 0, f)),
            pl.BlockSpec((None, D_MODEL, FFN_TF), lambda grp, f, m: (layer, 0, f + nf)),
            pl.BlockSpec((None, FFN_TF, D_MODEL), lambda grp, f, m: (layer, f, 0)),
        ],
        out_specs=pl.BlockSpec((tm, D_MODEL), o_idx),
        out_shape=jax.ShapeDtypeStruct((t, D_MODEL), F32),
        scratch_shapes=[pltpu.VMEM((FFN_GROUP, tm, D_MODEL), BF16),
                        pltpu.VMEM((FFN_GROUP, tm, D_MODEL), F32),
                        pltpu.VMEM((D_MODEL, FFN_TF), BF16),
                        pltpu.VMEM((D_MODEL, FFN_TF), BF16),
                        pltpu.VMEM((FFN_TF, D_MODEL), BF16)],
        compiler_params=_cparams(("arbitrary", "arbitrary", "arbitrary")),
        name="ffn",
    )(x, mod, g, final_g, w_in, w_in, w_out)


def _rope_tables():
    rows = DEC_SEQ // GRID_W
    row = jnp.repeat(jnp.arange(rows), GRID_W).astype(F32)
    col = jnp.tile(jnp.arange(GRID_W), rows).astype(F32)
    n = HEAD_DIM // 4
    inv = ROPE_THETA ** (-jnp.arange(n, dtype=F32) / n)
    ang = jnp.concatenate([row[:, None] * inv, col[:, None] * inv], axis=-1)
    cos, sin = jnp.cos(ang), jnp.sin(ang)
    cos_t = jnp.tile(cos, (1, 4))
    sin_t = jnp.tile(jnp.concatenate([-sin, sin], axis=-1), (1, 2))
    return cos_t, sin_t


def _pack_small(a_lam_q1, a_lam_k1, a_lam_q2, a_lam_k2, a_subln_g, b_qnorm_g, b_knorm_g):
    pad = lambda v: jnp.pad(v, ((0, 0), (0, LANES - HEAD_DIM)))
    rows = [pad(a_lam_q1), pad(a_lam_k1), pad(a_lam_q2), pad(a_lam_k2), a_subln_g,
            jnp.tile(b_qnorm_g, (1, 2)), jnp.tile(b_knorm_g, (1, 2)),
            jnp.zeros((DEPTH, LANES), F32)]
    return jnp.stack(rows, axis=1)


def kernel(x_prompt, x_sample, cache_a_k, cache_a_v, cache_b_k, cache_b_v, cache_c_k, cache_c_v, c, c_ctx, w_mod, b_mod, norm1_g, norm2_g, w_in, a_lam_q1, a_lam_k1, a_lam_q2, a_lam_k2, a_subln_g, b_qnorm_g, b_knorm_g, c_sink, w_br_a, w_br_b, w_br_c, w_out, w_ffn_in, w_ffn_out, final_g):
    t_ctx = BATCH * SEQ
    t_lat = DEC_BATCH * DEC_SEQ
    xp = x_prompt.reshape(t_ctx, D_MODEL)
    xs = x_sample.reshape(t_lat, D_MODEL)
    cv8 = jnp.concatenate([c_ctx[None, :], c, jnp.zeros((8 - 1 - DEC_BATCH, D_MODEL), F32)], axis=0)
    mod = _modulation(cv8, w_mod, b_mod)
    cos_t, sin_t = _rope_tables()
    sp = _pack_small(a_lam_q1, a_lam_k1, a_lam_q2, a_lam_k2, a_subln_g, b_qnorm_g, b_knorm_g)
    n1 = norm1_g.reshape(DEPTH, 1, D_MODEL)
    n2 = norm2_g.reshape(DEPTH, 1, D_MODEL)
    caches = (cache_a_k.reshape(DEC_BATCH, DEPTH, PAST_LEN * A_HEADS, LANES),
              cache_a_v.reshape(DEC_BATCH, DEPTH, PAST_LEN * A_HEADS, LANES),
              cache_b_k.reshape(DEC_BATCH, DEPTH, PAST_LEN, LANES),
              cache_b_v.reshape(DEC_BATCH, DEPTH, PAST_LEN, LANES),
              cache_c_k.reshape(DEC_BATCH, DEPTH, PAST_LEN, LANES),
              cache_c_v.reshape(DEC_BATCH, DEPTH, PAST_LEN, LANES))
    fg = final_g.reshape(1, D_MODEL)
    ka_all = jnp.zeros((BATCH, DEPTH, SEQ * A_HEADS, LANES), F32)
    va_all = jnp.zeros((BATCH, DEPTH, SEQ * A_HEADS, LANES), F32)
    small_all = jnp.zeros((BATCH, DEPTH, SEQ, 4 * LANES), F32)
    for l in range(DEPTH):
        lam_init = 0.8 - 0.6 * math.exp(-0.3 * l)

        qkv_c, gates_c = _proj(xp, mod, n1, w_in, l, t_ctx, 0)
        att_c, ka_all, va_all, small_all = _att_ctx(qkv_c, sp, c_sink, ka_all, va_all, small_all,
                                                    l, lam_init)
        xp = _post(att_c, gates_c, xp, mod, w_br_a, w_br_b, w_br_c, w_out, l, t_ctx, 0)
        xp = _ffn(xp, mod, n2, fg, w_ffn_in, w_ffn_out, l, t_ctx, 0)

        qkv_s, gates_s = _proj(xs, mod, n1, w_in, l, DEC_SEQ, 1)
        att_s = _att_lat(qkv_s, caches, cos_t, sin_t, sp, c_sink, l, lam_init)
        xs = _post(att_s, gates_s, xs, mod, w_br_a, w_br_b, w_br_c, w_out, l, DEC_SEQ, 1)
        xs = _ffn(xs, mod, n2, fg, w_ffn_in, w_ffn_out, l, DEC_SEQ, 1)

    y_prompt = xp.reshape(BATCH, SEQ, D_MODEL)
    y_sample = xs.reshape(DEC_BATCH, DEC_SEQ, D_MODEL)
    wide = tuple(a.reshape(BATCH, DEPTH, SEQ, A_HEADS, 2 * HEAD_DIM) for a in (ka_all, va_all))
    small = tuple(small_all[..., c:c + LANES].reshape(BATCH, DEPTH, SEQ, 2, HEAD_DIM)
                  for c in (SMALL_KB, SMALL_VB, SMALL_KC, SMALL_VC))
    return (y_prompt, y_sample) + wide + small
```

```python
import functools
import math

import jax
import jax.numpy as jnp
from jax import lax
from jax.experimental import pallas as pl
from jax.experimental.pallas import tpu as pltpu

D_MODEL = 1024
BATCH = 16
SEQ = 256
DEPTH = 4
DEC_BATCH = 4
DEC_SEQ = 1024
PAST_LEN = 512
GRID_W = 64
HEAD_DIM = 64
ROPE_THETA = 10000.0
EPS = 1e-6
NEG_INF = -1e30
A_HEADS = 4
WINDOW = 128
D_FF = -(-8 * D_MODEL // (3 * 256)) * 256
N_MOD = 6
D_QKV = 3072
D_GATE = 3072
D_ATT = 1536
LANES = 128
LOG2E = math.log2(math.e)
QSCALE = HEAD_DIM ** -0.5 * LOG2E

QA, KA, VA, QB, KB, VB, QC, KC, VC = 0, 512, 1024, 1536, 2048, 2176, 2304, 2816, 2944
R_LQ1, R_LK1, R_LQ2, R_LK2, R_SUBG, R_BQG, R_BKG = range(7)
M_SH1, M_SC1, M_G1, M_SH2, M_SC2, M_G2 = range(6)

F32 = jnp.float32
BF16 = jnp.bfloat16
VMEM_LIMIT = 56 * 1024 * 1024


def _cparams(sem):
    return pltpu.CompilerParams(dimension_semantics=sem, vmem_limit_bytes=VMEM_LIMIT)


MOD_TN = 1536


def _mod_kernel(cv_ref, w_ref, b_ref, o_ref):
    cv = cv_ref[...]
    s = (cv * jax.nn.sigmoid(cv)).astype(BF16)
    o_ref[...] = jnp.dot(s, w_ref[...].astype(BF16), preferred_element_type=F32) + b_ref[...]


def _modulation(cv8, w_mod, b_mod):
    n = N_MOD * D_MODEL
    return pl.pallas_call(
        _mod_kernel,
        grid=(DEPTH, n // MOD_TN),
        in_specs=[
            pl.BlockSpec((8, D_MODEL), lambda l, j: (0, 0)),
            pl.BlockSpec((None, D_MODEL, MOD_TN), lambda l, j: (l, 0, j)),
            pl.BlockSpec((None, 1, MOD_TN), lambda l, j: (l, 0, j)),
        ],
        out_specs=pl.BlockSpec((None, 8, MOD_TN), lambda l, j: (l, 0, j)),
        out_shape=jax.ShapeDtypeStruct((DEPTH, 8, n), F32),
        compiler_params=_cparams(("arbitrary", "arbitrary")),
        name="modulation",
    )(cv8, w_mod, b_mod.reshape(DEPTH, 1, n))


NORM_CHUNK = 256


def _mod_block(mod_ref, row, blk):
    return mod_ref[pl.ds(row, 1), blk * D_MODEL:(blk + 1) * D_MODEL]


def _norm_mod_rows(x_ref, h_ref, g, mod_ref, sc_blk, sh_blk, rows, mod_row0, rows_per_mod):
    def body(i, carry):
        r = pl.ds(pl.multiple_of(i * NORM_CHUNK, NORM_CHUNK), NORM_CHUNK)
        row = mod_row0 + lax.div(i * NORM_CHUNK, rows_per_mod)
        x = x_ref[r, :]
        y = x * lax.rsqrt(jnp.mean(x * x, axis=-1, keepdims=True) + EPS) * g
        h_ref[r, :] = (y * (1.0 + _mod_block(mod_ref, row, sc_blk))
                       + _mod_block(mod_ref, row, sh_blk)).astype(BF16)
        return carry
    lax.fori_loop(0, rows // NORM_CHUNK, body, 0)


def _head_rmsnorm(x, g, lane_lo):
    x2 = x * x
    zero = jnp.zeros_like(x2)
    lo = jnp.sum(jnp.where(lane_lo, x2, zero), axis=-1, keepdims=True)
    hi = jnp.sum(jnp.where(lane_lo, zero, x2), axis=-1, keepdims=True)
    ms = jnp.where(lane_lo, lo, hi) * (1.0 / HEAD_DIM)
    return x * lax.rsqrt(ms + EPS) * g


def _rope(x, cos, sin, lane):
    partner = jnp.where((lane & 32) == 0, pltpu.roll(x, LANES - 32, 1), pltpu.roll(x, 32, 1))
    return x * cos + partner * sin


def _lo_hi(x, lane, src_hi, ones_lane=False):
    lane_lo = lane < 64
    other = pltpu.roll(x, 64, 1)
    zero = jnp.zeros_like(x)
    lo = jnp.where(lane_lo, other if src_hi else x, zero)
    hi = jnp.where(lane_lo, zero, x if src_hi else other)
    if ones_lane:
        lo = jnp.where(lane == 64, 1.0, lo)
        hi = jnp.where(lane == 0, 1.0, hi)
    return lo, hi


def _split_heads(q, lane_lo):
    zero = jnp.zeros_like(q)
    return jnp.concatenate([jnp.where(lane_lo, q, zero), jnp.where(lane_lo, zero, q)], axis=0)


def _dot_nt(a, b):
    return lax.dot_general(a, b, (((1,), (1,)), ((), ())), preferred_element_type=F32)


def _probs(s, extra=None):
    m = jnp.max(s, axis=-1, keepdims=True)
    if extra is not None:
        m = jnp.maximum(m, extra)
    return jnp.exp2(s - m), m


def _lam(sp_ref, lam_init):
    dot1 = jnp.sum(sp_ref[R_LQ1:R_LQ1 + 1, :] * sp_ref[R_LK1:R_LK1 + 1, :], axis=-1, keepdims=True)
    dot2 = jnp.sum(sp_ref[R_LQ2:R_LQ2 + 1, :] * sp_ref[R_LK2:R_LK2 + 1, :], axis=-1, keepdims=True)
    return jnp.exp(dot1) - jnp.exp(dot2) + lam_init


def _subln(o, g, lam_init):
    return o * lax.rsqrt(jnp.mean(o * o, axis=-1, keepdims=True) + EPS) * g * (1.0 - lam_init)


def _mod_spec(layer):
    return pl.BlockSpec((None, 8, N_MOD * D_MODEL), lambda *_: (layer, 0, 0))


PROJ_TM = 2048
PROJ_TN = 512
N_QKV_TILES = D_QKV // PROJ_TN


def _proj_kernel(x_ref, mod_ref, g_ref, w_ref, qkv_ref, gate_ref, h_scr, w_scr, *, tm, rows_per_mod, row0):
    n = pl.program_id(0)
    m = pl.program_id(1)

    @pl.when(n == 0)
    def _():
        _norm_mod_rows(x_ref, h_scr.at[m], g_ref[...], mod_ref, M_SC1, M_SH1, tm,
                       row0 + lax.div(m * tm, rows_per_mod), rows_per_mod)

    @pl.when(m == 0)
    def _():
        w_scr[...] = w_ref[...].astype(BF16)

    @pl.when(n < N_QKV_TILES)
    def _():
        qkv_ref[...] = jnp.dot(h_scr[m], w_scr[...], preferred_element_type=F32)

    @pl.when(n >= N_QKV_TILES)
    def _():
        gate_ref[...] = jnp.dot(h_scr[m], w_scr[...], preferred_element_type=F32).astype(BF16)


def _proj(x, mod, g, w_in, layer, rows_per_mod, row0):
    t = x.shape[0]
    tm = PROJ_TM
    nm = t // tm
    return pl.pallas_call(
        functools.partial(_proj_kernel, tm=tm, rows_per_mod=rows_per_mod, row0=row0),
        grid=((D_QKV + D_GATE) // PROJ_TN, nm),
        in_specs=[
            pl.BlockSpec((tm, D_MODEL), lambda n, m: (jnp.where(n == 0, m, nm - 1), 0)),
            _mod_spec(layer),
            pl.BlockSpec((None, 1, D_MODEL), lambda n, m: (layer, 0, 0)),
            pl.BlockSpec((None, D_MODEL, PROJ_TN), lambda n, m: (layer, 0, n)),
        ],
        out_specs=[
            pl.BlockSpec((tm, PROJ_TN), lambda n, m: (jnp.where(n < N_QKV_TILES, m, nm - 1),
                                                      jnp.minimum(n, N_QKV_TILES - 1))),
            pl.BlockSpec((tm, PROJ_TN), lambda n, m: (jnp.where(n < N_QKV_TILES, 0, m),
                                                      jnp.maximum(n - N_QKV_TILES, 0))),
        ],
        out_shape=[jax.ShapeDtypeStruct((t, D_QKV), F32),
                   jax.ShapeDtypeStruct((t, D_GATE), BF16)],
        scratch_shapes=[pltpu.VMEM((nm, tm, D_MODEL), BF16), pltpu.VMEM((D_MODEL, PROJ_TN), BF16)],
        compiler_params=_cparams(("arbitrary", "arbitrary")),
        name="proj",
    )(x, mod, g, w_in)


CTX_NB = 1
SMALL_KB, SMALL_VB, SMALL_KC, SMALL_VC = (i * LANES for i in range(4))


def _att_ctx_kernel(qkv_ref, sp_ref, sink_ref, ka_in, va_in, small_in,
                    att_ref, ka_ref, va_ref, small_ref, *, layer, lam_init):
    del ka_in, va_in, small_in
    lane = lax.broadcasted_iota(jnp.int32, (SEQ, LANES), 1)
    lane_lo = lane < 64
    lam = _lam(sp_ref, lam_init)
    subg = sp_ref[R_SUBG:R_SUBG + 1, :]
    bqg = sp_ref[R_BQG:R_BQG + 1, :]
    bkg = sp_ref[R_BKG:R_BKG + 1, :]

    for i in range(CTX_NB):
        rows = slice(i * SEQ, (i + 1) * SEQ)
        tile = lambda c: qkv_ref[rows, c:c + LANES]

        for h in range(A_HEADS):
            q = _split_heads(tile(QA + h * LANES) * QSCALE, lane_lo).astype(BF16)
            k = tile(KA + h * LANES)
            v = tile(VA + h * LANES)
            ka_ref[i, pl.ds(h, SEQ, stride=A_HEADS), :] = k
            va_ref[i, pl.ds(h, SEQ, stride=A_HEADS), :] = v
            v = v.astype(BF16)
            s = _dot_nt(q, k.astype(BF16))
            p1, _ = _probs(s[:SEQ])
            p2, _ = _probs(s[SEQ:])
            o1 = jnp.dot(p1.astype(BF16), v, preferred_element_type=F32) / jnp.sum(p1, axis=-1, keepdims=True)
            o2 = jnp.dot(p2.astype(BF16), v, preferred_element_type=F32) / jnp.sum(p2, axis=-1, keepdims=True)
            o = _subln(o1 - lam * o2, subg, lam_init)
            att_ref[rows, h * LANES:(h + 1) * LANES] = o.astype(BF16)

        kb = _head_rmsnorm(tile(KB), bkg, lane_lo)
        vb, kc, vc = tile(VB), tile(KC), tile(VC)
        for x, c in ((kb, SMALL_KB), (vb, SMALL_VB), (kc, SMALL_KC), (vc, SMALL_VC)):
            small_ref[i, :, c:c + LANES] = x
        for mixer, (q0, k_t, v_t, o0) in enumerate(((QB, kb, vb, 512), (QC, kc, vc, 1024))):
            for g in range(2):
                k_lo, k_hi = _lo_hi(k_t, lane, g == 1)
                v_lo, v_hi = _lo_hi(v_t, lane, g == 1, ones_lane=True)
                k_both = (k_lo + k_hi).astype(BF16)
                v_lo = v_lo.astype(BF16)
                v_hi = v_hi.astype(BF16)
                for jj in range(2):
                    j = 2 * g + jj
                    q = tile(q0 + j * LANES)
                    if mixer == 0:
                        q = _head_rmsnorm(q, bqg, lane_lo)
                    s = _dot_nt(_split_heads(q * QSCALE, lane_lo).astype(BF16), k_both)
                    if mixer == 0:
                        pe, _ = _probs(s[:SEQ])
                        po, _ = _probs(s[SEQ:])
                    else:
                        sink_e = sink_ref[layer, 2 * j] * LOG2E
                        sink_o = sink_ref[layer, 2 * j + 1] * LOG2E
                        pe, me = _probs(s[:SEQ], sink_e)
                        po, mo = _probs(s[SEQ:], sink_o)
                    oe = jnp.dot(pe.astype(BF16), v_lo, preferred_element_type=F32)
                    oo = jnp.dot(po.astype(BF16), v_hi, preferred_element_type=F32)
                    le = oe[:, 64:65]
                    lo_ = oo[:, 0:1]
                    if mixer == 1:
                        le = le + jnp.exp2(sink_e - me)
                        lo_ = lo_ + jnp.exp2(sink_o - mo)
                    o = jnp.where(lane_lo, oe, oo) / jnp.where(lane_lo, le, lo_)
                    att_ref[rows, o0 + j * LANES:o0 + (j + 1) * LANES] = o.astype(BF16)


def _att_ctx(qkv, sp, sink, cache_out, layer, lam_init):
    t = qkv.shape[0]
    nb = t // SEQ
    rows = CTX_NB * SEQ
    wide = pl.BlockSpec((CTX_NB, None, SEQ * A_HEADS, LANES), lambda b: (b, layer, 0, 0))
    small = pl.BlockSpec((CTX_NB, None, SEQ, 4 * LANES), lambda b: (b, layer, 0, 0))
    wide_shape = jax.ShapeDtypeStruct((nb, DEPTH, SEQ * A_HEADS, LANES), F32)
    small_shape = jax.ShapeDtypeStruct((nb, DEPTH, SEQ, 4 * LANES), F32)
    if cache_out is None:
        cache_out = (jnp.zeros(wide_shape.shape, F32), jnp.zeros(wide_shape.shape, F32),
                     jnp.zeros(small_shape.shape, F32))
    passthrough = pl.BlockSpec(memory_space=pl.ANY)
    att, *cache_out = pl.pallas_call(
        functools.partial(_att_ctx_kernel, layer=layer, lam_init=lam_init),
        grid=(nb // CTX_NB,),
        in_specs=[
            pl.BlockSpec((rows, D_QKV), lambda b: (b, 0)),
            pl.BlockSpec((None, 8, LANES), lambda b: (layer, 0, 0)),
            pl.BlockSpec(memory_space=pltpu.SMEM),
            passthrough, passthrough, passthrough,
        ],
        out_specs=[pl.BlockSpec((rows, D_ATT), lambda b: (b, 0)), wide, wide, small],
        out_shape=[jax.ShapeDtypeStruct((t, D_ATT), BF16), wide_shape, wide_shape, small_shape],
        input_output_aliases={3: 1, 4: 2, 5: 3},
        compiler_params=_cparams(("arbitrary",)),
        name="att_ctx",
    )(qkv, sp, sink, *cache_out)
    return att, cache_out


LAT_TQ_NORM = 256
NK = PAST_LEN + DEC_SEQ
KCH = 512
PREP_ROWS = 512
N_UNITS = 12


def _attend_all_queries(q, k_chunks, v_chunks, masks=None, sink=None, l_lane=None):
    m = l = acc = None
    for c, (k_c, v_c) in enumerate(zip(k_chunks, v_chunks)):
        s = _dot_nt(q, k_c)
        if masks is not None and masks[c] is not None:
            s = jnp.where(masks[c], s, NEG_INF)
        mc = jnp.max(s, axis=-1, keepdims=True)
        if m is None:
            m_new = mc if sink is None else jnp.maximum(mc, sink)
            p = jnp.exp2(s - m_new)
            acc = jnp.dot(p.astype(BF16), v_c, preferred_element_type=F32)
            if l_lane is None:
                l = jnp.sum(p, axis=-1, keepdims=True)
        else:
            m_new = jnp.maximum(m, mc)
            alpha = jnp.exp2(m - m_new)
            p = jnp.exp2(s - m_new)
            acc = alpha * acc + jnp.dot(p.astype(BF16), v_c, preferred_element_type=F32)
            if l_lane is None:
                l = alpha * l + jnp.sum(p, axis=-1, keepdims=True)
        m = m_new
    if l_lane is not None:
        l = acc[:, l_lane:l_lane + 1]
    if sink is not None:
        l = l + jnp.exp2(sink - m)
    return acc, l


def _att_lat_kernel(q_ref, kva_ref, kvv_ref, kv4_ref, kv5_ref,
                    cak_ref, cav_ref, cbk_ref, cbv_ref, cck_ref, ccv_ref,
                    cos_ref, sin_ref, sp_ref, sink_ref,
                    att_ref,
                    ka_scr, va_scr, kb_scr, vb_scr, kc_scr, vc_scr, q_scr,
                    *, layer, lam_init):
    u = pl.program_id(1)
    subg = sp_ref[R_SUBG:R_SUBG + 1, :]
    bqg = sp_ref[R_BQG:R_BQG + 1, :]
    bkg = sp_ref[R_BKG:R_BKG + 1, :]
    @pl.when(u == 0)
    def _prepare_keys():
        lane = lax.broadcasted_iota(jnp.int32, (PREP_ROWS, LANES), 1)
        lane_lo = lane < 64
        def put(dst, a_k, a_v, pairs):
            lo_rows = pl.ds(pl.multiple_of(dst, PREP_ROWS), PREP_ROWS)
            hi_rows = pl.ds(pl.multiple_of(NK + dst, PREP_ROWS), PREP_ROWS)
            for h in range(A_HEADS):
                k = a_k[h]
                zero = jnp.zeros_like(k)
                ka_scr[h, lo_rows, :] = jnp.where(lane_lo, k, zero).astype(BF16)
                ka_scr[h, hi_rows, :] = jnp.where(lane_lo, zero, k).astype(BF16)
                va_scr[h, lo_rows, :] = a_v[h].astype(BF16)
            for x, scr, is_value in pairs:
                for g in range(2):
                    lo, hi = _lo_hi(x, lane, g == 1, ones_lane=is_value)
                    scr[g, lo_rows, :] = lo.astype(BF16)
                    scr[g, hi_rows, :] = hi.astype(BF16)

        def cached(i, carry):
            r = pl.ds(pl.multiple_of(i * PREP_ROWS, PREP_ROWS), PREP_ROWS)
            head_rows = lambda h: pl.ds(i * (PREP_ROWS * A_HEADS) + h, PREP_ROWS, stride=A_HEADS)
            put(i * PREP_ROWS,
                [cak_ref[head_rows(h), :] for h in range(A_HEADS)],
                [cav_ref[head_rows(h), :] for h in range(A_HEADS)],
                ((cbk_ref[r, :], kb_scr, False), (cbv_ref[r, :], vb_scr, True),
                 (cck_ref[r, :], kc_scr, False), (ccv_ref[r, :], vc_scr, True)))
            return carry
        lax.fori_loop(0, PAST_LEN // PREP_ROWS, cached, 0)

        def latent(i, carry):
            r = pl.ds(pl.multiple_of(i * PREP_ROWS, PREP_ROWS), PREP_ROWS)
            cos = cos_ref[r, :]
            sin = sin_ref[r, :]
            kb = _rope(_head_rmsnorm(kv4_ref[r, 0:LANES], bkg, lane_lo), cos, sin, lane)
            kc = _rope(kv5_ref[r, 256:256 + LANES], cos, sin, lane)
            put(PAST_LEN + i * PREP_ROWS,
                [_rope(kva_ref[r, h * LANES:(h + 1) * LANES], cos, sin, lane) for h in range(A_HEADS)],
                [kvv_ref[r, h * LANES:(h + 1) * LANES] for h in range(A_HEADS)],
                ((kb, kb_scr, False), (kv4_ref[r, LANES:2 * LANES], vb_scr, True),
                 (kc, kc_scr, False), (kv5_ref[r, 384:384 + LANES], vc_scr, True)))
            return carry
        lax.fori_loop(0, DEC_SEQ // PREP_ROWS, latent, 0)

    lam = _lam(sp_ref, lam_init)
    lane_lo_all = lax.broadcasted_iota(jnp.int32, (DEC_SEQ, LANES), 1) < 64
    chunks = lambda scr, i, base: [scr[i, base + c * KCH:base + (c + 1) * KCH, :] for c in range(NK // KCH)]

    def prepare_queries(normalise):
        tq = LAT_TQ_NORM if normalise else DEC_SEQ
        lane = lax.broadcasted_iota(jnp.int32, (tq, LANES), 1)
        lane_lo = lane < 64

        def step(i, carry):
            r = pl.ds(pl.multiple_of(i * tq, tq), tq)
            q = q_ref[r, :]
            if normalise:
                q = _head_rmsnorm(q, bqg, lane_lo)
            q_scr[r, :] = (_rope(q, cos_ref[r, :], sin_ref[r, :], lane) * QSCALE).astype(BF16)
            return carry
        lax.fori_loop(0, DEC_SEQ // tq, step, 0)

    @pl.when(u < A_HEADS)
    def _mixer_a():
        prepare_queries(False)
        q = q_scr[...]
        v = chunks(va_scr, u, 0)
        a1, l1 = _attend_all_queries(q, chunks(ka_scr, u, 0), v)
        a2, l2 = _attend_all_queries(q, chunks(ka_scr, u, NK), v)
        att_ref[...] = _subln(a1 / l1 - lam * (a2 / l2), subg, lam_init).astype(BF16)

    def pair_unit(k_scr, v_scr, g, masks=None, sinks=(None, None)):
        q = q_scr[...]
        ae, le = _attend_all_queries(q, chunks(k_scr, g, 0), chunks(v_scr, g, 0),
                                     masks=masks, sink=sinks[0], l_lane=64)
        ao, lo_ = _attend_all_queries(q, chunks(k_scr, g, NK), chunks(v_scr, g, NK),
                                      masks=masks, sink=sinks[1], l_lane=0)
        o = jnp.where(lane_lo_all, ae, ao) / jnp.where(lane_lo_all, le, lo_)
        att_ref[...] = o.astype(BF16)

    @pl.when((u >= A_HEADS) & (u < A_HEADS + 4))
    def _mixer_b():
        prepare_queries(True)
        pair_unit(kb_scr, vb_scr, lax.shift_right_logical(u - A_HEADS, 1))

    @pl.when(u >= A_HEADS + 4)
    def _mixer_c():
        j = u - (A_HEADS + 4)
        prepare_queries(False)
        qpos = lax.broadcasted_iota(jnp.int32, (DEC_SEQ, KCH), 0)
        kcol = lax.broadcasted_iota(jnp.int32, (DEC_SEQ, KCH), 1)
        masks = (None,) * (PAST_LEN // KCH) + tuple(
            jnp.abs(kcol + k0 - qpos) <= WINDOW for k0 in range(0, DEC_SEQ, KCH))
        pair_unit(kc_scr, vc_scr, lax.shift_right_logical(j, 1), masks=masks,
                  sinks=(sink_ref[layer, 2 * j] * LOG2E, sink_ref[layer, 2 * j + 1] * LOG2E))


def _unit_q_col(u):
    return jnp.where(u < A_HEADS, u, jnp.where(u < A_HEADS + 4, QB // LANES - A_HEADS + u,
                                               QC // LANES - A_HEADS - 4 + u))


def _att_lat(qkv, caches, cos, sin, sp, sink, layer, lam_init):
    t = qkv.shape[0]
    once = pl.Buffered(1)
    kv_blk = lambda col: pl.BlockSpec((DEC_SEQ, 512), lambda b, i: (b, col), pipeline_mode=once)
    cache_blk = lambda r: pl.BlockSpec((None, None, r, LANES), lambda b, i: (b, layer, 0, 0),
                                       pipeline_mode=once)
    return pl.pallas_call(
        functools.partial(_att_lat_kernel, layer=layer, lam_init=lam_init),
        grid=(DEC_BATCH, N_UNITS),
        in_specs=[
            pl.BlockSpec((DEC_SEQ, LANES), lambda b, i: (b, _unit_q_col(i))),
            kv_blk(KA // 512), kv_blk(VA // 512), kv_blk(4), kv_blk(5),
            cache_blk(PAST_LEN * A_HEADS), cache_blk(PAST_LEN * A_HEADS),
            cache_blk(PAST_LEN), cache_blk(PAST_LEN), cache_blk(PAST_LEN), cache_blk(PAST_LEN),
            pl.BlockSpec((DEC_SEQ, LANES), lambda b, i: (0, 0)),
            pl.BlockSpec((DEC_SEQ, LANES), lambda b, i: (0, 0)),
            pl.BlockSpec((None, 8, LANES), lambda b, i: (layer, 0, 0)),
            pl.BlockSpec(memory_space=pltpu.SMEM),
        ],
        out_specs=pl.BlockSpec((DEC_SEQ, LANES), lambda b, i: (b, i)),
        out_shape=jax.ShapeDtypeStruct((t, D_ATT), BF16),
        scratch_shapes=[
            pltpu.VMEM((A_HEADS, 2 * NK, LANES), BF16),
            pltpu.VMEM((A_HEADS, NK, LANES), BF16),
            pltpu.VMEM((2, 2 * NK, LANES), BF16),
            pltpu.VMEM((2, 2 * NK, LANES), BF16),
            pltpu.VMEM((2, 2 * NK, LANES), BF16),
            pltpu.VMEM((2, 2 * NK, LANES), BF16),
            pltpu.VMEM((DEC_SEQ, LANES), BF16),
        ],
        compiler_params=_cparams(("arbitrary", "arbitrary")),
        name="att_lat",
    )(qkv, qkv, qkv, qkv, qkv, *caches, cos, sin, sp, sink)


POST_TM = 512


def _post_kernel(att_ref, gate_ref, x_ref, mod_ref, wa_ref, wb_ref, wc_ref, wo_ref, o_ref,
                 wbr_scr, wo_scr, *, tm, rows_per_mod, row0):
    @pl.when(pl.program_id(0) == 0)
    def _():
        for i, w_ref in enumerate((wa_ref, wb_ref, wc_ref)):
            wbr_scr[i] = w_ref[...].astype(BF16)
        wo_scr[...] = wo_ref[...].astype(BF16)

    merged = None
    for i in range(3):
        y = jnp.dot(att_ref[:, i * 512:(i + 1) * 512], wbr_scr[i], preferred_element_type=F32)
        term = jax.nn.sigmoid(gate_ref[:, i * D_MODEL:(i + 1) * D_MODEL].astype(F32)) * y
        merged = term if merged is None else merged + term
    mixed = jnp.dot(merged.astype(BF16), wo_scr[...], preferred_element_type=F32)
    row = row0 + lax.div(pl.program_id(0) * tm, rows_per_mod)
    o_ref[...] = x_ref[...] + _mod_block(mod_ref, row, M_G1) * mixed


def _post(att, gates, x, mod, wa, wb, wc, wo, layer, rows_per_mod, row0):
    t = x.shape[0]
    tm = POST_TM
    full = lambda r: pl.BlockSpec((None, r, D_MODEL), lambda m: (layer, 0, 0),
                                  pipeline_mode=pl.Buffered(1))
    return pl.pallas_call(
        functools.partial(_post_kernel, tm=tm, rows_per_mod=rows_per_mod, row0=row0),
        grid=(t // tm,),
        in_specs=[
            pl.BlockSpec((tm, D_ATT), lambda m: (m, 0)),
            pl.BlockSpec((tm, D_GATE), lambda m: (m, 0)),
            pl.BlockSpec((tm, D_MODEL), lambda m: (m, 0)),
            _mod_spec(layer),
            full(512), full(512), full(512), full(D_MODEL),
        ],
        out_specs=pl.BlockSpec((tm, D_MODEL), lambda m: (m, 0)),
        out_shape=jax.ShapeDtypeStruct((t, D_MODEL), F32),
        scratch_shapes=[pltpu.VMEM((3, 512, D_MODEL), BF16), pltpu.VMEM((D_MODEL, D_MODEL), BF16)],
        compiler_params=_cparams(("arbitrary",)),
        name="post",
    )(att, gates, x, mod, wa, wb, wc, wo)


FFN_TM = 1024
FFN_TF = 256


FFN_GROUP = 2


def _ffn_kernel(x_ref, mod_ref, g_ref, fg_ref, wa_ref, wb_ref, wo_ref, o_ref,
                h_scr, acc_scr, wa_scr, wb_scr, wo_scr, *, tm, rows_per_mod, row0, final_norm):
    grp = pl.program_id(0)
    f = pl.program_id(1)
    m = pl.program_id(2)
    row = row0 + lax.div((grp * FFN_GROUP + m) * tm, rows_per_mod)

    @pl.when(f == 0)
    def _():
        _norm_mod_rows(x_ref, h_scr.at[m], g_ref[...], mod_ref, M_SC2, M_SH2, tm, row, rows_per_mod)
        acc_scr[m] = jnp.zeros((tm, D_MODEL), F32)

    @pl.when(m == 0)
    def _():
        wa_scr[...] = wa_ref[...].astype(BF16)
        wb_scr[...] = wb_ref[...].astype(BF16)
        wo_scr[...] = wo_ref[...].astype(BF16)

    h = h_scr[m]
    a = jnp.dot(h, wa_scr[...], preferred_element_type=F32)
    b = jnp.dot(h, wb_scr[...], preferred_element_type=F32)
    y = (a * jax.nn.sigmoid(a) * b).astype(BF16)
    acc_scr[m] += jnp.dot(y, wo_scr[...], preferred_element_type=F32)

    @pl.when(f == pl.num_programs(1) - 1)
    def _():
        y = x_ref[...] + _mod_block(mod_ref, row, M_G2) * acc_scr[m]
        if final_norm:
            y = y * lax.rsqrt(jnp.mean(y * y, axis=-1, keepdims=True) + EPS) * fg_ref[...]
        o_ref[...] = y


def _ffn(x, mod, g, final_g, w_in, w_out, layer, rows_per_mod, row0):
    t = x.shape[0]
    tm = FFN_TM
    assert rows_per_mod % tm == 0
    nf = D_FF // FFN_TF
    last = FFN_GROUP - 1
    x_idx = lambda grp, f, m: (grp * FFN_GROUP + jnp.where((f == 0) | (f == nf - 1), m, last), 0)
    o_idx = lambda grp, f, m: (grp * FFN_GROUP + jnp.where(f == nf - 1, m, 0), 0)
    return pl.pallas_call(
        functools.partial(_ffn_kernel, tm=tm, rows_per_mod=rows_per_mod, row0=row0,
                          final_norm=layer == DEPTH - 1),
        grid=(t // (tm * FFN_GROUP), nf, FFN_GROUP),
        in_specs=[
            pl.BlockSpec((tm, D_MODEL), x_idx),
            _mod_spec(layer),
            pl.BlockSpec((None, 1, D_MODEL), lambda grp, f, m: (layer, 0, 0)),
            pl.BlockSpec((1, D_MODEL), lambda grp, f, m: (0, 0)),
            pl.BlockSpec((None, D_MODEL, FFN_TF), lambda grp, f, m: (layer, 0, f)),
            pl.BlockSpec((None, D_MODEL, FFN_TF), lambda grp, f, m: (layer, 0, f + nf)),
            pl.BlockSpec((None, FFN_TF, D_MODEL), lambda grp, f, m: (layer, f, 0)),
        ],
        out_specs=pl.BlockSpec((tm, D_MODEL), o_idx),
        out_shape=jax.ShapeDtypeStruct((t, D_MODEL), F32),
        scratch_shapes=[pltpu.VMEM((FFN_GROUP, tm, D_MODEL), BF16),
                        pltpu.VMEM((FFN_GROUP, tm, D_MODEL), F32),
                        pltpu.VMEM((D_MODEL, FFN_TF), BF16),
                        pltpu.VMEM((D_MODEL, FFN_TF), BF16),
                        pltpu.VMEM((FFN_TF, D_MODEL), BF16)],
        compiler_params=_cparams(("arbitrary", "arbitrary", "arbitrary")),
        name="ffn",
    )(x, mod, g, final_g, w_in, w_in, w_out)


def _rope_tables():
    rows = DEC_SEQ // GRID_W
    row = jnp.repeat(jnp.arange(rows), GRID_W).astype(F32)
    col = jnp.tile(jnp.arange(GRID_W), rows).astype(F32)
    n = HEAD_DIM // 4
    inv = ROPE_THETA ** (-jnp.arange(n, dtype=F32) / n)
    ang = jnp.concatenate([row[:, None] * inv, col[:, None] * inv], axis=-1)
    cos, sin = jnp.cos(ang), jnp.sin(ang)
    cos_t = jnp.tile(cos, (1, 4))
    sin_t = jnp.tile(jnp.concatenate([-sin, sin], axis=-1), (1, 2))
    return cos_t, sin_t


def _pack_small(a_lam_q1, a_lam_k1, a_lam_q2, a_lam_k2, a_subln_g, b_qnorm_g, b_knorm_g):
    pad = lambda v: jnp.pad(v, ((0, 0), (0, LANES - HEAD_DIM)))
    rows = [pad(a_lam_q1), pad(a_lam_k1), pad(a_lam_q2), pad(a_lam_k2), a_subln_g,
            jnp.tile(b_qnorm_g, (1, 2)), jnp.tile(b_knorm_g, (1, 2)),
            jnp.zeros((DEPTH, LANES), F32)]
    return jnp.stack(rows, axis=1)


def kernel(x_prompt, x_sample, cache_a_k, cache_a_v, cache_b_k, cache_b_v, cache_c_k, cache_c_v, c, c_ctx, w_mod, b_mod, norm1_g, norm2_g, w_in, a_lam_q1, a_lam_k1, a_lam_q2, a_lam_k2, a_subln_g, b_qnorm_g, b_knorm_g, c_sink, w_br_a, w_br_b, w_br_c, w_out, w_ffn_in, w_ffn_out, final_g):
    t_ctx = BATCH * SEQ
    t_lat = DEC_BATCH * DEC_SEQ
    xp = x_prompt.reshape(t_ctx, D_MODEL)
    xs = x_sample.reshape(t_lat, D_MODEL)
    cv8 = jnp.concatenate([c_ctx[None, :], c, jnp.zeros((8 - 1 - DEC_BATCH, D_MODEL), F32)], axis=0)
    mod = _modulation(cv8, w_mod, b_mod)
    cos_t, sin_t = _rope_tables()
    sp = _pack_small(a_lam_q1, a_lam_k1, a_lam_q2, a_lam_k2, a_subln_g, b_qnorm_g, b_knorm_g)
    n1 = norm1_g.reshape(DEPTH, 1, D_MODEL)
    n2 = norm2_g.reshape(DEPTH, 1, D_MODEL)
    caches = (cache_a_k.reshape(DEC_BATCH, DEPTH, PAST_LEN * A_HEADS, LANES),
              cache_a_v.reshape(DEC_BATCH, DEPTH, PAST_LEN * A_HEADS, LANES),
              cache_b_k.reshape(DEC_BATCH, DEPTH, PAST_LEN, LANES),
              cache_b_v.reshape(DEC_BATCH, DEPTH, PAST_LEN, LANES),
              cache_c_k.reshape(DEC_BATCH, DEPTH, PAST_LEN, LANES),
              cache_c_v.reshape(DEC_BATCH, DEPTH, PAST_LEN, LANES))
    fg = final_g.reshape(1, D_MODEL)
    cache_out = None
    for l in range(DEPTH):
        lam_init = 0.8 - 0.6 * math.exp(-0.3 * l)

        qkv_c, gates_c = _proj(xp, mod, n1, w_in, l, t_ctx, 0)
        att_c, cache_out = _att_ctx(qkv_c, sp, c_sink, cache_out, l, lam_init)
        xp = _post(att_c, gates_c, xp, mod, w_br_a, w_br_b, w_br_c, w_out, l, t_ctx, 0)
        xp = _ffn(xp, mod, n2, fg, w_ffn_in, w_ffn_out, l, t_ctx, 0)

        qkv_s, gates_s = _proj(xs, mod, n1, w_in, l, DEC_SEQ, 1)
        att_s = _att_lat(qkv_s, caches, cos_t, sin_t, sp, c_sink, l, lam_init)
        xs = _post(att_s, gates_s, xs, mod, w_br_a, w_br_b, w_br_c, w_out, l, DEC_SEQ, 1)
        xs = _ffn(xs, mod, n2, fg, w_ffn_in, w_ffn_out, l, DEC_SEQ, 1)

    y_prompt = xp.reshape(BATCH, SEQ, D_MODEL)
    y_sample = xs.reshape(DEC_BATCH, DEC_SEQ, D_MODEL)
    ka_all, va_all, small_all = cache_out
    wide = tuple(a.reshape(BATCH, DEPTH, SEQ, A_HEADS, 2 * HEAD_DIM) for a in (ka_all, va_all))
    small = tuple(small_all[..., c:c + LANES].reshape(BATCH, DEPTH, SEQ, 2, HEAD_DIM)
                  for c in (SMALL_KB, SMALL_VB, SMALL_KC, SMALL_VC))
    return (y_prompt, y_sample) + wide + small
```

```python
import functools
import math

import jax
import jax.numpy as jnp
from jax import lax
from jax.experimental import pallas as pl
from jax.experimental.pallas import tpu as pltpu

D_MODEL = 1024
BATCH = 16
SEQ = 256
DEPTH = 4
DEC_BATCH = 4
DEC_SEQ = 1024
PAST_LEN = 512
GRID_W = 64
HEAD_DIM = 64
ROPE_THETA = 10000.0
EPS = 1e-6
NEG_INF = -1e30
A_HEADS = 4
WINDOW = 128
D_FF = -(-8 * D_MODEL // (3 * 256)) * 256
N_MOD = 6
D_QKV = 3072
D_GATE = 3072
D_ATT = 1536
LANES = 128
LOG2E = math.log2(math.e)
QSCALE = HEAD_DIM ** -0.5 * LOG2E

QA, KA, VA, QB, KB, VB, QC, KC, VC = 0, 512, 1024, 1536, 2048, 2176, 2304, 2816, 2944
R_LQ1, R_LK1, R_LQ2, R_LK2, R_SUBG, R_BQG, R_BKG = range(7)
M_SH1, M_SC1, M_G1, M_SH2, M_SC2, M_G2 = range(6)

F32 = jnp.float32
BF16 = jnp.bfloat16
VMEM_LIMIT = 56 * 1024 * 1024


def _cparams(sem):
    return pltpu.CompilerParams(dimension_semantics=sem, vmem_limit_bytes=VMEM_LIMIT)


MOD_TN = 1536


def _mod_kernel(cv_ref, w_ref, b_ref, o_ref):
    cv = cv_ref[...]
    s = (cv * jax.nn.sigmoid(cv)).astype(BF16)
    o_ref[...] = jnp.dot(s, w_ref[...].astype(BF16), preferred_element_type=F32) + b_ref[...]


def _modulation(cv8, w_mod, b_mod):
    n = N_MOD * D_MODEL
    return pl.pallas_call(
        _mod_kernel,
        grid=(DEPTH, n // MOD_TN),
        in_specs=[
            pl.BlockSpec((8, D_MODEL), lambda l, j: (0, 0)),
            pl.BlockSpec((None, D_MODEL, MOD_TN), lambda l, j: (l, 0, j)),
            pl.BlockSpec((None, 1, MOD_TN), lambda l, j: (l, 0, j)),
        ],
        out_specs=pl.BlockSpec((None, 8, MOD_TN), lambda l, j: (l, 0, j)),
        out_shape=jax.ShapeDtypeStruct((DEPTH, 8, n), F32),
        compiler_params=_cparams(("arbitrary", "arbitrary")),
        name="modulation",
    )(cv8, w_mod, b_mod.reshape(DEPTH, 1, n))


NORM_CHUNK = 256


def _mod_block(mod_ref, row, blk):
    return mod_ref[pl.ds(row, 1), blk * D_MODEL:(blk + 1) * D_MODEL]


def _norm_mod_rows(x_ref, h_ref, g, mod_ref, sc_blk, sh_blk, rows, mod_row0, rows_per_mod):
    def body(i, carry):
        r = pl.ds(pl.multiple_of(i * NORM_CHUNK, NORM_CHUNK), NORM_CHUNK)
        row = mod_row0 + lax.div(i * NORM_CHUNK, rows_per_mod)
        x = x_ref[r, :]
        y = x * lax.rsqrt(jnp.mean(x * x, axis=-1, keepdims=True) + EPS) * g
        h_ref[r, :] = (y * (1.0 + _mod_block(mod_ref, row, sc_blk))
                       + _mod_block(mod_ref, row, sh_blk)).astype(BF16)
        return carry
    lax.fori_loop(0, rows // NORM_CHUNK, body, 0)


def _head_rmsnorm(x, g, lane_lo):
    x2 = x * x
    zero = jnp.zeros_like(x2)
    lo = jnp.sum(jnp.where(lane_lo, x2, zero), axis=-1, keepdims=True)
    hi = jnp.sum(jnp.where(lane_lo, zero, x2), axis=-1, keepdims=True)
    ms = jnp.where(lane_lo, lo, hi) * (1.0 / HEAD_DIM)
    return x * lax.rsqrt(ms + EPS) * g


def _rope(x, cos, sin, lane):
    partner = jnp.where((lane & 32) == 0, pltpu.roll(x, LANES - 32, 1), pltpu.roll(x, 32, 1))
    return x * cos + partner * sin


def _lo_hi(x, lane, src_hi, ones_lane=False):
    lane_lo = lane < 64
    other = pltpu.roll(x, 64, 1)
    zero = jnp.zeros_like(x)
    lo = jnp.where(lane_lo, other if src_hi else x, zero)
    hi = jnp.where(lane_lo, zero, x if src_hi else other)
    if ones_lane:
        lo = jnp.where(lane == 64, 1.0, lo)
        hi = jnp.where(lane == 0, 1.0, hi)
    return lo, hi


def _split_heads(q, lane_lo):
    zero = jnp.zeros_like(q)
    return jnp.concatenate([jnp.where(lane_lo, q, zero), jnp.where(lane_lo, zero, q)], axis=0)


def _dot_nt(a, b):
    return lax.dot_general(a, b, (((1,), (1,)), ((), ())), preferred_element_type=F32)


def _probs(s, extra=None):
    m = jnp.max(s, axis=-1, keepdims=True)
    if extra is not None:
        m = jnp.maximum(m, extra)
    return jnp.exp2(s - m), m


def _lam(sp_ref, lam_init):
    dot1 = jnp.sum(sp_ref[R_LQ1:R_LQ1 + 1, :] * sp_ref[R_LK1:R_LK1 + 1, :], axis=-1, keepdims=True)
    dot2 = jnp.sum(sp_ref[R_LQ2:R_LQ2 + 1, :] * sp_ref[R_LK2:R_LK2 + 1, :], axis=-1, keepdims=True)
    return jnp.exp(dot1) - jnp.exp(dot2) + lam_init


def _subln(o, g, lam_init):
    return o * lax.rsqrt(jnp.mean(o * o, axis=-1, keepdims=True) + EPS) * g * (1.0 - lam_init)


def _mod_spec(layer):
    return pl.BlockSpec((None, 8, N_MOD * D_MODEL), lambda *_: (layer, 0, 0))


PROJ_TM = 2048
PROJ_TN = 512
N_QKV_TILES = D_QKV // PROJ_TN


def _proj_kernel(x_ref, mod_ref, g_ref, w_ref, qkv_ref, gate_ref, h_scr, w_scr, *, tm, rows_per_mod, row0):
    n = pl.program_id(0)
    m = pl.program_id(1)

    @pl.when(n == 0)
    def _():
        _norm_mod_rows(x_ref, h_scr.at[m], g_ref[...], mod_ref, M_SC1, M_SH1, tm,
                       row0 + lax.div(m * tm, rows_per_mod), rows_per_mod)

    @pl.when(m == 0)
    def _():
        w_scr[...] = w_ref[...].astype(BF16)

    @pl.when(n < N_QKV_TILES)
    def _():
        qkv_ref[...] = jnp.dot(h_scr[m], w_scr[...], preferred_element_type=F32)

    @pl.when(n >= N_QKV_TILES)
    def _():
        gate_ref[...] = jnp.dot(h_scr[m], w_scr[...], preferred_element_type=F32).astype(BF16)


def _proj(x, mod, g, w_in, layer, rows_per_mod, row0):
    t = x.shape[0]
    tm = PROJ_TM
    nm = t // tm
    return pl.pallas_call(
        functools.partial(_proj_kernel, tm=tm, rows_per_mod=rows_per_mod, row0=row0),
        grid=((D_QKV + D_GATE) // PROJ_TN, nm),
        in_specs=[
            pl.BlockSpec((tm, D_MODEL), lambda n, m: (jnp.where(n == 0, m, nm - 1), 0)),
            _mod_spec(layer),
            pl.BlockSpec((None, 1, D_MODEL), lambda n, m: (layer, 0, 0)),
            pl.BlockSpec((None, D_MODEL, PROJ_TN), lambda n, m: (layer, 0, n)),
        ],
        out_specs=[
            pl.BlockSpec((tm, PROJ_TN), lambda n, m: (jnp.where(n < N_QKV_TILES, m, nm - 1),
                                                      jnp.minimum(n, N_QKV_TILES - 1))),
            pl.BlockSpec((tm, PROJ_TN), lambda n, m: (jnp.where(n < N_QKV_TILES, 0, m),
                                                      jnp.maximum(n - N_QKV_TILES, 0))),
        ],
        out_shape=[jax.ShapeDtypeStruct((t, D_QKV), F32),
                   jax.ShapeDtypeStruct((t, D_GATE), BF16)],
        scratch_shapes=[pltpu.VMEM((nm, tm, D_MODEL), BF16), pltpu.VMEM((D_MODEL, PROJ_TN), BF16)],
        compiler_params=_cparams(("arbitrary", "arbitrary")),
        name="proj",
    )(x, mod, g, w_in)


CTX_NB = 1
SMALL_KB, SMALL_VB, SMALL_KC, SMALL_VC = (i * LANES for i in range(4))


def _att_ctx_kernel(qkv_ref, sp_ref, sink_ref, ka_in, va_in, small_in,
                    att_ref, ka_ref, va_ref, small_ref, *, layer, lam_init):
    del ka_in, va_in, small_in
    lane = lax.broadcasted_iota(jnp.int32, (SEQ, LANES), 1)
    lane_lo = lane < 64
    lam = _lam(sp_ref, lam_init)
    subg = sp_ref[R_SUBG:R_SUBG + 1, :]
    bqg = sp_ref[R_BQG:R_BQG + 1, :]
    bkg = sp_ref[R_BKG:R_BKG + 1, :]

    for i in range(CTX_NB):
        rows = slice(i * SEQ, (i + 1) * SEQ)
        tile = lambda c: qkv_ref[rows, c:c + LANES]

        for h in range(A_HEADS):
            q = _split_heads(tile(QA + h * LANES) * QSCALE, lane_lo).astype(BF16)
            k = tile(KA + h * LANES)
            v = tile(VA + h * LANES)
            ka_ref[i, pl.ds(h, SEQ, stride=A_HEADS), :] = k
            va_ref[i, pl.ds(h, SEQ, stride=A_HEADS), :] = v
            v = v.astype(BF16)
            s = _dot_nt(q, k.astype(BF16))
            p1, _ = _probs(s[:SEQ])
            p2, _ = _probs(s[SEQ:])
            o1 = jnp.dot(p1.astype(BF16), v, preferred_element_type=F32) / jnp.sum(p1, axis=-1, keepdims=True)
            o2 = jnp.dot(p2.astype(BF16), v, preferred_element_type=F32) / jnp.sum(p2, axis=-1, keepdims=True)
            o = _subln(o1 - lam * o2, subg, lam_init)
            att_ref[rows, h * LANES:(h + 1) * LANES] = o.astype(BF16)

        kb = _head_rmsnorm(tile(KB), bkg, lane_lo)
        vb, kc, vc = tile(VB), tile(KC), tile(VC)
        for x, c in ((kb, SMALL_KB), (vb, SMALL_VB), (kc, SMALL_KC), (vc, SMALL_VC)):
            small_ref[i, :, c:c + LANES] = x
        for mixer, (q0, k_t, v_t, o0) in enumerate(((QB, kb, vb, 512), (QC, kc, vc, 1024))):
            for g in range(2):
                k_lo, k_hi = _lo_hi(k_t, lane, g == 1)
                v_lo, v_hi = _lo_hi(v_t, lane, g == 1, ones_lane=True)
                k_both = (k_lo + k_hi).astype(BF16)
                v_lo = v_lo.astype(BF16)
                v_hi = v_hi.astype(BF16)
                for jj in range(2):
                    j = 2 * g + jj
                    q = tile(q0 + j * LANES)
                    if mixer == 0:
                        q = _head_rmsnorm(q, bqg, lane_lo)
                    s = _dot_nt(_split_heads(q * QSCALE, lane_lo).astype(BF16), k_both)
                    if mixer == 0:
                        pe, _ = _probs(s[:SEQ])
                        po, _ = _probs(s[SEQ:])
                    else:
                        sink_e = sink_ref[layer, 2 * j] * LOG2E
                        sink_o = sink_ref[layer, 2 * j + 1] * LOG2E
                        pe, me = _probs(s[:SEQ], sink_e)
                        po, mo = _probs(s[SEQ:], sink_o)
                    oe = jnp.dot(pe.astype(BF16), v_lo, preferred_element_type=F32)
                    oo = jnp.dot(po.astype(BF16), v_hi, preferred_element_type=F32)
                    le = oe[:, 64:65]
                    lo_ = oo[:, 0:1]
                    if mixer == 1:
                        le = le + jnp.exp2(sink_e - me)
                        lo_ = lo_ + jnp.exp2(sink_o - mo)
                    o = jnp.where(lane_lo, oe, oo) / jnp.where(lane_lo, le, lo_)
                    att_ref[rows, o0 + j * LANES:o0 + (j + 1) * LANES] = o.astype(BF16)


def _att_ctx(qkv, sp, sink, cache_out, layer, lam_init):
    t = qkv.shape[0]
    nb = t // SEQ
    rows = CTX_NB * SEQ
    wide = pl.BlockSpec((CTX_NB, None, SEQ * A_HEADS, LANES), lambda b: (b, layer, 0, 0))
    small = pl.BlockSpec((CTX_NB, None, SEQ, 4 * LANES), lambda b: (b, layer, 0, 0))
    wide_shape = jax.ShapeDtypeStruct((nb, DEPTH, SEQ * A_HEADS, LANES), F32)
    small_shape = jax.ShapeDtypeStruct((nb, DEPTH, SEQ, 4 * LANES), F32)
    if cache_out is None:
        cache_out = (jnp.zeros(wide_shape.shape, F32), jnp.zeros(wide_shape.shape, F32),
                     jnp.zeros(small_shape.shape, F32))
    passthrough = pl.BlockSpec(memory_space=pl.ANY)
    att, *cache_out = pl.pallas_call(
        functools.partial(_att_ctx_kernel, layer=layer, lam_init=lam_init),
        grid=(nb // CTX_NB,),
        in_specs=[
            pl.BlockSpec((rows, D_QKV), lambda b: (b, 0)),
            pl.BlockSpec((None, 8, LANES), lambda b: (layer, 0, 0)),
            pl.BlockSpec(memory_space=pltpu.SMEM),
            passthrough, passthrough, passthrough,
        ],
        out_specs=[pl.BlockSpec((rows, D_ATT), lambda b: (b, 0)), wide, wide, small],
        out_shape=[jax.ShapeDtypeStruct((t, D_ATT), BF16), wide_shape, wide_shape, small_shape],
        input_output_aliases={3: 1, 4: 2, 5: 3},
        compiler_params=_cparams(("arbitrary",)),
        name="att_ctx",
    )(qkv, sp, sink, *cache_out)
    return att, cache_out


LAT_TQ_NORM = 256
NK = PAST_LEN + DEC_SEQ
KCH = 512
PREP_ROWS = 512
N_UNITS = 12


def _attend_all_queries(q, k_chunks, v_chunks, masks=None, sink=None, l_lane=None):
    m = l = acc = None
    for c, (k_c, v_c) in enumerate(zip(k_chunks, v_chunks)):
        s = _dot_nt(q, k_c)
        if masks is not None and masks[c] is not None:
            s = jnp.where(masks[c], s, NEG_INF)
        mc = jnp.max(s, axis=-1, keepdims=True)
        if m is None:
            m_new = mc if sink is None else jnp.maximum(mc, sink)
            p = jnp.exp2(s - m_new)
            acc = jnp.dot(p.astype(BF16), v_c, preferred_element_type=F32)
            if l_lane is None:
                l = jnp.sum(p, axis=-1, keepdims=True)
        else:
            m_new = jnp.maximum(m, mc)
            alpha = jnp.exp2(m - m_new)
            p = jnp.exp2(s - m_new)
            acc = alpha * acc + jnp.dot(p.astype(BF16), v_c, preferred_element_type=F32)
            if l_lane is None:
                l = alpha * l + jnp.sum(p, axis=-1, keepdims=True)
        m = m_new
    if l_lane is not None:
        l = acc[:, l_lane:l_lane + 1]
    if sink is not None:
        l = l + jnp.exp2(sink - m)
    return acc, l


def _att_lat_kernel(q_ref, kva_ref, kvv_ref, kv4_ref, kv5_ref,
                    cak_ref, cav_ref, cbk_ref, cbv_ref, cck_ref, ccv_ref,
                    cos_ref, sin_ref, sp_ref, sink_ref,
                    att_ref,
                    ka_scr, va_scr, kb_scr, vb_scr, kc_scr, vc_scr, q_scr,
                    *, layer, lam_init):
    u = pl.program_id(1)
    subg = sp_ref[R_SUBG:R_SUBG + 1, :]
    bqg = sp_ref[R_BQG:R_BQG + 1, :]
    bkg = sp_ref[R_BKG:R_BKG + 1, :]
    @pl.when(u == 0)
    def _prepare_keys():
        lane = lax.broadcasted_iota(jnp.int32, (PREP_ROWS, LANES), 1)
        lane_lo = lane < 64
        def put(dst, a_k, a_v, pairs):
            lo_rows = pl.ds(pl.multiple_of(dst, PREP_ROWS), PREP_ROWS)
            hi_rows = pl.ds(pl.multiple_of(NK + dst, PREP_ROWS), PREP_ROWS)
            for h in range(A_HEADS):
                k = a_k[h]
                zero = jnp.zeros_like(k)
                ka_scr[h, lo_rows, :] = jnp.where(lane_lo, k, zero).astype(BF16)
                ka_scr[h, hi_rows, :] = jnp.where(lane_lo, zero, k).astype(BF16)
                va_scr[h, lo_rows, :] = a_v[h].astype(BF16)
            for x, scr, is_value in pairs:
                for g in range(2):
                    lo, hi = _lo_hi(x, lane, g == 1, ones_lane=is_value)
                    scr[g, lo_rows, :] = lo.astype(BF16)
                    scr[g, hi_rows, :] = hi.astype(BF16)

        def cached(i, carry):
            r = pl.ds(pl.multiple_of(i * PREP_ROWS, PREP_ROWS), PREP_ROWS)
            head_rows = lambda h: pl.ds(i * (PREP_ROWS * A_HEADS) + h, PREP_ROWS, stride=A_HEADS)
            put(i * PREP_ROWS,
                [cak_ref[head_rows(h), :] for h in range(A_HEADS)],
                [cav_ref[head_rows(h), :] for h in range(A_HEADS)],
                ((cbk_ref[r, :], kb_scr, False), (cbv_ref[r, :], vb_scr, True),
                 (cck_ref[r, :], kc_scr, False), (ccv_ref[r, :], vc_scr, True)))
            return carry
        lax.fori_loop(0, PAST_LEN // PREP_ROWS, cached, 0)

        def latent(i, carry):
            r = pl.ds(pl.multiple_of(i * PREP_ROWS, PREP_ROWS), PREP_ROWS)
            cos = cos_ref[r, :]
            sin = sin_ref[r, :]
            kb = _rope(_head_rmsnorm(kv4_ref[r, 0:LANES], bkg, lane_lo), cos, sin, lane)
            kc = _rope(kv5_ref[r, 256:256 + LANES], cos, sin, lane)
            put(PAST_LEN + i * PREP_ROWS,
                [_rope(kva_ref[r, h * LANES:(h + 1) * LANES], cos, sin, lane) for h in range(A_HEADS)],
                [kvv_ref[r, h * LANES:(h + 1) * LANES] for h in range(A_HEADS)],
                ((kb, kb_scr, False), (kv4_ref[r, LANES:2 * LANES], vb_scr, True),
                 (kc, kc_scr, False), (kv5_ref[r, 384:384 + LANES], vc_scr, True)))
            return carry
        lax.fori_loop(0, DEC_SEQ // PREP_ROWS, latent, 0)

    lam = _lam(sp_ref, lam_init)
    lane_lo_all = lax.broadcasted_iota(jnp.int32, (DEC_SEQ, LANES), 1) < 64
    chunks = lambda scr, i, base: [scr[i, base + c * KCH:base + (c + 1) * KCH, :] for c in range(NK // KCH)]

    def prepare_queries(normalise):
        tq = LAT_TQ_NORM if normalise else DEC_SEQ
        lane = lax.broadcasted_iota(jnp.int32, (tq, LANES), 1)
        lane_lo = lane < 64

        def step(i, carry):
            r = pl.ds(pl.multiple_of(i * tq, tq), tq)
            q = q_ref[r, :]
            if normalise:
                q = _head_rmsnorm(q, bqg, lane_lo)
            q_scr[r, :] = (_rope(q, cos_ref[r, :], sin_ref[r, :], lane) * QSCALE).astype(BF16)
            return carry
        lax.fori_loop(0, DEC_SEQ // tq, step, 0)

    @pl.when(u < A_HEADS)
    def _mixer_a():
        prepare_queries(False)
        q = q_scr[...]
        v = chunks(va_scr, u, 0)
        a1, l1 = _attend_all_queries(q, chunks(ka_scr, u, 0), v)
        a2, l2 = _attend_all_queries(q, chunks(ka_scr, u, NK), v)
        att_ref[...] = _subln(a1 / l1 - lam * (a2 / l2), subg, lam_init).astype(BF16)

    def pair_unit(k_scr, v_scr, g, masks=None, sinks=(None, None)):
        q = q_scr[...]
        ae, le = _attend_all_queries(q, chunks(k_scr, g, 0), chunks(v_scr, g, 0),
                                     masks=masks, sink=sinks[0], l_lane=64)
        ao, lo_ = _attend_all_queries(q, chunks(k_scr, g, NK), chunks(v_scr, g, NK),
                                      masks=masks, sink=sinks[1], l_lane=0)
        o = jnp.where(lane_lo_all, ae, ao) / jnp.where(lane_lo_all, le, lo_)
        att_ref[...] = o.astype(BF16)

    @pl.when((u >= A_HEADS) & (u < A_HEADS + 4))
    def _mixer_b():
        prepare_queries(True)
        pair_unit(kb_scr, vb_scr, lax.shift_right_logical(u - A_HEADS, 1))

    @pl.when(u >= A_HEADS + 4)
    def _mixer_c():
        j = u - (A_HEADS + 4)
        prepare_queries(False)
        qpos = lax.broadcasted_iota(jnp.int32, (DEC_SEQ, KCH), 0)
        kcol = lax.broadcasted_iota(jnp.int32, (DEC_SEQ, KCH), 1)
        masks = (None,) * (PAST_LEN // KCH) + tuple(
            jnp.abs(kcol + k0 - qpos) <= WINDOW for k0 in range(0, DEC_SEQ, KCH))
        pair_unit(kc_scr, vc_scr, lax.shift_right_logical(j, 1), masks=masks,
                  sinks=(sink_ref[layer, 2 * j] * LOG2E, sink_ref[layer, 2 * j + 1] * LOG2E))


def _unit_q_col(u):
    return jnp.where(u < A_HEADS, u, jnp.where(u < A_HEADS + 4, QB // LANES - A_HEADS + u,
                                               QC // LANES - A_HEADS - 4 + u))


def _att_lat(qkv, caches, cos, sin, sp, sink, layer, lam_init):
    t = qkv.shape[0]
    once = pl.Buffered(1)
    kv_blk = lambda col: pl.BlockSpec((DEC_SEQ, 512), lambda b, i: (b, col), pipeline_mode=once)
    cache_blk = lambda r: pl.BlockSpec((None, None, r, LANES), lambda b, i: (b, layer, 0, 0),
                                       pipeline_mode=once)
    return pl.pallas_call(
        functools.partial(_att_lat_kernel, layer=layer, lam_init=lam_init),
        grid=(DEC_BATCH, N_UNITS),
        in_specs=[
            pl.BlockSpec((DEC_SEQ, LANES), lambda b, i: (b, _unit_q_col(i))),
            kv_blk(KA // 512), kv_blk(VA // 512), kv_blk(4), kv_blk(5),
            cache_blk(PAST_LEN * A_HEADS), cache_blk(PAST_LEN * A_HEADS),
            cache_blk(PAST_LEN), cache_blk(PAST_LEN), cache_blk(PAST_LEN), cache_blk(PAST_LEN),
            pl.BlockSpec((DEC_SEQ, LANES), lambda b, i: (0, 0)),
            pl.BlockSpec((DEC_SEQ, LANES), lambda b, i: (0, 0)),
            pl.BlockSpec((None, 8, LANES), lambda b, i: (layer, 0, 0)),
            pl.BlockSpec(memory_space=pltpu.SMEM),
        ],
        out_specs=pl.BlockSpec((DEC_SEQ, LANES), lambda b, i: (b, i)),
        out_shape=jax.ShapeDtypeStruct((t, D_ATT), BF16),
        scratch_shapes=[
            pltpu.VMEM((A_HEADS, 2 * NK, LANES), BF16),
            pltpu.VMEM((A_HEADS, NK, LANES), BF16),
            pltpu.VMEM((2, 2 * NK, LANES), BF16),
            pltpu.VMEM((2, 2 * NK, LANES), BF16),
            pltpu.VMEM((2, 2 * NK, LANES), BF16),
            pltpu.VMEM((2, 2 * NK, LANES), BF16),
            pltpu.VMEM((DEC_SEQ, LANES), BF16),
        ],
        compiler_params=_cparams(("arbitrary", "arbitrary")),
        name="att_lat",
    )(qkv, qkv, qkv, qkv, qkv, *caches, cos, sin, sp, sink)


POST_TM = 512


def _post_kernel(att_ref, gate_ref, x_ref, mod_ref, g2_ref, wa_ref, wb_ref, wc_ref, wo_ref,
                 o_ref, h_ref, wbr_scr, wo_scr, *, tm, rows_per_mod, row0):
    @pl.when(pl.program_id(0) == 0)
    def _():
        for i, w_ref in enumerate((wa_ref, wb_ref, wc_ref)):
            wbr_scr[i] = w_ref[...].astype(BF16)
        wo_scr[...] = wo_ref[...].astype(BF16)

    merged = None
    for i in range(3):
        y = jnp.dot(att_ref[:, i * 512:(i + 1) * 512], wbr_scr[i], preferred_element_type=F32)
        term = jax.nn.sigmoid(gate_ref[:, i * D_MODEL:(i + 1) * D_MODEL].astype(F32)) * y
        merged = term if merged is None else merged + term
    mixed = jnp.dot(merged.astype(BF16), wo_scr[...], preferred_element_type=F32)
    row = row0 + lax.div(pl.program_id(0) * tm, rows_per_mod)
    x1 = x_ref[...] + _mod_block(mod_ref, row, M_G1) * mixed
    o_ref[...] = x1
    y = x1 * lax.rsqrt(jnp.mean(x1 * x1, axis=-1, keepdims=True) + EPS) * g2_ref[...]
    h_ref[...] = (y * (1.0 + _mod_block(mod_ref, row, M_SC2)) + _mod_block(mod_ref, row, M_SH2)).astype(BF16)


def _post(att, gates, x, mod, g2, wa, wb, wc, wo, layer, rows_per_mod, row0):
    t = x.shape[0]
    tm = POST_TM
    full = lambda r: pl.BlockSpec((None, r, D_MODEL), lambda m: (layer, 0, 0),
                                  pipeline_mode=pl.Buffered(1))
    return pl.pallas_call(
        functools.partial(_post_kernel, tm=tm, rows_per_mod=rows_per_mod, row0=row0),
        grid=(t // tm,),
        in_specs=[
            pl.BlockSpec((tm, D_ATT), lambda m: (m, 0)),
            pl.BlockSpec((tm, D_GATE), lambda m: (m, 0)),
            pl.BlockSpec((tm, D_MODEL), lambda m: (m, 0)),
            _mod_spec(layer),
            pl.BlockSpec((None, 1, D_MODEL), lambda m: (layer, 0, 0)),
            full(512), full(512), full(512), full(D_MODEL),
        ],
        out_specs=[pl.BlockSpec((tm, D_MODEL), lambda m: (m, 0)),
                   pl.BlockSpec((tm, D_MODEL), lambda m: (m, 0))],
        out_shape=[jax.ShapeDtypeStruct((t, D_MODEL), F32),
                   jax.ShapeDtypeStruct((t, D_MODEL), BF16)],
        scratch_shapes=[pltpu.VMEM((3, 512, D_MODEL), BF16), pltpu.VMEM((D_MODEL, D_MODEL), BF16)],
        compiler_params=_cparams(("arbitrary",)),
        name="post",
    )(att, gates, x, mod, g2, wa, wb, wc, wo)


FFN_TM = 1024
FFN_TF = 256


FFN_GROUP = 2


def _ffn_kernel(x_ref, h_ref, mod_ref, fg_ref, wa_ref, wb_ref, wo_ref, o_ref,
                acc_scr, wab_scr, wo_scr, *, tm, rows_per_mod, row0, final_norm):
    grp = pl.program_id(0)
    f = pl.program_id(1)
    m = pl.program_id(2)

    @pl.when(f == 0)
    def _():
        acc_scr[m] = jnp.zeros((tm, D_MODEL), F32)

    @pl.when(m == 0)
    def _():
        wab_scr[:, 0:FFN_TF] = wa_ref[...].astype(BF16)
        wab_scr[:, FFN_TF:2 * FFN_TF] = wb_ref[...].astype(BF16)
        wo_scr[...] = wo_ref[...].astype(BF16)

    ab = jnp.dot(h_ref[pl.ds(pl.multiple_of(m * tm, tm), tm), :], wab_scr[...],
                 preferred_element_type=F32)
    a = ab[:, 0:FFN_TF]
    y = (a * jax.nn.sigmoid(a) * ab[:, FFN_TF:2 * FFN_TF]).astype(BF16)
    acc_scr[m] += jnp.dot(y, wo_scr[...], preferred_element_type=F32)

    @pl.when(f == pl.num_programs(1) - 1)
    def _():
        row = row0 + lax.div((grp * FFN_GROUP + m) * tm, rows_per_mod)
        y = x_ref[...] + _mod_block(mod_ref, row, M_G2) * acc_scr[m]
        if final_norm:
            y = y * lax.rsqrt(jnp.mean(y * y, axis=-1, keepdims=True) + EPS) * fg_ref[...]
        o_ref[...] = y


def _ffn(x, h, mod, final_g, w_in, w_out, layer, rows_per_mod, row0):
    t = x.shape[0]
    tm = FFN_TM
    assert rows_per_mod % tm == 0
    nf = D_FF // FFN_TF
    io_idx = lambda grp, f, m: (grp * FFN_GROUP + jnp.where(f == nf - 1, m, 0), 0)
    return pl.pallas_call(
        functools.partial(_ffn_kernel, tm=tm, rows_per_mod=rows_per_mod, row0=row0,
                          final_norm=layer == DEPTH - 1),
        grid=(t // (tm * FFN_GROUP), nf, FFN_GROUP),
        in_specs=[
            pl.BlockSpec((tm, D_MODEL), io_idx),
            pl.BlockSpec((FFN_GROUP * tm, D_MODEL), lambda grp, f, m: (grp, 0)),
            _mod_spec(layer),
            pl.BlockSpec((1, D_MODEL), lambda grp, f, m: (0, 0)),
            pl.BlockSpec((None, D_MODEL, FFN_TF), lambda grp, f, m: (layer, 0, f)),
            pl.BlockSpec((None, D_MODEL, FFN_TF), lambda grp, f, m: (layer, 0, f + nf)),
            pl.BlockSpec((None, FFN_TF, D_MODEL), lambda grp, f, m: (layer, f, 0)),
        ],
        out_specs=pl.BlockSpec((tm, D_MODEL), io_idx),
        out_shape=jax.ShapeDtypeStruct((t, D_MODEL), F32),
        scratch_shapes=[pltpu.VMEM((FFN_GROUP, tm, D_MODEL), F32),
                        pltpu.VMEM((D_MODEL, 2 * FFN_TF), BF16),
                        pltpu.VMEM((FFN_TF, D_MODEL), BF16)],
        compiler_params=_cparams(("arbitrary", "arbitrary", "arbitrary")),
        name="ffn",
    )(x, h, mod, final_g, w_in, w_in, w_out)


def _rope_tables():
    rows = DEC_SEQ // GRID_W
    row = jnp.repeat(jnp.arange(rows), GRID_W).astype(F32)
    col = jnp.tile(jnp.arange(GRID_W), rows).astype(F32)
    n = HEAD_DIM // 4
    inv = ROPE_THETA ** (-jnp.arange(n, dtype=F32) / n)
    ang = jnp.concatenate([row[:, None] * inv, col[:, None] * inv], axis=-1)
    cos, sin = jnp.cos(ang), jnp.sin(ang)
    cos_t = jnp.tile(cos, (1, 4))
    sin_t = jnp.tile(jnp.concatenate([-sin, sin], axis=-1), (1, 2))
    return cos_t, sin_t


def _pack_small(a_lam_q1, a_lam_k1, a_lam_q2, a_lam_k2, a_subln_g, b_qnorm_g, b_knorm_g):
    pad = lambda v: jnp.pad(v, ((0, 0), (0, LANES - HEAD_DIM)))
    rows = [pad(a_lam_q1), pad(a_lam_k1), pad(a_lam_q2), pad(a_lam_k2), a_subln_g,
            jnp.tile(b_qnorm_g, (1, 2)), jnp.tile(b_knorm_g, (1, 2)),
            jnp.zeros((DEPTH, LANES), F32)]
    return jnp.stack(rows, axis=1)


def kernel(x_prompt, x_sample, cache_a_k, cache_a_v, cache_b_k, cache_b_v, cache_c_k, cache_c_v, c, c_ctx, w_mod, b_mod, norm1_g, norm2_g, w_in, a_lam_q1, a_lam_k1, a_lam_q2, a_lam_k2, a_subln_g, b_qnorm_g, b_knorm_g, c_sink, w_br_a, w_br_b, w_br_c, w_out, w_ffn_in, w_ffn_out, final_g):
    t_ctx = BATCH * SEQ
    t_lat = DEC_BATCH * DEC_SEQ
    xp = x_prompt.reshape(t_ctx, D_MODEL)
    xs = x_sample.reshape(t_lat, D_MODEL)
    cv8 = jnp.concatenate([c_ctx[None, :], c, jnp.zeros((8 - 1 - DEC_BATCH, D_MODEL), F32)], axis=0)
    mod = _modulation(cv8, w_mod, b_mod)
    cos_t, sin_t = _rope_tables()
    sp = _pack_small(a_lam_q1, a_lam_k1, a_lam_q2, a_lam_k2, a_subln_g, b_qnorm_g, b_knorm_g)
    n1 = norm1_g.reshape(DEPTH, 1, D_MODEL)
    n2 = norm2_g.reshape(DEPTH, 1, D_MODEL)
    caches = (cache_a_k.reshape(DEC_BATCH, DEPTH, PAST_LEN * A_HEADS, LANES),
              cache_a_v.reshape(DEC_BATCH, DEPTH, PAST_LEN * A_HEADS, LANES),
              cache_b_k.reshape(DEC_BATCH, DEPTH, PAST_LEN, LANES),
              cache_b_v.reshape(DEC_BATCH, DEPTH, PAST_LEN, LANES),
              cache_c_k.reshape(DEC_BATCH, DEPTH, PAST_LEN, LANES),
              cache_c_v.reshape(DEC_BATCH, DEPTH, PAST_LEN, LANES))
    fg = final_g.reshape(1, D_MODEL)
    cache_out = None
    for l in range(DEPTH):
        lam_init = 0.8 - 0.6 * math.exp(-0.3 * l)

        qkv_c, gates_c = _proj(xp, mod, n1, w_in, l, t_ctx, 0)
        att_c, cache_out = _att_ctx(qkv_c, sp, c_sink, cache_out, l, lam_init)
        xp, hp = _post(att_c, gates_c, xp, mod, n2, w_br_a, w_br_b, w_br_c, w_out, l, t_ctx, 0)
        xp = _ffn(xp, hp, mod, fg, w_ffn_in, w_ffn_out, l, t_ctx, 0)

        qkv_s, gates_s = _proj(xs, mod, n1, w_in, l, DEC_SEQ, 1)
        att_s = _att_lat(qkv_s, caches, cos_t, sin_t, sp, c_sink, l, lam_init)
        xs, hs = _post(att_s, gates_s, xs, mod, n2, w_br_a, w_br_b, w_br_c, w_out, l, DEC_SEQ, 1)
        xs = _ffn(xs, hs, mod, fg, w_ffn_in, w_ffn_out, l, DEC_SEQ, 1)

    y_prompt = xp.reshape(BATCH, SEQ, D_MODEL)
    y_sample = xs.reshape(DEC_BATCH, DEC_SEQ, D_MODEL)
    ka_all, va_all, small_all = cache_out
    wide = tuple(a.reshape(BATCH, DEPTH, SEQ, A_HEADS, 2 * HEAD_DIM) for a in (ka_all, va_all))
    small = tuple(small_all[..., c:c + LANES].reshape(BATCH, DEPTH, SEQ, 2, HEAD_DIM)
                  for c in (SMALL_KB, SMALL_VB, SMALL_KC, SMALL_VC))
    return (y_prompt, y_sample) + wide + small
```

```python
import functools
import math

import jax
import jax.numpy as jnp
from jax import lax
from jax.experimental import pallas as pl
from jax.experimental.pallas import tpu as pltpu

D_MODEL = 1024
BATCH = 16
SEQ = 256
DEPTH = 4
DEC_BATCH = 4
DEC_SEQ = 1024
PAST_LEN = 512
GRID_W = 64
HEAD_DIM = 64
ROPE_THETA = 10000.0
EPS = 1e-6
NEG_INF = -1e30
A_HEADS = 4
WINDOW = 128
D_FF = -(-8 * D_MODEL // (3 * 256)) * 256
N_MOD = 6
N_BRANCH = 3
BRANCH = 512
D_ATT = N_BRANCH * BRANCH
D_GATE = N_BRANCH * D_MODEL
LANES = 128
HALF = HEAD_DIM // 2
LOG2E = math.log2(math.e)
QSCALE = HEAD_DIM ** -0.5 * LOG2E

_KV = 2 * HEAD_DIM
QA, KA, VA = 0, BRANCH, 2 * BRANCH
QB, KB, VB = 3 * BRANCH, 4 * BRANCH, 4 * BRANCH + _KV
QC, KC, VC = VB + _KV, VB + _KV + BRANCH, VB + 2 * _KV + BRANCH
D_QKV = VC + _KV
R_LQ1, R_LK1, R_LQ2, R_LK2, R_SUBG, R_BQG, R_BKG = range(7)
M_SH1, M_SC1, M_G1, M_SH2, M_SC2, M_G2 = range(6)

F32 = jnp.float32
BF16 = jnp.bfloat16
VMEM_LIMIT = 56 * 1024 * 1024


def _cparams(sem):
    return pltpu.CompilerParams(dimension_semantics=sem, vmem_limit_bytes=VMEM_LIMIT)


MOD_TN = 1536


def _mod_kernel(cv_ref, w_ref, b_ref, o_ref):
    cv = cv_ref[...]
    s = (cv * jax.nn.sigmoid(cv)).astype(BF16)
    o_ref[...] = jnp.dot(s, w_ref[...].astype(BF16), preferred_element_type=F32) + b_ref[...]


def _modulation(cv8, w_mod, b_mod):
    n = N_MOD * D_MODEL
    return pl.pallas_call(
        _mod_kernel,
        grid=(DEPTH, n // MOD_TN),
        in_specs=[
            pl.BlockSpec((8, D_MODEL), lambda l, j: (0, 0)),
            pl.BlockSpec((None, D_MODEL, MOD_TN), lambda l, j: (l, 0, j)),
            pl.BlockSpec((None, 1, MOD_TN), lambda l, j: (l, 0, j)),
        ],
        out_specs=pl.BlockSpec((None, 8, MOD_TN), lambda l, j: (l, 0, j)),
        out_shape=jax.ShapeDtypeStruct((DEPTH, 8, n), F32),
        compiler_params=_cparams(("arbitrary", "arbitrary")),
        name="modulation",
    )(cv8, w_mod, b_mod.reshape(DEPTH, 1, n))


NORM_CHUNK = 256


def _mod_block(mod_ref, row, blk):
    return mod_ref[pl.ds(row, 1), blk * D_MODEL:(blk + 1) * D_MODEL]


def _norm_mod_rows(x_ref, h_ref, g, mod_ref, sc_blk, sh_blk, rows, mod_row0, rows_per_mod):
    def body(i, carry):
        r = pl.ds(pl.multiple_of(i * NORM_CHUNK, NORM_CHUNK), NORM_CHUNK)
        row = mod_row0 + lax.div(i * NORM_CHUNK, rows_per_mod)
        x = x_ref[r, :]
        y = x * lax.rsqrt(jnp.mean(x * x, axis=-1, keepdims=True) + EPS) * g
        h_ref[r, :] = (y * (1.0 + _mod_block(mod_ref, row, sc_blk))
                       + _mod_block(mod_ref, row, sh_blk)).astype(BF16)
        return carry
    lax.fori_loop(0, rows // NORM_CHUNK, body, 0)


def _head_rmsnorm(x, g, lane_lo):
    x2 = x * x
    zero = jnp.zeros_like(x2)
    lo = jnp.sum(jnp.where(lane_lo, x2, zero), axis=-1, keepdims=True)
    hi = jnp.sum(jnp.where(lane_lo, zero, x2), axis=-1, keepdims=True)
    ms = jnp.where(lane_lo, lo, hi) * (1.0 / HEAD_DIM)
    return x * lax.rsqrt(ms + EPS) * g


def _rope(x, cos, sin, lane):
    partner = jnp.where((lane & HALF) == 0, pltpu.roll(x, LANES - HALF, 1), pltpu.roll(x, HALF, 1))
    return x * cos + partner * sin


ONES_LANE_LO, ONES_LANE_HI = HEAD_DIM, 0


def _lo_hi(x, lane, src_hi, ones_lane=False):
    lane_lo = lane < HEAD_DIM
    other = pltpu.roll(x, HEAD_DIM, 1)
    zero = jnp.zeros_like(x)
    lo = jnp.where(lane_lo, other if src_hi else x, zero)
    hi = jnp.where(lane_lo, zero, x if src_hi else other)
    if ones_lane:
        lo = jnp.where(lane == ONES_LANE_LO, 1.0, lo)
        hi = jnp.where(lane == ONES_LANE_HI, 1.0, hi)
    return lo, hi


def _split_heads(q, lane_lo):
    zero = jnp.zeros_like(q)
    return jnp.concatenate([jnp.where(lane_lo, q, zero), jnp.where(lane_lo, zero, q)], axis=0)


def _dot_nt(a, b):
    return lax.dot_general(a, b, (((1,), (1,)), ((), ())), preferred_element_type=F32)


def _probs(s, extra=None):
    m = jnp.max(s, axis=-1, keepdims=True)
    if extra is not None:
        m = jnp.maximum(m, extra)
    return jnp.exp2(s - m), m


def _lam(sp_ref, lam_init):
    dot1 = jnp.sum(sp_ref[R_LQ1:R_LQ1 + 1, :] * sp_ref[R_LK1:R_LK1 + 1, :], axis=-1, keepdims=True)
    dot2 = jnp.sum(sp_ref[R_LQ2:R_LQ2 + 1, :] * sp_ref[R_LK2:R_LK2 + 1, :], axis=-1, keepdims=True)
    return jnp.exp(dot1) - jnp.exp(dot2) + lam_init


def _subln(o, g, lam_init):
    return o * lax.rsqrt(jnp.mean(o * o, axis=-1, keepdims=True) + EPS) * g * (1.0 - lam_init)


def _mod_spec(layer):
    return pl.BlockSpec((None, 8, N_MOD * D_MODEL), lambda *_: (layer, 0, 0))


PROJ_TM = 2048
PROJ_TN = 512


def _proj_kernel(x_ref, mod_ref, g_ref, w_ref, qkv_ref, hout_ref, h_scr, w_scr, *, tm, rows_per_mod, row0):
    n = pl.program_id(0)
    m = pl.program_id(1)

    @pl.when(n == 0)
    def _():
        _norm_mod_rows(x_ref, h_scr.at[m], g_ref[...], mod_ref, M_SC1, M_SH1, tm,
                       row0 + lax.div(m * tm, rows_per_mod), rows_per_mod)
        hout_ref[...] = h_scr[m]

    @pl.when(m == 0)
    def _():
        w_scr[...] = w_ref[...].astype(BF16)

    qkv_ref[...] = jnp.dot(h_scr[m], w_scr[...], preferred_element_type=F32)


def _proj(x, mod, g, w_in, layer, rows_per_mod, row0):
    t = x.shape[0]
    tm = PROJ_TM
    nm = t // tm
    first_pass = lambda n, m: (jnp.where(n == 0, m, nm - 1), 0)
    return pl.pallas_call(
        functools.partial(_proj_kernel, tm=tm, rows_per_mod=rows_per_mod, row0=row0),
        grid=(D_QKV // PROJ_TN, nm),
        in_specs=[
            pl.BlockSpec((tm, D_MODEL), first_pass),
            _mod_spec(layer),
            pl.BlockSpec((None, 1, D_MODEL), lambda n, m: (layer, 0, 0)),
            pl.BlockSpec((None, D_MODEL, PROJ_TN), lambda n, m: (layer, 0, n)),
        ],
        out_specs=[pl.BlockSpec((tm, PROJ_TN), lambda n, m: (m, n)),
                   pl.BlockSpec((tm, D_MODEL), first_pass)],
        out_shape=[jax.ShapeDtypeStruct((t, D_QKV), F32),
                   jax.ShapeDtypeStruct((t, D_MODEL), BF16)],
        scratch_shapes=[pltpu.VMEM((nm, tm, D_MODEL), BF16), pltpu.VMEM((D_MODEL, PROJ_TN), BF16)],
        compiler_params=_cparams(("arbitrary", "arbitrary")),
        name="proj",
    )(x, mod, g, w_in)


CTX_NB = 1
SMALL_KB, SMALL_VB, SMALL_KC, SMALL_VC = (i * LANES for i in range(4))


def _att_ctx_kernel(qkv_ref, sp_ref, sink_ref, ka_in, va_in, small_in,
                    att_ref, ka_ref, va_ref, small_ref, *, layer, lam_init):
    del ka_in, va_in, small_in
    lane = lax.broadcasted_iota(jnp.int32, (SEQ, LANES), 1)
    lane_lo = lane < HEAD_DIM
    lam = _lam(sp_ref, lam_init)
    subg = sp_ref[R_SUBG:R_SUBG + 1, :]
    bqg = sp_ref[R_BQG:R_BQG + 1, :]
    bkg = sp_ref[R_BKG:R_BKG + 1, :]

    for i in range(CTX_NB):
        rows = slice(i * SEQ, (i + 1) * SEQ)
        tile = lambda c: qkv_ref[rows, c:c + LANES]

        for h in range(A_HEADS):
            q = _split_heads(tile(QA + h * LANES) * QSCALE, lane_lo).astype(BF16)
            k = tile(KA + h * LANES)
            v = tile(VA + h * LANES)
            ka_ref[i, pl.ds(h, SEQ, stride=A_HEADS), :] = k
            va_ref[i, pl.ds(h, SEQ, stride=A_HEADS), :] = v
            v = v.astype(BF16)
            s = _dot_nt(q, k.astype(BF16))
            p1, _ = _probs(s[:SEQ])
            p2, _ = _probs(s[SEQ:])
            o1 = jnp.dot(p1.astype(BF16), v, preferred_element_type=F32) / jnp.sum(p1, axis=-1, keepdims=True)
            o2 = jnp.dot(p2.astype(BF16), v, preferred_element_type=F32) / jnp.sum(p2, axis=-1, keepdims=True)
            o = _subln(o1 - lam * o2, subg, lam_init)
            att_ref[rows, h * LANES:(h + 1) * LANES] = o.astype(BF16)

        kb = _head_rmsnorm(tile(KB), bkg, lane_lo)
        vb, kc, vc = tile(VB), tile(KC), tile(VC)
        for x, c in ((kb, SMALL_KB), (vb, SMALL_VB), (kc, SMALL_KC), (vc, SMALL_VC)):
            small_ref[i, :, c:c + LANES] = x
        for mixer, (q0, k_t, v_t, o0) in enumerate(((QB, kb, vb, BRANCH), (QC, kc, vc, 2 * BRANCH))):
            for g in range(2):
                k_lo, k_hi = _lo_hi(k_t, lane, g == 1)
                v_lo, v_hi = _lo_hi(v_t, lane, g == 1, ones_lane=True)
                k_both = (k_lo + k_hi).astype(BF16)
                v_lo = v_lo.astype(BF16)
                v_hi = v_hi.astype(BF16)
                for jj in range(2):
                    j = 2 * g + jj
                    q = tile(q0 + j * LANES)
                    if mixer == 0:
                        q = _head_rmsnorm(q, bqg, lane_lo)
                    s = _dot_nt(_split_heads(q * QSCALE, lane_lo).astype(BF16), k_both)
                    if mixer == 0:
                        pe, _ = _probs(s[:SEQ])
                        po, _ = _probs(s[SEQ:])
                    else:
                        sink_e = sink_ref[layer, 2 * j] * LOG2E
                        sink_o = sink_ref[layer, 2 * j + 1] * LOG2E
                        pe, me = _probs(s[:SEQ], sink_e)
                        po, mo = _probs(s[SEQ:], sink_o)
                    oe = jnp.dot(pe.astype(BF16), v_lo, preferred_element_type=F32)
                    oo = jnp.dot(po.astype(BF16), v_hi, preferred_element_type=F32)
                    le = oe[:, ONES_LANE_LO:ONES_LANE_LO + 1]
                    lo_ = oo[:, ONES_LANE_HI:ONES_LANE_HI + 1]
                    if mixer == 1:
                        le = le + jnp.exp2(sink_e - me)
                        lo_ = lo_ + jnp.exp2(sink_o - mo)
                    o = jnp.where(lane_lo, oe, oo) / jnp.where(lane_lo, le, lo_)
                    att_ref[rows, o0 + j * LANES:o0 + (j + 1) * LANES] = o.astype(BF16)


def _att_ctx(qkv, sp, sink, cache_out, layer, lam_init):
    t = qkv.shape[0]
    nb = t // SEQ
    rows = CTX_NB * SEQ
    wide = pl.BlockSpec((CTX_NB, None, SEQ * A_HEADS, LANES), lambda b: (b, layer, 0, 0))
    small = pl.BlockSpec((CTX_NB, None, SEQ, 4 * LANES), lambda b: (b, layer, 0, 0))
    wide_shape = jax.ShapeDtypeStruct((nb, DEPTH, SEQ * A_HEADS, LANES), F32)
    small_shape = jax.ShapeDtypeStruct((nb, DEPTH, SEQ, 4 * LANES), F32)
    if cache_out is None:
        cache_out = (jnp.zeros(wide_shape.shape, F32), jnp.zeros(wide_shape.shape, F32),
                     jnp.zeros(small_shape.shape, F32))
    passthrough = pl.BlockSpec(memory_space=pl.ANY)
    att, *cache_out = pl.pallas_call(
        functools.partial(_att_ctx_kernel, layer=layer, lam_init=lam_init),
        grid=(nb // CTX_NB,),
        in_specs=[
            pl.BlockSpec((rows, D_QKV), lambda b: (b, 0)),
            pl.BlockSpec((None, 8, LANES), lambda b: (layer, 0, 0)),
            pl.BlockSpec(memory_space=pltpu.SMEM),
            passthrough, passthrough, passthrough,
        ],
        out_specs=[pl.BlockSpec((rows, D_ATT), lambda b: (b, 0)), wide, wide, small],
        out_shape=[jax.ShapeDtypeStruct((t, D_ATT), BF16), wide_shape, wide_shape, small_shape],
        input_output_aliases={3: 1, 4: 2, 5: 3},
        compiler_params=_cparams(("arbitrary",)),
        name="att_ctx",
    )(qkv, sp, sink, *cache_out)
    return att, cache_out


LAT_TQ_NORM = 256
NK = PAST_LEN + DEC_SEQ
KCH = 512
KV_BLK = 512
assert KA % KV_BLK == 0 and VA % KV_BLK == 0 and KB // KV_BLK == VB // KV_BLK and KC // KV_BLK == VC // KV_BLK
PREP_ROWS = 512
N_UNITS = 12
UNITS_PER_STEP = 2
A_STEPS = A_HEADS // UNITS_PER_STEP


def _attend_all_queries(q, k_chunks, v_chunks, masks=None, sink=None, l_lane=None):
    m = l = acc = None
    for c, (k_c, v_c) in enumerate(zip(k_chunks, v_chunks)):
        s = _dot_nt(q, k_c)
        if masks is not None and masks[c] is not None:
            s = jnp.where(masks[c], s, NEG_INF)
        mc = jnp.max(s, axis=-1, keepdims=True)
        if m is None:
            m_new = mc if sink is None else jnp.maximum(mc, sink)
            p = jnp.exp2(s - m_new)
            acc = jnp.dot(p.astype(BF16), v_c, preferred_element_type=F32)
            if l_lane is None:
                l = jnp.sum(p, axis=-1, keepdims=True)
        else:
            m_new = jnp.maximum(m, mc)
            alpha = jnp.exp2(m - m_new)
            p = jnp.exp2(s - m_new)
            acc = alpha * acc + jnp.dot(p.astype(BF16), v_c, preferred_element_type=F32)
            if l_lane is None:
                l = alpha * l + jnp.sum(p, axis=-1, keepdims=True)
        m = m_new
    if l_lane is not None:
        l = acc[:, l_lane:l_lane + 1]
    if sink is not None:
        l = l + jnp.exp2(sink - m)
    return acc, l


def _att_lat_kernel(q_ref, kva_ref, kvv_ref, kvb_ref, kvc_ref,
                    cak_ref, cav_ref, cbk_ref, cbv_ref, cck_ref, ccv_ref,
                    cos_ref, sin_ref, sp_ref, sink_ref,
                    att_ref,
                    ka_scr, va_scr, kb_scr, vb_scr, kc_scr, vc_scr, q_scr,
                    *, layer, lam_init):
    step_id = pl.program_id(1)
    subg = sp_ref[R_SUBG:R_SUBG + 1, :]
    bqg = sp_ref[R_BQG:R_BQG + 1, :]
    bkg = sp_ref[R_BKG:R_BKG + 1, :]

    @pl.when(step_id == 0)
    def _prepare_keys():
        lane = lax.broadcasted_iota(jnp.int32, (PREP_ROWS, LANES), 1)
        lane_lo = lane < HEAD_DIM
        def put(dst, a_k, a_v, pairs):
            lo_rows = pl.ds(pl.multiple_of(dst, PREP_ROWS), PREP_ROWS)
            hi_rows = pl.ds(pl.multiple_of(NK + dst, PREP_ROWS), PREP_ROWS)
            for h in range(A_HEADS):
                k = a_k[h]
                zero = jnp.zeros_like(k)
                ka_scr[h, lo_rows, :] = jnp.where(lane_lo, k, zero).astype(BF16)
                ka_scr[h, hi_rows, :] = jnp.where(lane_lo, zero, k).astype(BF16)
                va_scr[h, lo_rows, :] = a_v[h].astype(BF16)
            for x, scr, is_value in pairs:
                for g in range(2):
                    lo, hi = _lo_hi(x, lane, g == 1, ones_lane=is_value)
                    scr[g, lo_rows, :] = lo.astype(BF16)
                    scr[g, hi_rows, :] = hi.astype(BF16)

        def cached(i, carry):
            r = pl.ds(pl.multiple_of(i * PREP_ROWS, PREP_ROWS), PREP_ROWS)
            head_rows = lambda h: pl.ds(i * (PREP_ROWS * A_HEADS) + h, PREP_ROWS, stride=A_HEADS)
            put(i * PREP_ROWS,
                [cak_ref[head_rows(h), :] for h in range(A_HEADS)],
                [cav_ref[head_rows(h), :] for h in range(A_HEADS)],
                ((cbk_ref[r, :], kb_scr, False), (cbv_ref[r, :], vb_scr, True),
                 (cck_ref[r, :], kc_scr, False), (ccv_ref[r, :], vc_scr, True)))
            return carry
        lax.fori_loop(0, PAST_LEN // PREP_ROWS, cached, 0)

        def latent(i, carry):
            r = pl.ds(pl.multiple_of(i * PREP_ROWS, PREP_ROWS), PREP_ROWS)
            cos = cos_ref[r, :]
            sin = sin_ref[r, :]
            in_blk = lambda ref, col: ref[r, col % KV_BLK:col % KV_BLK + LANES]
            kb = _rope(_head_rmsnorm(in_blk(kvb_ref, KB), bkg, lane_lo), cos, sin, lane)
            kc = _rope(in_blk(kvc_ref, KC), cos, sin, lane)
            put(PAST_LEN + i * PREP_ROWS,
                [_rope(kva_ref[r, h * LANES:(h + 1) * LANES], cos, sin, lane) for h in range(A_HEADS)],
                [kvv_ref[r, h * LANES:(h + 1) * LANES] for h in range(A_HEADS)],
                ((kb, kb_scr, False), (in_blk(kvb_ref, VB), vb_scr, True),
                 (kc, kc_scr, False), (in_blk(kvc_ref, VC), vc_scr, True)))
            return carry
        lax.fori_loop(0, DEC_SEQ // PREP_ROWS, latent, 0)

    lam = _lam(sp_ref, lam_init)
    lane_lo_all = lax.broadcasted_iota(jnp.int32, (DEC_SEQ, LANES), 1) < HEAD_DIM
    chunks = lambda scr, i, base: [scr[i, base + c * KCH:base + (c + 1) * KCH, :] for c in range(NK // KCH)]

    def prepare_queries(normalise):
        tq = LAT_TQ_NORM if normalise else DEC_SEQ
        lane = lax.broadcasted_iota(jnp.int32, (tq, LANES), 1)
        lane_lo = lane < HEAD_DIM

        def step(i, carry):
            r = pl.ds(pl.multiple_of(i * tq, tq), tq)
            for n in range(UNITS_PER_STEP):
                q = q_ref[r, n * LANES:(n + 1) * LANES]
                if normalise:
                    q = _head_rmsnorm(q, bqg, lane_lo)
                q_scr[n, r, :] = (_rope(q, cos_ref[r, :], sin_ref[r, :], lane) * QSCALE).astype(BF16)
            return carry
        lax.fori_loop(0, DEC_SEQ // tq, step, 0)

    @pl.when(step_id < A_STEPS)
    def _mixer_a():
        prepare_queries(False)
        for n in range(UNITS_PER_STEP):
            h = step_id * UNITS_PER_STEP + n
            q = q_scr[n]
            v = chunks(va_scr, h, 0)
            a1, l1 = _attend_all_queries(q, chunks(ka_scr, h, 0), v)
            a2, l2 = _attend_all_queries(q, chunks(ka_scr, h, NK), v)
            att_ref[:, n * LANES:(n + 1) * LANES] = _subln(a1 / l1 - lam * (a2 / l2), subg,
                                                          lam_init).astype(BF16)

    def pair_unit(n, k_scr, v_scr, g, masks=None, sinks=(None, None)):
        q = q_scr[n]
        ae, le = _attend_all_queries(q, chunks(k_scr, g, 0), chunks(v_scr, g, 0),
                                     masks=masks, sink=sinks[0], l_lane=ONES_LANE_LO)
        ao, lo_ = _attend_all_queries(q, chunks(k_scr, g, NK), chunks(v_scr, g, NK),
                                      masks=masks, sink=sinks[1], l_lane=ONES_LANE_HI)
        o = jnp.where(lane_lo_all, ae, ao) / jnp.where(lane_lo_all, le, lo_)
        att_ref[:, n * LANES:(n + 1) * LANES] = o.astype(BF16)

    @pl.when((step_id >= A_STEPS) & (step_id < 2 * A_STEPS))
    def _mixer_b():
        prepare_queries(True)
        for n in range(UNITS_PER_STEP):
            unit = (step_id - A_STEPS) * UNITS_PER_STEP + n
            pair_unit(n, kb_scr, vb_scr, lax.shift_right_logical(unit, 1))

    @pl.when(step_id >= 2 * A_STEPS)
    def _mixer_c():
        prepare_queries(False)
        qpos = lax.broadcasted_iota(jnp.int32, (DEC_SEQ, KCH), 0)
        kcol = lax.broadcasted_iota(jnp.int32, (DEC_SEQ, KCH), 1)
        masks = (None,) * (PAST_LEN // KCH) + tuple(
            jnp.abs(kcol + k0 - qpos) <= WINDOW for k0 in range(0, DEC_SEQ, KCH))
        for n in range(UNITS_PER_STEP):
            unit = (step_id - 2 * A_STEPS) * UNITS_PER_STEP + n
            pair_unit(n, kc_scr, vc_scr, lax.shift_right_logical(unit, 1), masks=masks,
                      sinks=(sink_ref[layer, 2 * unit] * LOG2E, sink_ref[layer, 2 * unit + 1] * LOG2E))


def _step_q_col(s):
    width = UNITS_PER_STEP * LANES
    return jnp.where(s < A_STEPS, s, jnp.where(s < 2 * A_STEPS, QB // width - A_STEPS + s,
                                               QC // width - 2 * A_STEPS + s))


def _att_lat(qkv, caches, cos, sin, sp, sink, layer, lam_init):
    t = qkv.shape[0]
    once = pl.Buffered(1)
    kv_blk = lambda col: pl.BlockSpec((DEC_SEQ, KV_BLK), lambda b, i: (b, col // KV_BLK), pipeline_mode=once)
    cache_blk = lambda r: pl.BlockSpec((None, None, r, LANES), lambda b, i: (b, layer, 0, 0),
                                       pipeline_mode=once)
    return pl.pallas_call(
        functools.partial(_att_lat_kernel, layer=layer, lam_init=lam_init),
        grid=(DEC_BATCH, N_UNITS // UNITS_PER_STEP),
        in_specs=[
            pl.BlockSpec((DEC_SEQ, UNITS_PER_STEP * LANES), lambda b, i: (b, _step_q_col(i))),
            kv_blk(KA), kv_blk(VA), kv_blk(KB), kv_blk(KC),
            cache_blk(PAST_LEN * A_HEADS), cache_blk(PAST_LEN * A_HEADS),
            cache_blk(PAST_LEN), cache_blk(PAST_LEN), cache_blk(PAST_LEN), cache_blk(PAST_LEN),
            pl.BlockSpec((DEC_SEQ, LANES), lambda b, i: (0, 0)),
            pl.BlockSpec((DEC_SEQ, LANES), lambda b, i: (0, 0)),
            pl.BlockSpec((None, 8, LANES), lambda b, i: (layer, 0, 0)),
            pl.BlockSpec(memory_space=pltpu.SMEM),
        ],
        out_specs=pl.BlockSpec((DEC_SEQ, UNITS_PER_STEP * LANES), lambda b, i: (b, i)),
        out_shape=jax.ShapeDtypeStruct((t, D_ATT), BF16),
        scratch_shapes=[
            pltpu.VMEM((A_HEADS, 2 * NK, LANES), BF16),
            pltpu.VMEM((A_HEADS, NK, LANES), BF16),
            pltpu.VMEM((2, 2 * NK, LANES), BF16),
            pltpu.VMEM((2, 2 * NK, LANES), BF16),
            pltpu.VMEM((2, 2 * NK, LANES), BF16),
            pltpu.VMEM((2, 2 * NK, LANES), BF16),
            pltpu.VMEM((UNITS_PER_STEP, DEC_SEQ, LANES), BF16),
        ],
        compiler_params=_cparams(("arbitrary", "arbitrary")),
        name="att_lat",
    )(qkv, qkv, qkv, qkv, qkv, *caches, cos, sin, sp, sink)


POST_TM = 512


def _post_kernel(att_ref, h1_ref, x_ref, mod_ref, g2_ref, wg_ref, wa_ref, wb_ref, wc_ref, wo_ref,
                 o_ref, h_ref, wg_scr, wbr_scr, wo_scr, *, tm, rows_per_mod, row0):
    @pl.when(pl.program_id(0) == 0)
    def _():
        wg_scr[...] = wg_ref[...].astype(BF16)
        for i, w_ref in enumerate((wa_ref, wb_ref, wc_ref)):
            wbr_scr[i] = w_ref[...].astype(BF16)
        wo_scr[...] = wo_ref[...].astype(BF16)

    h1 = h1_ref[...]
    merged = None
    for i in range(N_BRANCH):
        y = jnp.dot(att_ref[:, i * BRANCH:(i + 1) * BRANCH], wbr_scr[i], preferred_element_type=F32)
        gate = jnp.dot(h1, wg_scr[:, i * D_MODEL:(i + 1) * D_MODEL], preferred_element_type=F32)
        term = jax.nn.sigmoid(gate) * y
        merged = term if merged is None else merged + term
    mixed = jnp.dot(merged.astype(BF16), wo_scr[...], preferred_element_type=F32)
    row = row0 + lax.div(pl.program_id(0) * tm, rows_per_mod)
    x1 = x_ref[...] + _mod_block(mod_ref, row, M_G1) * mixed
    o_ref[...] = x1
    y = x1 * lax.rsqrt(jnp.mean(x1 * x1, axis=-1, keepdims=True) + EPS) * g2_ref[...]
    h_ref[...] = (y * (1.0 + _mod_block(mod_ref, row, M_SC2)) + _mod_block(mod_ref, row, M_SH2)).astype(BF16)


def _post(att, h1, x, mod, g2, w_in, wa, wb, wc, wo, layer, rows_per_mod, row0):
    t = x.shape[0]
    tm = POST_TM
    once = pl.Buffered(1)
    full = lambda r: pl.BlockSpec((None, r, D_MODEL), lambda m: (layer, 0, 0), pipeline_mode=once)
    return pl.pallas_call(
        functools.partial(_post_kernel, tm=tm, rows_per_mod=rows_per_mod, row0=row0),
        grid=(t // tm,),
        in_specs=[
            pl.BlockSpec((tm, D_ATT), lambda m: (m, 0)),
            pl.BlockSpec((tm, D_MODEL), lambda m: (m, 0)),
            pl.BlockSpec((tm, D_MODEL), lambda m: (m, 0)),
            _mod_spec(layer),
            pl.BlockSpec((None, 1, D_MODEL), lambda m: (layer, 0, 0)),
            pl.BlockSpec((None, D_MODEL, D_GATE), lambda m: (layer, 0, D_QKV // D_GATE), pipeline_mode=once),
            full(BRANCH), full(BRANCH), full(BRANCH), full(D_MODEL),
        ],
        out_specs=[pl.BlockSpec((tm, D_MODEL), lambda m: (m, 0)),
                   pl.BlockSpec((tm, D_MODEL), lambda m: (m, 0))],
        out_shape=[jax.ShapeDtypeStruct((t, D_MODEL), F32),
                   jax.ShapeDtypeStruct((t, D_MODEL), BF16)],
        scratch_shapes=[pltpu.VMEM((D_MODEL, D_GATE), BF16), pltpu.VMEM((N_BRANCH, BRANCH, D_MODEL), BF16),
                        pltpu.VMEM((D_MODEL, D_MODEL), BF16)],
        compiler_params=_cparams(("arbitrary",)),
        name="post",
    )(att, h1, x, mod, g2, w_in, wa, wb, wc, wo)


FFN_TM = 1024
FFN_TF = 256


FFN_GROUP = 2


def _ffn_kernel(x_ref, h_ref, mod_ref, fg_ref, wa_ref, wb_ref, wo_ref, o_ref,
                acc_scr, wab_scr, wo_scr, *, tm, rows_per_mod, row0, final_norm):
    grp = pl.program_id(0)
    f = pl.program_id(1)
    m = pl.program_id(2)

    @pl.when(f == 0)
    def _():
        acc_scr[m] = jnp.zeros((tm, D_MODEL), F32)

    @pl.when(m == 0)
    def _():
        wab_scr[:, 0:FFN_TF] = wa_ref[...].astype(BF16)
        wab_scr[:, FFN_TF:2 * FFN_TF] = wb_ref[...].astype(BF16)
        wo_scr[...] = wo_ref[...].astype(BF16)

    ab = jnp.dot(h_ref[pl.ds(pl.multiple_of(m * tm, tm), tm), :], wab_scr[...],
                 preferred_element_type=F32)
    a = ab[:, 0:FFN_TF]
    y = (a * jax.nn.sigmoid(a) * ab[:, FFN_TF:2 * FFN_TF]).astype(BF16)
    acc_scr[m] += jnp.dot(y, wo_scr[...], preferred_element_type=F32)

    @pl.when(f == pl.num_programs(1) - 1)
    def _():
        row = row0 + lax.div((grp * FFN_GROUP + m) * tm, rows_per_mod)
        y = x_ref[...] + _mod_block(mod_ref, row, M_G2) * acc_scr[m]
        if final_norm:
            y = y * lax.rsqrt(jnp.mean(y * y, axis=-1, keepdims=True) + EPS) * fg_ref[...]
        o_ref[...] = y


def _ffn(x, h, mod, final_g, w_in, w_out, layer, rows_per_mod, row0):
    t = x.shape[0]
    tm = FFN_TM
    assert rows_per_mod % tm == 0
    nf = D_FF // FFN_TF
    io_idx = lambda grp, f, m: (grp * FFN_GROUP + jnp.where(f == nf - 1, m, 0), 0)
    return pl.pallas_call(
        functools.partial(_ffn_kernel, tm=tm, rows_per_mod=rows_per_mod, row0=row0,
                          final_norm=layer == DEPTH - 1),
        grid=(t // (tm * FFN_GROUP), nf, FFN_GROUP),
        in_specs=[
            pl.BlockSpec((tm, D_MODEL), io_idx),
            pl.BlockSpec((FFN_GROUP * tm, D_MODEL), lambda grp, f, m: (grp, 0)),
            _mod_spec(layer),
            pl.BlockSpec((1, D_MODEL), lambda grp, f, m: (0, 0)),
            pl.BlockSpec((None, D_MODEL, FFN_TF), lambda grp, f, m: (layer, 0, f)),
            pl.BlockSpec((None, D_MODEL, FFN_TF), lambda grp, f, m: (layer, 0, f + nf)),
            pl.BlockSpec((None, FFN_TF, D_MODEL), lambda grp, f, m: (layer, f, 0)),
        ],
        out_specs=pl.BlockSpec((tm, D_MODEL), io_idx),
        out_shape=jax.ShapeDtypeStruct((t, D_MODEL), F32),
        scratch_shapes=[pltpu.VMEM((FFN_GROUP, tm, D_MODEL), F32),
                        pltpu.VMEM((D_MODEL, 2 * FFN_TF), BF16),
                        pltpu.VMEM((FFN_TF, D_MODEL), BF16)],
        compiler_params=_cparams(("arbitrary", "arbitrary", "arbitrary")),
        name="ffn",
    )(x, h, mod, final_g, w_in, w_in, w_out)


def _rope_tables():
    rows = DEC_SEQ // GRID_W
    row = jnp.repeat(jnp.arange(rows), GRID_W).astype(F32)
    col = jnp.tile(jnp.arange(GRID_W), rows).astype(F32)
    n = HEAD_DIM // 4
    inv = ROPE_THETA ** (-jnp.arange(n, dtype=F32) / n)
    ang = jnp.concatenate([row[:, None] * inv, col[:, None] * inv], axis=-1)
    cos, sin = jnp.cos(ang), jnp.sin(ang)
    cos_t = jnp.tile(cos, (1, 4))
    sin_t = jnp.tile(jnp.concatenate([-sin, sin], axis=-1), (1, 2))
    return cos_t, sin_t


def _pack_small(a_lam_q1, a_lam_k1, a_lam_q2, a_lam_k2, a_subln_g, b_qnorm_g, b_knorm_g):
    pad = lambda v: jnp.pad(v, ((0, 0), (0, LANES - HEAD_DIM)))
    rows = [pad(a_lam_q1), pad(a_lam_k1), pad(a_lam_q2), pad(a_lam_k2), a_subln_g,
            jnp.tile(b_qnorm_g, (1, 2)), jnp.tile(b_knorm_g, (1, 2)),
            jnp.zeros((DEPTH, LANES), F32)]
    return jnp.stack(rows, axis=1)


def kernel(x_prompt, x_sample, cache_a_k, cache_a_v, cache_b_k, cache_b_v, cache_c_k, cache_c_v, c, c_ctx, w_mod, b_mod, norm1_g, norm2_g, w_in, a_lam_q1, a_lam_k1, a_lam_q2, a_lam_k2, a_subln_g, b_qnorm_g, b_knorm_g, c_sink, w_br_a, w_br_b, w_br_c, w_out, w_ffn_in, w_ffn_out, final_g):
    t_ctx = BATCH * SEQ
    t_lat = DEC_BATCH * DEC_SEQ
    xp = x_prompt.reshape(t_ctx, D_MODEL)
    xs = x_sample.reshape(t_lat, D_MODEL)
    cv8 = jnp.concatenate([c_ctx[None, :], c, jnp.zeros((8 - 1 - DEC_BATCH, D_MODEL), F32)], axis=0)
    mod = _modulation(cv8, w_mod, b_mod)
    cos_t, sin_t = _rope_tables()
    sp = _pack_small(a_lam_q1, a_lam_k1, a_lam_q2, a_lam_k2, a_subln_g, b_qnorm_g, b_knorm_g)
    n1 = norm1_g.reshape(DEPTH, 1, D_MODEL)
    n2 = norm2_g.reshape(DEPTH, 1, D_MODEL)
    caches = (cache_a_k.reshape(DEC_BATCH, DEPTH, PAST_LEN * A_HEADS, LANES),
              cache_a_v.reshape(DEC_BATCH, DEPTH, PAST_LEN * A_HEADS, LANES),
              cache_b_k.reshape(DEC_BATCH, DEPTH, PAST_LEN, LANES),
              cache_b_v.reshape(DEC_BATCH, DEPTH, PAST_LEN, LANES),
              cache_c_k.reshape(DEC_BATCH, DEPTH, PAST_LEN, LANES),
              cache_c_v.reshape(DEC_BATCH, DEPTH, PAST_LEN, LANES))
    fg = final_g.reshape(1, D_MODEL)
    cache_out = None
    for l in range(DEPTH):
        lam_init = 0.8 - 0.6 * math.exp(-0.3 * l)

        qkv_c, h1_c = _proj(xp, mod, n1, w_in, l, t_ctx, 0)
        att_c, cache_out = _att_ctx(qkv_c, sp, c_sink, cache_out, l, lam_init)
        xp, hp = _post(att_c, h1_c, xp, mod, n2, w_in, w_br_a, w_br_b, w_br_c, w_out, l, t_ctx, 0)
        xp = _ffn(xp, hp, mod, fg, w_ffn_in, w_ffn_out, l, t_ctx, 0)

        qkv_s, h1_s = _proj(xs, mod, n1, w_in, l, DEC_SEQ, 1)
        att_s = _att_lat(qkv_s, caches, cos_t, sin_t, sp, c_sink, l, lam_init)
        xs, hs = _post(att_s, h1_s, xs, mod, n2, w_in, w_br_a, w_br_b, w_br_c, w_out, l, DEC_SEQ, 1)
        xs = _ffn(xs, hs, mod, fg, w_ffn_in, w_ffn_out, l, DEC_SEQ, 1)

    y_prompt = xp.reshape(BATCH, SEQ, D_MODEL)
    y_sample = xs.reshape(DEC_BATCH, DEC_SEQ, D_MODEL)
    ka_all, va_all, small_all = cache_out
    wide = tuple(a.reshape(BATCH, DEPTH, SEQ, A_HEADS, 2 * HEAD_DIM) for a in (ka_all, va_all))
    small = tuple(small_all[..., c:c + LANES].reshape(BATCH, DEPTH, SEQ, 2, HEAD_DIM)
                  for c in (SMALL_KB, SMALL_VB, SMALL_KC, SMALL_VC))
    return (y_prompt, y_sample) + wide + small
```

```python
import functools
import math

import jax
import jax.numpy as jnp
from jax import lax
from jax.experimental import pallas as pl
from jax.experimental.pallas import tpu as pltpu

D_MODEL = 1024
BATCH = 16
SEQ = 256
DEPTH = 4
DEC_BATCH = 4
DEC_SEQ = 1024
PAST_LEN = 512
GRID_W = 64
HEAD_DIM = 64
ROPE_THETA = 10000.0
EPS = 1e-6
NEG_INF = -1e30
A_HEADS = 4
WINDOW = 128
D_FF = -(-8 * D_MODEL // (3 * 256)) * 256
N_MOD = 6
N_BRANCH = 3
BRANCH = 512
D_ATT = N_BRANCH * BRANCH
D_GATE = N_BRANCH * D_MODEL
LANES = 128
HALF = HEAD_DIM // 2
LOG2E = math.log2(math.e)
QSCALE = HEAD_DIM ** -0.5 * LOG2E

_KV = 2 * HEAD_DIM
QA, KA, VA = 0, BRANCH, 2 * BRANCH
QB, KB, VB = 3 * BRANCH, 4 * BRANCH, 4 * BRANCH + _KV
QC, KC, VC = VB + _KV, VB + _KV + BRANCH, VB + 2 * _KV + BRANCH
D_QKV = VC + _KV
R_LQ1, R_LK1, R_LQ2, R_LK2, R_SUBG, R_BQG, R_BKG = range(7)
M_SH1, M_SC1, M_G1, M_SH2, M_SC2, M_G2 = range(6)

F32 = jnp.float32
BF16 = jnp.bfloat16
VMEM_LIMIT = 56 * 1024 * 1024


def _cparams(sem):
    return pltpu.CompilerParams(dimension_semantics=sem, vmem_limit_bytes=VMEM_LIMIT)


MOD_TN = 1536


def _mod_kernel(cv_ref, w_ref, b_ref, o_ref):
    cv = cv_ref[...]
    s = (cv * jax.nn.sigmoid(cv)).astype(BF16)
    o_ref[...] = jnp.dot(s, w_ref[...].astype(BF16), preferred_element_type=F32) + b_ref[...]


def _modulation(cv8, w_mod, b_mod):
    n = N_MOD * D_MODEL
    return pl.pallas_call(
        _mod_kernel,
        grid=(DEPTH, n // MOD_TN),
        in_specs=[
            pl.BlockSpec((8, D_MODEL), lambda l, j: (0, 0)),
            pl.BlockSpec((None, D_MODEL, MOD_TN), lambda l, j: (l, 0, j)),
            pl.BlockSpec((None, 1, MOD_TN), lambda l, j: (l, 0, j)),
        ],
        out_specs=pl.BlockSpec((None, 8, MOD_TN), lambda l, j: (l, 0, j)),
        out_shape=jax.ShapeDtypeStruct((DEPTH, 8, n), F32),
        compiler_params=_cparams(("arbitrary", "arbitrary")),
        name="modulation",
    )(cv8, w_mod, b_mod.reshape(DEPTH, 1, n))


NORM_CHUNK = 256


def _mod_block(mod_ref, row, blk):
    return mod_ref[pl.ds(row, 1), blk * D_MODEL:(blk + 1) * D_MODEL]


def _norm_mod_rows(x_ref, h_ref, g, mod_ref, sc_blk, sh_blk, rows, mod_row0, rows_per_mod):
    def body(i, carry):
        r = pl.ds(pl.multiple_of(i * NORM_CHUNK, NORM_CHUNK), NORM_CHUNK)
        row = mod_row0 + lax.div(i * NORM_CHUNK, rows_per_mod)
        x = x_ref[r, :]
        y = x * lax.rsqrt(jnp.mean(x * x, axis=-1, keepdims=True) + EPS) * g
        h_ref[r, :] = (y * (1.0 + _mod_block(mod_ref, row, sc_blk))
                       + _mod_block(mod_ref, row, sh_blk)).astype(BF16)
        return carry
    lax.fori_loop(0, rows // NORM_CHUNK, body, 0)


def _head_rmsnorm(x, g, lane_lo):
    x2 = x * x
    zero = jnp.zeros_like(x2)
    lo = jnp.sum(jnp.where(lane_lo, x2, zero), axis=-1, keepdims=True)
    hi = jnp.sum(jnp.where(lane_lo, zero, x2), axis=-1, keepdims=True)
    ms = jnp.where(lane_lo, lo, hi) * (1.0 / HEAD_DIM)
    return x * lax.rsqrt(ms + EPS) * g


def _rope(x, cos, sin, lane):
    partner = jnp.where((lane & HALF) == 0, pltpu.roll(x, LANES - HALF, 1), pltpu.roll(x, HALF, 1))
    return x * cos + partner * sin


ONES_LANE_LO, ONES_LANE_HI = HEAD_DIM, 0


def _lo_hi(x, lane, src_hi, ones_lane=False):
    lane_lo = lane < HEAD_DIM
    other = pltpu.roll(x, HEAD_DIM, 1)
    zero = jnp.zeros_like(x)
    lo = jnp.where(lane_lo, other if src_hi else x, zero)
    hi = jnp.where(lane_lo, zero, x if src_hi else other)
    if ones_lane:
        lo = jnp.where(lane == ONES_LANE_LO, 1.0, lo)
        hi = jnp.where(lane == ONES_LANE_HI, 1.0, hi)
    return lo, hi


def _split_heads(q, lane_lo):
    zero = jnp.zeros_like(q)
    return jnp.concatenate([jnp.where(lane_lo, q, zero), jnp.where(lane_lo, zero, q)], axis=0)


def _dot_nt(a, b):
    return lax.dot_general(a, b, (((1,), (1,)), ((), ())), preferred_element_type=F32)


def _probs(s, extra=None):
    m = jnp.max(s, axis=-1, keepdims=True)
    if extra is not None:
        m = jnp.maximum(m, extra)
    return jnp.exp2(s - m), m


def _lam(sp_ref, lam_init):
    dot1 = jnp.sum(sp_ref[R_LQ1:R_LQ1 + 1, :] * sp_ref[R_LK1:R_LK1 + 1, :], axis=-1, keepdims=True)
    dot2 = jnp.sum(sp_ref[R_LQ2:R_LQ2 + 1, :] * sp_ref[R_LK2:R_LK2 + 1, :], axis=-1, keepdims=True)
    return jnp.exp(dot1) - jnp.exp(dot2) + lam_init


def _subln(o, g, lam_init):
    return o * lax.rsqrt(jnp.mean(o * o, axis=-1, keepdims=True) + EPS) * g * (1.0 - lam_init)


def _mod_spec(layer):
    return pl.BlockSpec((None, 8, N_MOD * D_MODEL), lambda *_: (layer, 0, 0))


PROJ_TM = 2048
PROJ_TN = 512


def _proj_kernel(x_ref, mod_ref, g_ref, w_ref, qkv_ref, hout_ref, h_scr, w_scr, *, tm, rows_per_mod, row0):
    n = pl.program_id(0)
    m = pl.program_id(1)

    @pl.when(n == 0)
    def _():
        _norm_mod_rows(x_ref, h_scr.at[m], g_ref[...], mod_ref, M_SC1, M_SH1, tm,
                       row0 + lax.div(m * tm, rows_per_mod), rows_per_mod)
        hout_ref[...] = h_scr[m]

    @pl.when(m == 0)
    def _():
        w_scr[...] = w_ref[...].astype(BF16)

    qkv_ref[...] = jnp.dot(h_scr[m], w_scr[...], preferred_element_type=F32)


def _proj(x, mod, g, w_in, layer, rows_per_mod, row0):
    t = x.shape[0]
    tm = PROJ_TM
    nm = t // tm
    first_pass = lambda n, m: (jnp.where(n == 0, m, nm - 1), 0)
    return pl.pallas_call(
        functools.partial(_proj_kernel, tm=tm, rows_per_mod=rows_per_mod, row0=row0),
        grid=(D_QKV // PROJ_TN, nm),
        in_specs=[
            pl.BlockSpec((tm, D_MODEL), first_pass),
            _mod_spec(layer),
            pl.BlockSpec((None, 1, D_MODEL), lambda n, m: (layer, 0, 0)),
            pl.BlockSpec((None, D_MODEL, PROJ_TN), lambda n, m: (layer, 0, n)),
        ],
        out_specs=[pl.BlockSpec((tm, PROJ_TN), lambda n, m: (m, n)),
                   pl.BlockSpec((tm, D_MODEL), first_pass)],
        out_shape=[jax.ShapeDtypeStruct((t, D_QKV), F32),
                   jax.ShapeDtypeStruct((t, D_MODEL), BF16)],
        scratch_shapes=[pltpu.VMEM((nm, tm, D_MODEL), BF16), pltpu.VMEM((D_MODEL, PROJ_TN), BF16)],
        compiler_params=_cparams(("arbitrary", "arbitrary")),
        name="proj",
    )(x, mod, g, w_in)


CTX_NB = 1
SMALL_KB, SMALL_VB, SMALL_KC, SMALL_VC = (i * LANES for i in range(4))


def _att_ctx_kernel(qkv_ref, sp_ref, sink_ref, ka_in, va_in, small_in,
                    att_ref, ka_ref, va_ref, small_ref, *, layer, lam_init):
    del ka_in, va_in, small_in
    lane = lax.broadcasted_iota(jnp.int32, (SEQ, LANES), 1)
    lane_lo = lane < HEAD_DIM
    lam = _lam(sp_ref, lam_init)
    subg = sp_ref[R_SUBG:R_SUBG + 1, :]
    bqg = sp_ref[R_BQG:R_BQG + 1, :]
    bkg = sp_ref[R_BKG:R_BKG + 1, :]

    for i in range(CTX_NB):
        rows = slice(i * SEQ, (i + 1) * SEQ)
        tile = lambda c: qkv_ref[rows, c:c + LANES]

        for h in range(A_HEADS):
            q = _split_heads(tile(QA + h * LANES) * QSCALE, lane_lo).astype(BF16)
            k = tile(KA + h * LANES)
            v = tile(VA + h * LANES)
            ka_ref[i, pl.ds(h, SEQ, stride=A_HEADS), :] = k
            va_ref[i, pl.ds(h, SEQ, stride=A_HEADS), :] = v
            v = v.astype(BF16)
            s = _dot_nt(q, k.astype(BF16))
            p1, _ = _probs(s[:SEQ])
            p2, _ = _probs(s[SEQ:])
            o1 = jnp.dot(p1.astype(BF16), v, preferred_element_type=F32) / jnp.sum(p1, axis=-1, keepdims=True)
            o2 = jnp.dot(p2.astype(BF16), v, preferred_element_type=F32) / jnp.sum(p2, axis=-1, keepdims=True)
            o = _subln(o1 - lam * o2, subg, lam_init)
            att_ref[rows, h * LANES:(h + 1) * LANES] = o.astype(BF16)

        kb = _head_rmsnorm(tile(KB), bkg, lane_lo)
        vb, kc, vc = tile(VB), tile(KC), tile(VC)
        for x, c in ((kb, SMALL_KB), (vb, SMALL_VB), (kc, SMALL_KC), (vc, SMALL_VC)):
            small_ref[i, :, c:c + LANES] = x
        for mixer, (q0, k_t, v_t, o0) in enumerate(((QB, kb, vb, BRANCH), (QC, kc, vc, 2 * BRANCH))):
            for g in range(2):
                k_lo, k_hi = _lo_hi(k_t, lane, g == 1)
                v_lo, v_hi = _lo_hi(v_t, lane, g == 1, ones_lane=True)
                k_both = (k_lo + k_hi).astype(BF16)
                v_lo = v_lo.astype(BF16)
                v_hi = v_hi.astype(BF16)
                for jj in range(2):
                    j = 2 * g + jj
                    q = tile(q0 + j * LANES)
                    if mixer == 0:
                        q = _head_rmsnorm(q, bqg, lane_lo)
                    s = _dot_nt(_split_heads(q * QSCALE, lane_lo).astype(BF16), k_both)
                    if mixer == 0:
                        pe, _ = _probs(s[:SEQ])
                        po, _ = _probs(s[SEQ:])
                    else:
                        sink_e = sink_ref[layer, 2 * j] * LOG2E
                        sink_o = sink_ref[layer, 2 * j + 1] * LOG2E
                        pe, me = _probs(s[:SEQ], sink_e)
                        po, mo = _probs(s[SEQ:], sink_o)
                    oe = jnp.dot(pe.astype(BF16), v_lo, preferred_element_type=F32)
                    oo = jnp.dot(po.astype(BF16), v_hi, preferred_element_type=F32)
                    le = oe[:, ONES_LANE_LO:ONES_LANE_LO + 1]
                    lo_ = oo[:, ONES_LANE_HI:ONES_LANE_HI + 1]
                    if mixer == 1:
                        le = le + jnp.exp2(sink_e - me)
                        lo_ = lo_ + jnp.exp2(sink_o - mo)
                    o = jnp.where(lane_lo, oe, oo) / jnp.where(lane_lo, le, lo_)
                    att_ref[rows, o0 + j * LANES:o0 + (j + 1) * LANES] = o.astype(BF16)


def _att_ctx(qkv, sp, sink, cache_out, layer, lam_init):
    t = qkv.shape[0]
    nb = t // SEQ
    rows = CTX_NB * SEQ
    wide = pl.BlockSpec((CTX_NB, None, SEQ * A_HEADS, LANES), lambda b: (b, layer, 0, 0))
    small = pl.BlockSpec((CTX_NB, None, SEQ, 4 * LANES), lambda b: (b, layer, 0, 0))
    wide_shape = jax.ShapeDtypeStruct((nb, DEPTH, SEQ * A_HEADS, LANES), F32)
    small_shape = jax.ShapeDtypeStruct((nb, DEPTH, SEQ, 4 * LANES), F32)
    if cache_out is None:
        cache_out = (jnp.zeros(wide_shape.shape, F32), jnp.zeros(wide_shape.shape, F32),
                     jnp.zeros(small_shape.shape, F32))
    passthrough = pl.BlockSpec(memory_space=pl.ANY)
    att, *cache_out = pl.pallas_call(
        functools.partial(_att_ctx_kernel, layer=layer, lam_init=lam_init),
        grid=(nb // CTX_NB,),
        in_specs=[
            pl.BlockSpec((rows, D_QKV), lambda b: (b, 0)),
            pl.BlockSpec((None, 8, LANES), lambda b: (layer, 0, 0)),
            pl.BlockSpec(memory_space=pltpu.SMEM),
            passthrough, passthrough, passthrough,
        ],
        out_specs=[pl.BlockSpec((rows, D_ATT), lambda b: (b, 0)), wide, wide, small],
        out_shape=[jax.ShapeDtypeStruct((t, D_ATT), BF16), wide_shape, wide_shape, small_shape],
        input_output_aliases={3: 1, 4: 2, 5: 3},
        compiler_params=_cparams(("arbitrary",)),
        name="att_ctx",
    )(qkv, sp, sink, *cache_out)
    return att, cache_out


LAT_TQ_NORM = 256
NK = PAST_LEN + DEC_SEQ
KCH = 512
KV_BLK = 512
assert KA % KV_BLK == 0 and VA % KV_BLK == 0 and KB // KV_BLK == VB // KV_BLK and KC // KV_BLK == VC // KV_BLK
PREP_ROWS = 512
N_UNITS = 12
UNITS_PER_STEP = 2
A_STEPS = A_HEADS // UNITS_PER_STEP


def _attend_all_queries(q, k_chunks, v_chunks, masks=None, sink=None, l_lane=None):
    m = l = acc = None
    for c, (k_c, v_c) in enumerate(zip(k_chunks, v_chunks)):
        s = _dot_nt(q, k_c)
        if masks is not None and masks[c] is not None:
            s = jnp.where(masks[c], s, NEG_INF)
        mc = jnp.max(s, axis=-1, keepdims=True)
        if m is None:
            m_new = mc if sink is None else jnp.maximum(mc, sink)
            p = jnp.exp2(s - m_new)
            acc = jnp.dot(p.astype(BF16), v_c, preferred_element_type=F32)
            if l_lane is None:
                l = jnp.sum(p, axis=-1, keepdims=True)
        else:
            m_new = jnp.maximum(m, mc)
            alpha = jnp.exp2(m - m_new)
            p = jnp.exp2(s - m_new)
            acc = alpha * acc + jnp.dot(p.astype(BF16), v_c, preferred_element_type=F32)
            if l_lane is None:
                l = alpha * l + jnp.sum(p, axis=-1, keepdims=True)
        m = m_new
    if l_lane is not None:
        l = acc[:, l_lane:l_lane + 1]
    if sink is not None:
        l = l + jnp.exp2(sink - m)
    return acc, l


def _att_lat_kernel(q_ref, kva_ref, kvv_ref, kvb_ref, kvc_ref,
                    cak_ref, cav_ref, cbk_ref, cbv_ref, cck_ref, ccv_ref,
                    cos_ref, sin_ref, sp_ref, sink_ref,
                    att_ref,
                    ka_scr, va_scr, kb_scr, vb_scr, kc_scr, vc_scr, q_scr,
                    *, layer, lam_init):
    step_id = pl.program_id(1)
    subg = sp_ref[R_SUBG:R_SUBG + 1, :]
    bqg = sp_ref[R_BQG:R_BQG + 1, :]
    bkg = sp_ref[R_BKG:R_BKG + 1, :]

    @pl.when(step_id == 0)
    def _prepare_keys():
        lane = lax.broadcasted_iota(jnp.int32, (PREP_ROWS, LANES), 1)
        lane_lo = lane < HEAD_DIM
        def put(dst, a_k, a_v, pairs):
            lo_rows = pl.ds(pl.multiple_of(dst, PREP_ROWS), PREP_ROWS)
            hi_rows = pl.ds(pl.multiple_of(NK + dst, PREP_ROWS), PREP_ROWS)
            for h in range(A_HEADS):
                k = a_k[h]
                zero = jnp.zeros_like(k)
                ka_scr[h, lo_rows, :] = jnp.where(lane_lo, k, zero).astype(BF16)
                ka_scr[h, hi_rows, :] = jnp.where(lane_lo, zero, k).astype(BF16)
                va_scr[h, lo_rows, :] = a_v[h].astype(BF16)
            for x, scr, is_value in pairs:
                for g in range(2):
                    lo, hi = _lo_hi(x, lane, g == 1, ones_lane=is_value)
                    scr[g, lo_rows, :] = lo.astype(BF16)
                    scr[g, hi_rows, :] = hi.astype(BF16)

        def cached(i, carry):
            r = pl.ds(pl.multiple_of(i * PREP_ROWS, PREP_ROWS), PREP_ROWS)
            head_rows = lambda h: pl.ds(i * (PREP_ROWS * A_HEADS) + h, PREP_ROWS, stride=A_HEADS)
            put(i * PREP_ROWS,
                [cak_ref[head_rows(h), :] for h in range(A_HEADS)],
                [cav_ref[head_rows(h), :] for h in range(A_HEADS)],
                ((cbk_ref[r, :], kb_scr, False), (cbv_ref[r, :], vb_scr, True),
                 (cck_ref[r, :], kc_scr, False), (ccv_ref[r, :], vc_scr, True)))
            return carry
        lax.fori_loop(0, PAST_LEN // PREP_ROWS, cached, 0)

        def latent(i, carry):
            r = pl.ds(pl.multiple_of(i * PREP_ROWS, PREP_ROWS), PREP_ROWS)
            cos = cos_ref[r, :]
            sin = sin_ref[r, :]
            in_blk = lambda ref, col: ref[r, col % KV_BLK:col % KV_BLK + LANES]
            kb = _rope(_head_rmsnorm(in_blk(kvb_ref, KB), bkg, lane_lo), cos, sin, lane)
            kc = _rope(in_blk(kvc_ref, KC), cos, sin, lane)
            put(PAST_LEN + i * PREP_ROWS,
                [_rope(kva_ref[r, h * LANES:(h + 1) * LANES], cos, sin, lane) for h in range(A_HEADS)],
                [kvv_ref[r, h * LANES:(h + 1) * LANES] for h in range(A_HEADS)],
                ((kb, kb_scr, False), (in_blk(kvb_ref, VB), vb_scr, True),
                 (kc, kc_scr, False), (in_blk(kvc_ref, VC), vc_scr, True)))
            return carry
        lax.fori_loop(0, DEC_SEQ // PREP_ROWS, latent, 0)

    lam = _lam(sp_ref, lam_init)
    lane_lo_all = lax.broadcasted_iota(jnp.int32, (DEC_SEQ, LANES), 1) < HEAD_DIM
    chunks = lambda scr, i, base: [scr[i, base + c * KCH:base + (c + 1) * KCH, :] for c in range(NK // KCH)]

    def prepare_queries(normalise):
        tq = LAT_TQ_NORM if normalise else DEC_SEQ
        lane = lax.broadcasted_iota(jnp.int32, (tq, LANES), 1)
        lane_lo = lane < HEAD_DIM

        def step(i, carry):
            r = pl.ds(pl.multiple_of(i * tq, tq), tq)
            for n in range(UNITS_PER_STEP):
                q = q_ref[r, n * LANES:(n + 1) * LANES]
                if normalise:
                    q = _head_rmsnorm(q, bqg, lane_lo)
                q_scr[n, r, :] = (_rope(q, cos_ref[r, :], sin_ref[r, :], lane) * QSCALE).astype(BF16)
            return carry
        lax.fori_loop(0, DEC_SEQ // tq, step, 0)

    @pl.when(step_id < A_STEPS)
    def _mixer_a():
        prepare_queries(False)
        for n in range(UNITS_PER_STEP):
            h = step_id * UNITS_PER_STEP + n
            q = q_scr[n]
            v = chunks(va_scr, h, 0)
            a1, l1 = _attend_all_queries(q, chunks(ka_scr, h, 0), v)
            a2, l2 = _attend_all_queries(q, chunks(ka_scr, h, NK), v)
            att_ref[:, n * LANES:(n + 1) * LANES] = _subln(a1 / l1 - lam * (a2 / l2), subg,
                                                          lam_init).astype(BF16)

    def pair_unit(n, k_scr, v_scr, g, masks=None, sinks=(None, None)):
        q = q_scr[n]
        ae, le = _attend_all_queries(q, chunks(k_scr, g, 0), chunks(v_scr, g, 0),
                                     masks=masks, sink=sinks[0], l_lane=ONES_LANE_LO)
        ao, lo_ = _attend_all_queries(q, chunks(k_scr, g, NK), chunks(v_scr, g, NK),
                                      masks=masks, sink=sinks[1], l_lane=ONES_LANE_HI)
        o = jnp.where(lane_lo_all, ae, ao) / jnp.where(lane_lo_all, le, lo_)
        att_ref[:, n * LANES:(n + 1) * LANES] = o.astype(BF16)

    @pl.when((step_id >= A_STEPS) & (step_id < 2 * A_STEPS))
    def _mixer_b():
        prepare_queries(True)
        for n in range(UNITS_PER_STEP):
            unit = (step_id - A_STEPS) * UNITS_PER_STEP + n
            pair_unit(n, kb_scr, vb_scr, lax.shift_right_logical(unit, 1))

    @pl.when(step_id >= 2 * A_STEPS)
    def _mixer_c():
        prepare_queries(False)
        qpos = lax.broadcasted_iota(jnp.int32, (DEC_SEQ, KCH), 0)
        kcol = lax.broadcasted_iota(jnp.int32, (DEC_SEQ, KCH), 1)
        masks = (None,) * (PAST_LEN // KCH) + tuple(
            jnp.abs(kcol + k0 - qpos) <= WINDOW for k0 in range(0, DEC_SEQ, KCH))
        for n in range(UNITS_PER_STEP):
            unit = (step_id - 2 * A_STEPS) * UNITS_PER_STEP + n
            pair_unit(n, kc_scr, vc_scr, lax.shift_right_logical(unit, 1), masks=masks,
                      sinks=(sink_ref[layer, 2 * unit] * LOG2E, sink_ref[layer, 2 * unit + 1] * LOG2E))


def _step_q_col(s):
    width = UNITS_PER_STEP * LANES
    return jnp.where(s < A_STEPS, s, jnp.where(s < 2 * A_STEPS, QB // width - A_STEPS + s,
                                               QC // width - 2 * A_STEPS + s))


def _att_lat(qkv, caches, cos, sin, sp, sink, layer, lam_init):
    t = qkv.shape[0]
    once = pl.Buffered(1)
    kv_blk = lambda col: pl.BlockSpec((DEC_SEQ, KV_BLK), lambda b, i: (b, col // KV_BLK))
    cache_blk = lambda r: pl.BlockSpec((None, None, r, LANES), lambda b, i: (b, layer, 0, 0),
                                       pipeline_mode=once)
    return pl.pallas_call(
        functools.partial(_att_lat_kernel, layer=layer, lam_init=lam_init),
        grid=(DEC_BATCH, N_UNITS // UNITS_PER_STEP),
        in_specs=[
            pl.BlockSpec((DEC_SEQ, UNITS_PER_STEP * LANES), lambda b, i: (b, _step_q_col(i))),
            kv_blk(KA), kv_blk(VA), kv_blk(KB), kv_blk(KC),
            cache_blk(PAST_LEN * A_HEADS), cache_blk(PAST_LEN * A_HEADS),
            cache_blk(PAST_LEN), cache_blk(PAST_LEN), cache_blk(PAST_LEN), cache_blk(PAST_LEN),
            pl.BlockSpec((DEC_SEQ, LANES), lambda b, i: (0, 0)),
            pl.BlockSpec((DEC_SEQ, LANES), lambda b, i: (0, 0)),
            pl.BlockSpec((None, 8, LANES), lambda b, i: (layer, 0, 0)),
            pl.BlockSpec(memory_space=pltpu.SMEM),
        ],
        out_specs=pl.BlockSpec((DEC_SEQ, UNITS_PER_STEP * LANES), lambda b, i: (b, i)),
        out_shape=jax.ShapeDtypeStruct((t, D_ATT), BF16),
        scratch_shapes=[
            pltpu.VMEM((A_HEADS, 2 * NK, LANES), BF16),
            pltpu.VMEM((A_HEADS, NK, LANES), BF16),
            pltpu.VMEM((2, 2 * NK, LANES), BF16),
            pltpu.VMEM((2, 2 * NK, LANES), BF16),
            pltpu.VMEM((2, 2 * NK, LANES), BF16),
            pltpu.VMEM((2, 2 * NK, LANES), BF16),
            pltpu.VMEM((UNITS_PER_STEP, DEC_SEQ, LANES), BF16),
        ],
        compiler_params=_cparams(("arbitrary", "arbitrary")),
        name="att_lat",
    )(qkv, qkv, qkv, qkv, qkv, *caches, cos, sin, sp, sink)


POST_TM = 512


def _post_kernel(att_ref, h1_ref, x_ref, mod_ref, g2_ref, wg_ref, wa_ref, wb_ref, wc_ref, wo_ref,
                 o_ref, h_ref, wg_scr, wbr_scr, wo_scr, *, tm, rows_per_mod, row0):
    @pl.when(pl.program_id(0) == 0)
    def _():
        wg_scr[...] = wg_ref[...].astype(BF16)
        for i, w_ref in enumerate((wa_ref, wb_ref, wc_ref)):
            wbr_scr[i] = w_ref[...].astype(BF16)
        wo_scr[...] = wo_ref[...].astype(BF16)

    h1 = h1_ref[...]
    merged = None
    for i in range(N_BRANCH):
        y = jnp.dot(att_ref[:, i * BRANCH:(i + 1) * BRANCH], wbr_scr[i], preferred_element_type=F32)
        gate = jnp.dot(h1, wg_scr[:, i * D_MODEL:(i + 1) * D_MODEL], preferred_element_type=F32)
        term = jax.nn.sigmoid(gate) * y
        merged = term if merged is None else merged + term
    mixed = jnp.dot(merged.astype(BF16), wo_scr[...], preferred_element_type=F32)
    row = row0 + lax.div(pl.program_id(0) * tm, rows_per_mod)
    x1 = x_ref[...] + _mod_block(mod_ref, row, M_G1) * mixed
    o_ref[...] = x1
    y = x1 * lax.rsqrt(jnp.mean(x1 * x1, axis=-1, keepdims=True) + EPS) * g2_ref[...]
    h_ref[...] = (y * (1.0 + _mod_block(mod_ref, row, M_SC2)) + _mod_block(mod_ref, row, M_SH2)).astype(BF16)


def _post(att, h1, x, mod, g2, w_in, wa, wb, wc, wo, layer, rows_per_mod, row0):
    t = x.shape[0]
    tm = POST_TM
    once = pl.Buffered(1)
    full = lambda r: pl.BlockSpec((None, r, D_MODEL), lambda m: (layer, 0, 0), pipeline_mode=once)
    return pl.pallas_call(
        functools.partial(_post_kernel, tm=tm, rows_per_mod=rows_per_mod, row0=row0),
        grid=(t // tm,),
        in_specs=[
            pl.BlockSpec((tm, D_ATT), lambda m: (m, 0)),
            pl.BlockSpec((tm, D_MODEL), lambda m: (m, 0)),
            pl.BlockSpec((tm, D_MODEL), lambda m: (m, 0)),
            _mod_spec(layer),
            pl.BlockSpec((None, 1, D_MODEL), lambda m: (layer, 0, 0)),
            pl.BlockSpec((None, D_MODEL, D_GATE), lambda m: (layer, 0, D_QKV // D_GATE), pipeline_mode=once),
            full(BRANCH), full(BRANCH), full(BRANCH), full(D_MODEL),
        ],
        out_specs=[pl.BlockSpec((tm, D_MODEL), lambda m: (m, 0)),
                   pl.BlockSpec((tm, D_MODEL), lambda m: (m, 0))],
        out_shape=[jax.ShapeDtypeStruct((t, D_MODEL), F32),
                   jax.ShapeDtypeStruct((t, D_MODEL), BF16)],
        scratch_shapes=[pltpu.VMEM((D_MODEL, D_GATE), BF16), pltpu.VMEM((N_BRANCH, BRANCH, D_MODEL), BF16),
                        pltpu.VMEM((D_MODEL, D_MODEL), BF16)],
        compiler_params=_cparams(("arbitrary",)),
        name="post",
    )(att, h1, x, mod, g2, w_in, wa, wb, wc, wo)


FFN_TM = 1024
FFN_TF = 256


FFN_GROUP = 2


def _ffn_kernel(x_ref, h_ref, mod_ref, fg_ref, wa_ref, wb_ref, wo_ref, o_ref,
                acc_scr, wab_scr, wo_scr, *, tm, rows_per_mod, row0, final_norm):
    grp = pl.program_id(0)
    f = pl.program_id(1)
    m = pl.program_id(2)

    @pl.when(f == 0)
    def _():
        acc_scr[m] = jnp.zeros((tm, D_MODEL), F32)

    @pl.when(m == 0)
    def _():
        wab_scr[:, 0:FFN_TF] = wa_ref[...].astype(BF16)
        wab_scr[:, FFN_TF:2 * FFN_TF] = wb_ref[...].astype(BF16)
        wo_scr[...] = wo_ref[...].astype(BF16)

    ab = jnp.dot(h_ref[pl.ds(pl.multiple_of(m * tm, tm), tm), :], wab_scr[...],
                 preferred_element_type=F32)
    a = ab[:, 0:FFN_TF]
    y = (a * jax.nn.sigmoid(a) * ab[:, FFN_TF:2 * FFN_TF]).astype(BF16)
    acc_scr[m] += jnp.dot(y, wo_scr[...], preferred_element_type=F32)

    @pl.when(f == pl.num_programs(1) - 1)
    def _():
        row = row0 + lax.div((grp * FFN_GROUP + m) * tm, rows_per_mod)
        y = x_ref[...] + _mod_block(mod_ref, row, M_G2) * acc_scr[m]
        if final_norm:
            y = y * lax.rsqrt(jnp.mean(y * y, axis=-1, keepdims=True) + EPS) * fg_ref[...]
        o_ref[...] = y


def _ffn(x, h, mod, final_g, w_in, w_out, layer, rows_per_mod, row0):
    t = x.shape[0]
    tm = FFN_TM
    assert rows_per_mod % tm == 0
    nf = D_FF // FFN_TF
    io_idx = lambda grp, f, m: (grp * FFN_GROUP + jnp.where(f == nf - 1, m, 0), 0)
    return pl.pallas_call(
        functools.partial(_ffn_kernel, tm=tm, rows_per_mod=rows_per_mod, row0=row0,
                          final_norm=layer == DEPTH - 1),
        grid=(t // (tm * FFN_GROUP), nf, FFN_GROUP),
        in_specs=[
            pl.BlockSpec((tm, D_MODEL), io_idx),
            pl.BlockSpec((FFN_GROUP * tm, D_MODEL), lambda grp, f, m: (grp, 0)),
            _mod_spec(layer),
            pl.BlockSpec((1, D_MODEL), lambda grp, f, m: (0, 0)),
            pl.BlockSpec((None, D_MODEL, FFN_TF), lambda grp, f, m: (layer, 0, f)),
            pl.BlockSpec((None, D_MODEL, FFN_TF), lambda grp, f, m: (layer, 0, f + nf)),
            pl.BlockSpec((None, FFN_TF, D_MODEL), lambda grp, f, m: (layer, f, 0)),
        ],
        out_specs=pl.BlockSpec((tm, D_MODEL), io_idx),
        out_shape=jax.ShapeDtypeStruct((t, D_MODEL), F32),
        scratch_shapes=[pltpu.VMEM((FFN_GROUP, tm, D_MODEL), F32),
                        pltpu.VMEM((D_MODEL, 2 * FFN_TF), BF16),
                        pltpu.VMEM((FFN_TF, D_MODEL), BF16)],
        compiler_params=_cparams(("arbitrary", "arbitrary", "arbitrary")),
        name="ffn",
    )(x, h, mod, final_g, w_in, w_in, w_out)


def _rope_tables():
    rows = DEC_SEQ // GRID_W
    row = jnp.repeat(jnp.arange(rows), GRID_W).astype(F32)
    col = jnp.tile(jnp.arange(GRID_W), rows).astype(F32)
    n = HEAD_DIM // 4
    inv = ROPE_THETA ** (-jnp.arange(n, dtype=F32) / n)
    ang = jnp.concatenate([row[:, None] * inv, col[:, None] * inv], axis=-1)
    cos, sin = jnp.cos(ang), jnp.sin(ang)
    cos_t = jnp.tile(cos, (1, 4))
    sin_t = jnp.tile(jnp.concatenate([-sin, sin], axis=-1), (1, 2))
    return cos_t, sin_t


def _pack_small(a_lam_q1, a_lam_k1, a_lam_q2, a_lam_k2, a_subln_g, b_qnorm_g, b_knorm_g):
    pad = lambda v: jnp.pad(v, ((0, 0), (0, LANES - HEAD_DIM)))
    rows = [pad(a_lam_q1), pad(a_lam_k1), pad(a_lam_q2), pad(a_lam_k2), a_subln_g,
            jnp.tile(b_qnorm_g, (1, 2)), jnp.tile(b_knorm_g, (1, 2)),
            jnp.zeros((DEPTH, LANES), F32)]
    return jnp.stack(rows, axis=1)


def kernel(x_prompt, x_sample, cache_a_k, cache_a_v, cache_b_k, cache_b_v, cache_c_k, cache_c_v, c, c_ctx, w_mod, b_mod, norm1_g, norm2_g, w_in, a_lam_q1, a_lam_k1, a_lam_q2, a_lam_k2, a_subln_g, b_qnorm_g, b_knorm_g, c_sink, w_br_a, w_br_b, w_br_c, w_out, w_ffn_in, w_ffn_out, final_g):
    t_ctx = BATCH * SEQ
    t_lat = DEC_BATCH * DEC_SEQ
    xp = x_prompt.reshape(t_ctx, D_MODEL)
    xs = x_sample.reshape(t_lat, D_MODEL)
    cv8 = jnp.concatenate([c_ctx[None, :], c, jnp.zeros((8 - 1 - DEC_BATCH, D_MODEL), F32)], axis=0)
    mod = _modulation(cv8, w_mod, b_mod)
    cos_t, sin_t = _rope_tables()
    sp = _pack_small(a_lam_q1, a_lam_k1, a_lam_q2, a_lam_k2, a_subln_g, b_qnorm_g, b_knorm_g)
    n1 = norm1_g.reshape(DEPTH, 1, D_MODEL)
    n2 = norm2_g.reshape(DEPTH, 1, D_MODEL)
    caches = (cache_a_k.reshape(DEC_BATCH, DEPTH, PAST_LEN * A_HEADS, LANES),
              cache_a_v.reshape(DEC_BATCH, DEPTH, PAST_LEN * A_HEADS, LANES),
              cache_b_k.reshape(DEC_BATCH, DEPTH, PAST_LEN, LANES),
              cache_b_v.reshape(DEC_BATCH, DEPTH, PAST_LEN, LANES),
              cache_c_k.reshape(DEC_BATCH, DEPTH, PAST_LEN, LANES),
              cache_c_v.reshape(DEC_BATCH, DEPTH, PAST_LEN, LANES))
    fg = final_g.reshape(1, D_MODEL)
    cache_out = None
    for l in range(DEPTH):
        lam_init = 0.8 - 0.6 * math.exp(-0.3 * l)

        qkv_c, h1_c = _proj(xp, mod, n1, w_in, l, t_ctx, 0)
        att_c, cache_out = _att_ctx(qkv_c, sp, c_sink, cache_out, l, lam_init)
        xp, hp = _post(att_c, h1_c, xp, mod, n2, w_in, w_br_a, w_br_b, w_br_c, w_out, l, t_ctx, 0)
        xp = _ffn(xp, hp, mod, fg, w_ffn_in, w_ffn_out, l, t_ctx, 0)

        qkv_s, h1_s = _proj(xs, mod, n1, w_in, l, DEC_SEQ, 1)
        att_s = _att_lat(qkv_s, caches, cos_t, sin_t, sp, c_sink, l, lam_init)
        xs, hs = _post(att_s, h1_s, xs, mod, n2, w_in, w_br_a, w_br_b, w_br_c, w_out, l, DEC_SEQ, 1)
        xs = _ffn(xs, hs, mod, fg, w_ffn_in, w_ffn_out, l, DEC_SEQ, 1)

    y_prompt = xp.reshape(BATCH, SEQ, D_MODEL)
    y_sample = xs.reshape(DEC_BATCH, DEC_SEQ, D_MODEL)
    ka_all, va_all, small_all = cache_out
    wide = tuple(a.reshape(BATCH, DEPTH, SEQ, A_HEADS, 2 * HEAD_DIM) for a in (ka_all, va_all))
    small = tuple(small_all[..., c:c + LANES].reshape(BATCH, DEPTH, SEQ, 2, HEAD_DIM)
                  for c in (SMALL_KB, SMALL_VB, SMALL_KC, SMALL_VC))
    return (y_prompt, y_sample) + wide + small
```

```python
import functools
import math

import jax
import jax.numpy as jnp
from jax import lax
from jax.experimental import pallas as pl
from jax.experimental.pallas import tpu as pltpu

D_MODEL = 1024
BATCH = 16
SEQ = 256
DEPTH = 4
DEC_BATCH = 4
DEC_SEQ = 1024
PAST_LEN = 512
GRID_W = 64
HEAD_DIM = 64
ROPE_THETA = 10000.0
EPS = 1e-6
NEG_INF = -1e30
A_HEADS = 4
WINDOW = 128
D_FF = -(-8 * D_MODEL // (3 * 256)) * 256
N_MOD = 6
N_BRANCH = 3
BRANCH = 512
D_ATT = N_BRANCH * BRANCH
D_GATE = N_BRANCH * D_MODEL
LANES = 128
HALF = HEAD_DIM // 2
LOG2E = math.log2(math.e)
QSCALE = HEAD_DIM ** -0.5 * LOG2E

_KV = 2 * HEAD_DIM
QA, KA, VA = 0, BRANCH, 2 * BRANCH
QB, KB, VB = 3 * BRANCH, 4 * BRANCH, 4 * BRANCH + _KV
QC, KC, VC = VB + _KV, VB + _KV + BRANCH, VB + 2 * _KV + BRANCH
D_QKV = VC + _KV
R_LQ1, R_LK1, R_LQ2, R_LK2, R_SUBG, R_BQG, R_BKG = range(7)
M_SH1, M_SC1, M_G1, M_SH2, M_SC2, M_G2 = range(6)

F32 = jnp.float32
BF16 = jnp.bfloat16
VMEM_LIMIT = 56 * 1024 * 1024


def _cparams(sem):
    return pltpu.CompilerParams(dimension_semantics=sem, vmem_limit_bytes=VMEM_LIMIT)


MOD_TN = 1536


def _mod_kernel(cv_ref, w_ref, b_ref, o_ref):
    cv = cv_ref[...]
    s = (cv * jax.nn.sigmoid(cv)).astype(BF16)
    o_ref[...] = jnp.dot(s, w_ref[...].astype(BF16), preferred_element_type=F32) + b_ref[...]


def _modulation(cv8, w_mod, b_mod):
    n = N_MOD * D_MODEL
    return pl.pallas_call(
        _mod_kernel,
        grid=(DEPTH, n // MOD_TN),
        in_specs=[
            pl.BlockSpec((8, D_MODEL), lambda l, j: (0, 0)),
            pl.BlockSpec((None, D_MODEL, MOD_TN), lambda l, j: (l, 0, j)),
            pl.BlockSpec((None, 1, MOD_TN), lambda l, j: (l, 0, j)),
        ],
        out_specs=pl.BlockSpec((None, 8, MOD_TN), lambda l, j: (l, 0, j)),
        out_shape=jax.ShapeDtypeStruct((DEPTH, 8, n), F32),
        compiler_params=_cparams(("arbitrary", "arbitrary")),
        name="modulation",
    )(cv8, w_mod, b_mod.reshape(DEPTH, 1, n))


NORM_CHUNK = 256


def _mod_block(mod_ref, row, blk):
    return mod_ref[pl.ds(row, 1), blk * D_MODEL:(blk + 1) * D_MODEL]


def _norm_mod_rows(x_ref, h_ref, g, mod_ref, sc_blk, sh_blk, rows, mod_row0, rows_per_mod):
    def body(i, carry):
        r = pl.ds(pl.multiple_of(i * NORM_CHUNK, NORM_CHUNK), NORM_CHUNK)
        row = mod_row0 + lax.div(i * NORM_CHUNK, rows_per_mod)
        x = x_ref[r, :]
        y = x * lax.rsqrt(jnp.mean(x * x, axis=-1, keepdims=True) + EPS) * g
        h_ref[r, :] = (y * (1.0 + _mod_block(mod_ref, row, sc_blk))
                       + _mod_block(mod_ref, row, sh_blk)).astype(BF16)
        return carry
    lax.fori_loop(0, rows // NORM_CHUNK, body, 0)


def _head_rmsnorm(x, g, lane_lo):
    x2 = x * x
    zero = jnp.zeros_like(x2)
    lo = jnp.sum(jnp.where(lane_lo, x2, zero), axis=-1, keepdims=True)
    hi = jnp.sum(jnp.where(lane_lo, zero, x2), axis=-1, keepdims=True)
    ms = jnp.where(lane_lo, lo, hi) * (1.0 / HEAD_DIM)
    return x * lax.rsqrt(ms + EPS) * g


def _rope(x, cos, sin, lane):
    partner = jnp.where((lane & HALF) == 0, pltpu.roll(x, LANES - HALF, 1), pltpu.roll(x, HALF, 1))
    return x * cos + partner * sin


ONES_LANE_LO, ONES_LANE_HI = HEAD_DIM, 0


def _lo_hi(x, lane, src_hi, ones_lane=False):
    lane_lo = lane < HEAD_DIM
    other = pltpu.roll(x, HEAD_DIM, 1)
    zero = jnp.zeros_like(x)
    lo = jnp.where(lane_lo, other if src_hi else x, zero)
    hi = jnp.where(lane_lo, zero, x if src_hi else other)
    if ones_lane:
        lo = jnp.where(lane == ONES_LANE_LO, 1.0, lo)
        hi = jnp.where(lane == ONES_LANE_HI, 1.0, hi)
    return lo, hi


def _split_heads(q, lane_lo):
    zero = jnp.zeros_like(q)
    return jnp.concatenate([jnp.where(lane_lo, q, zero), jnp.where(lane_lo, zero, q)], axis=0)


def _dot_nt(a, b):
    return lax.dot_general(a, b, (((1,), (1,)), ((), ())), preferred_element_type=F32)


def _probs(s, extra=None):
    m = jnp.max(s, axis=-1, keepdims=True)
    if extra is not None:
        m = jnp.maximum(m, extra)
    return jnp.exp2(s - m), m


def _lam(sp_ref, lam_init):
    dot1 = jnp.sum(sp_ref[R_LQ1:R_LQ1 + 1, :] * sp_ref[R_LK1:R_LK1 + 1, :], axis=-1, keepdims=True)
    dot2 = jnp.sum(sp_ref[R_LQ2:R_LQ2 + 1, :] * sp_ref[R_LK2:R_LK2 + 1, :], axis=-1, keepdims=True)
    return jnp.exp(dot1) - jnp.exp(dot2) + lam_init


def _subln(o, g, lam_init):
    return o * lax.rsqrt(jnp.mean(o * o, axis=-1, keepdims=True) + EPS) * g * (1.0 - lam_init)


def _mod_spec(layer):
    return pl.BlockSpec((None, 8, N_MOD * D_MODEL), lambda *_: (layer, 0, 0))


PROJ_TM = 2048
PROJ_TN = 512


def _proj_kernel(x_ref, mod_ref, g_ref, w_ref, qkv_ref, hout_ref, h_scr, w_scr, *, tm, rows_per_mod, row0):
    n = pl.program_id(0)
    m = pl.program_id(1)

    @pl.when(n == 0)
    def _():
        _norm_mod_rows(x_ref, h_scr.at[m], g_ref[...], mod_ref, M_SC1, M_SH1, tm,
                       row0 + lax.div(m * tm, rows_per_mod), rows_per_mod)
        hout_ref[...] = h_scr[m]

    @pl.when(m == 0)
    def _():
        w_scr[...] = w_ref[...].astype(BF16)

    qkv_ref[...] = jnp.dot(h_scr[m], w_scr[...], preferred_element_type=F32)


def _proj(x, mod, g, w_in, layer, rows_per_mod, row0):
    t = x.shape[0]
    tm = PROJ_TM
    nm = t // tm
    first_pass = lambda n, m: (jnp.where(n == 0, m, nm - 1), 0)
    return pl.pallas_call(
        functools.partial(_proj_kernel, tm=tm, rows_per_mod=rows_per_mod, row0=row0),
        grid=(D_QKV // PROJ_TN, nm),
        in_specs=[
            pl.BlockSpec((tm, D_MODEL), first_pass),
            _mod_spec(layer),
            pl.BlockSpec((None, 1, D_MODEL), lambda n, m: (layer, 0, 0)),
            pl.BlockSpec((None, D_MODEL, PROJ_TN), lambda n, m: (layer, 0, n)),
        ],
        out_specs=[pl.BlockSpec((tm, PROJ_TN), lambda n, m: (m, n)),
                   pl.BlockSpec((tm, D_MODEL), first_pass)],
        out_shape=[jax.ShapeDtypeStruct((t, D_QKV), F32),
                   jax.ShapeDtypeStruct((t, D_MODEL), BF16)],
        scratch_shapes=[pltpu.VMEM((nm, tm, D_MODEL), BF16), pltpu.VMEM((D_MODEL, PROJ_TN), BF16)],
        compiler_params=_cparams(("arbitrary", "arbitrary")),
        name="proj",
    )(x, mod, g, w_in)


CTX_NB = 1
SMALL_KB, SMALL_VB, SMALL_KC, SMALL_VC = (i * LANES for i in range(4))


def _att_ctx_kernel(qkv_ref, sp_ref, sink_ref, ka_in, va_in, small_in,
                    att_ref, ka_ref, va_ref, small_ref, *, layer, lam_init):
    del ka_in, va_in, small_in
    lane = lax.broadcasted_iota(jnp.int32, (SEQ, LANES), 1)
    lane_lo = lane < HEAD_DIM
    lam = _lam(sp_ref, lam_init)
    subg = sp_ref[R_SUBG:R_SUBG + 1, :]
    bqg = sp_ref[R_BQG:R_BQG + 1, :]
    bkg = sp_ref[R_BKG:R_BKG + 1, :]

    for i in range(CTX_NB):
        rows = slice(i * SEQ, (i + 1) * SEQ)
        tile = lambda c: qkv_ref[rows, c:c + LANES]

        for h in range(A_HEADS):
            q = _split_heads(tile(QA + h * LANES) * QSCALE, lane_lo).astype(BF16)
            k = tile(KA + h * LANES)
            v = tile(VA + h * LANES)
            ka_ref[i, pl.ds(h, SEQ, stride=A_HEADS), :] = k
            va_ref[i, pl.ds(h, SEQ, stride=A_HEADS), :] = v
            v = v.astype(BF16)
            s = _dot_nt(q, k.astype(BF16))
            p1, _ = _probs(s[:SEQ])
            p2, _ = _probs(s[SEQ:])
            o1 = jnp.dot(p1.astype(BF16), v, preferred_element_type=F32) / jnp.sum(p1, axis=-1, keepdims=True)
            o2 = jnp.dot(p2.astype(BF16), v, preferred_element_type=F32) / jnp.sum(p2, axis=-1, keepdims=True)
            o = _subln(o1 - lam * o2, subg, lam_init)
            att_ref[rows, h * LANES:(h + 1) * LANES] = o.astype(BF16)

        kb = _head_rmsnorm(tile(KB), bkg, lane_lo)
        vb, kc, vc = tile(VB), tile(KC), tile(VC)
        for x, c in ((kb, SMALL_KB), (vb, SMALL_VB), (kc, SMALL_KC), (vc, SMALL_VC)):
            small_ref[i, :, c:c + LANES] = x
        for mixer, (q0, k_t, v_t, o0) in enumerate(((QB, kb, vb, BRANCH), (QC, kc, vc, 2 * BRANCH))):
            for g in range(2):
                k_lo, k_hi = _lo_hi(k_t, lane, g == 1)
                v_lo, v_hi = _lo_hi(v_t, lane, g == 1, ones_lane=True)
                k_both = (k_lo + k_hi).astype(BF16)
                v_lo = v_lo.astype(BF16)
                v_hi = v_hi.astype(BF16)
                for jj in range(2):
                    j = 2 * g + jj
                    q = tile(q0 + j * LANES)
                    if mixer == 0:
                        q = _head_rmsnorm(q, bqg, lane_lo)
                    s = _dot_nt(_split_heads(q * QSCALE, lane_lo).astype(BF16), k_both)
                    if mixer == 0:
                        pe, _ = _probs(s[:SEQ])
                        po, _ = _probs(s[SEQ:])
                    else:
                        sink_e = sink_ref[layer, 2 * j] * LOG2E
                        sink_o = sink_ref[layer, 2 * j + 1] * LOG2E
                        pe, me = _probs(s[:SEQ], sink_e)
                        po, mo = _probs(s[SEQ:], sink_o)
                    oe = jnp.dot(pe.astype(BF16), v_lo, preferred_element_type=F32)
                    oo = jnp.dot(po.astype(BF16), v_hi, preferred_element_type=F32)
                    le = oe[:, ONES_LANE_LO:ONES_LANE_LO + 1]
                    lo_ = oo[:, ONES_LANE_HI:ONES_LANE_HI + 1]
                    if mixer == 1:
                        le = le + jnp.exp2(sink_e - me)
                        lo_ = lo_ + jnp.exp2(sink_o - mo)
                    o = jnp.where(lane_lo, oe, oo) / jnp.where(lane_lo, le, lo_)
                    att_ref[rows, o0 + j * LANES:o0 + (j + 1) * LANES] = o.astype(BF16)


def _att_ctx(qkv, sp, sink, cache_out, layer, lam_init):
    t = qkv.shape[0]
    nb = t // SEQ
    rows = CTX_NB * SEQ
    wide = pl.BlockSpec((CTX_NB, None, SEQ * A_HEADS, LANES), lambda b: (b, layer, 0, 0))
    small = pl.BlockSpec((CTX_NB, None, SEQ, 4 * LANES), lambda b: (b, layer, 0, 0))
    wide_shape = jax.ShapeDtypeStruct((nb, DEPTH, SEQ * A_HEADS, LANES), F32)
    small_shape = jax.ShapeDtypeStruct((nb, DEPTH, SEQ, 4 * LANES), F32)
    if cache_out is None:
        cache_out = (jnp.zeros(wide_shape.shape, F32), jnp.zeros(wide_shape.shape, F32),
                     jnp.zeros(small_shape.shape, F32))
    passthrough = pl.BlockSpec(memory_space=pl.ANY)
    att, *cache_out = pl.pallas_call(
        functools.partial(_att_ctx_kernel, layer=layer, lam_init=lam_init),
        grid=(nb // CTX_NB,),
        in_specs=[
            pl.BlockSpec((rows, D_QKV), lambda b: (b, 0)),
            pl.BlockSpec((None, 8, LANES), lambda b: (layer, 0, 0)),
            pl.BlockSpec(memory_space=pltpu.SMEM),
            passthrough, passthrough, passthrough,
        ],
        out_specs=[pl.BlockSpec((rows, D_ATT), lambda b: (b, 0)), wide, wide, small],
        out_shape=[jax.ShapeDtypeStruct((t, D_ATT), BF16), wide_shape, wide_shape, small_shape],
        input_output_aliases={3: 1, 4: 2, 5: 3},
        compiler_params=_cparams(("arbitrary",)),
        name="att_ctx",
    )(qkv, sp, sink, *cache_out)
    return att, cache_out


LAT_TQ_NORM = 256
NK = PAST_LEN + DEC_SEQ
KCH = 512
KV_BLK = 512
assert KA % KV_BLK == 0 and VA % KV_BLK == 0 and KB // KV_BLK == VB // KV_BLK and KC // KV_BLK == VC // KV_BLK
PREP_ROWS = 512
N_UNITS = 12
UNITS_PER_STEP = 2
A_STEPS = A_HEADS // UNITS_PER_STEP


def _attend_all_queries(q, k_chunks, v_chunks, masks=None, sink=None, l_lane=None):
    m = l = acc = None
    for c, (k_c, v_c) in enumerate(zip(k_chunks, v_chunks)):
        s = _dot_nt(q, k_c)
        if masks is not None and masks[c] is not None:
            s = jnp.where(masks[c], s, NEG_INF)
        mc = jnp.max(s, axis=-1, keepdims=True)
        if m is None:
            m_new = mc if sink is None else jnp.maximum(mc, sink)
            p = jnp.exp2(s - m_new)
            acc = jnp.dot(p.astype(BF16), v_c, preferred_element_type=F32)
            if l_lane is None:
                l = jnp.sum(p, axis=-1, keepdims=True)
        else:
            m_new = jnp.maximum(m, mc)
            alpha = jnp.exp2(m - m_new)
            p = jnp.exp2(s - m_new)
            acc = alpha * acc + jnp.dot(p.astype(BF16), v_c, preferred_element_type=F32)
            if l_lane is None:
                l = alpha * l + jnp.sum(p, axis=-1, keepdims=True)
        m = m_new
    if l_lane is not None:
        l = acc[:, l_lane:l_lane + 1]
    if sink is not None:
        l = l + jnp.exp2(sink - m)
    return acc, l


def _att_lat_kernel(q_ref, kva_ref, kvv_ref, kvb_ref, kvc_ref,
                    cak_ref, cav_ref, cbk_ref, cbv_ref, cck_ref, ccv_ref,
                    cos_ref, sin_ref, sp_ref, sink_ref,
                    att_ref,
                    ka_scr, va_scr, kb_scr, vb_scr, kc_scr, vc_scr, q_scr,
                    *, layer, lam_init):
    step_id = pl.program_id(1)
    subg = sp_ref[R_SUBG:R_SUBG + 1, :]
    bqg = sp_ref[R_BQG:R_BQG + 1, :]
    bkg = sp_ref[R_BKG:R_BKG + 1, :]

    @pl.when(step_id == 0)
    def _prepare_keys():
        lane = lax.broadcasted_iota(jnp.int32, (PREP_ROWS, LANES), 1)
        lane_lo = lane < HEAD_DIM
        def put(dst, a_k, a_v, pairs):
            lo_rows = pl.ds(pl.multiple_of(dst, PREP_ROWS), PREP_ROWS)
            hi_rows = pl.ds(pl.multiple_of(NK + dst, PREP_ROWS), PREP_ROWS)
            for h in range(A_HEADS):
                k = a_k[h]
                zero = jnp.zeros_like(k)
                ka_scr[h, lo_rows, :] = jnp.where(lane_lo, k, zero).astype(BF16)
                ka_scr[h, hi_rows, :] = jnp.where(lane_lo, zero, k).astype(BF16)
                va_scr[h, lo_rows, :] = a_v[h].astype(BF16)
            for x, scr, is_value in pairs:
                for g in range(2):
                    lo, hi = _lo_hi(x, lane, g == 1, ones_lane=is_value)
                    scr[g, lo_rows, :] = lo.astype(BF16)
                    scr[g, hi_rows, :] = hi.astype(BF16)

        def cached(i, carry):
            r = pl.ds(pl.multiple_of(i * PREP_ROWS, PREP_ROWS), PREP_ROWS)
            head_rows = lambda h: pl.ds(i * (PREP_ROWS * A_HEADS) + h, PREP_ROWS, stride=A_HEADS)
            put(i * PREP_ROWS,
                [cak_ref[head_rows(h), :] for h in range(A_HEADS)],
                [cav_ref[head_rows(h), :] for h in range(A_HEADS)],
                ((cbk_ref[r, :], kb_scr, False), (cbv_ref[r, :], vb_scr, True),
                 (cck_ref[r, :], kc_scr, False), (ccv_ref[r, :], vc_scr, True)))
            return carry
        lax.fori_loop(0, PAST_LEN // PREP_ROWS, cached, 0)

        def latent(i, carry):
            r = pl.ds(pl.multiple_of(i * PREP_ROWS, PREP_ROWS), PREP_ROWS)
            cos = cos_ref[r, :]
            sin = sin_ref[r, :]
            in_blk = lambda ref, col: ref[r, col % KV_BLK:col % KV_BLK + LANES]
            kb = _rope(_head_rmsnorm(in_blk(kvb_ref, KB), bkg, lane_lo), cos, sin, lane)
            kc = _rope(in_blk(kvc_ref, KC), cos, sin, lane)
            put(PAST_LEN + i * PREP_ROWS,
                [_rope(kva_ref[r, h * LANES:(h + 1) * LANES], cos, sin, lane) for h in range(A_HEADS)],
                [kvv_ref[r, h * LANES:(h + 1) * LANES] for h in range(A_HEADS)],
                ((kb, kb_scr, False), (in_blk(kvb_ref, VB), vb_scr, True),
                 (kc, kc_scr, False), (in_blk(kvc_ref, VC), vc_scr, True)))
            return carry
        lax.fori_loop(0, DEC_SEQ // PREP_ROWS, latent, 0)

    lam = _lam(sp_ref, lam_init)
    lane_lo_all = lax.broadcasted_iota(jnp.int32, (DEC_SEQ, LANES), 1) < HEAD_DIM
    chunks = lambda scr, i, base: [scr[i, base + c * KCH:base + (c + 1) * KCH, :] for c in range(NK // KCH)]

    def prepare_queries(normalise):
        tq = LAT_TQ_NORM if normalise else DEC_SEQ
        lane = lax.broadcasted_iota(jnp.int32, (tq, LANES), 1)
        lane_lo = lane < HEAD_DIM

        def step(i, carry):
            r = pl.ds(pl.multiple_of(i * tq, tq), tq)
            for n in range(UNITS_PER_STEP):
                q = q_ref[r, n * LANES:(n + 1) * LANES]
                if normalise:
                    q = _head_rmsnorm(q, bqg, lane_lo)
                q_scr[n, r, :] = (_rope(q, cos_ref[r, :], sin_ref[r, :], lane) * QSCALE).astype(BF16)
            return carry
        lax.fori_loop(0, DEC_SEQ // tq, step, 0)

    @pl.when(step_id < A_STEPS)
    def _mixer_a():
        prepare_queries(False)
        for n in range(UNITS_PER_STEP):
            h = step_id * UNITS_PER_STEP + n
            q = q_scr[n]
            v = chunks(va_scr, h, 0)
            a1, l1 = _attend_all_queries(q, chunks(ka_scr, h, 0), v)
            a2, l2 = _attend_all_queries(q, chunks(ka_scr, h, NK), v)
            att_ref[:, n * LANES:(n + 1) * LANES] = _subln(a1 / l1 - lam * (a2 / l2), subg,
                                                          lam_init).astype(BF16)

    def pair_unit(n, k_scr, v_scr, g, masks=None, sinks=(None, None)):
        q = q_scr[n]
        ae, le = _attend_all_queries(q, chunks(k_scr, g, 0), chunks(v_scr, g, 0),
                                     masks=masks, sink=sinks[0], l_lane=ONES_LANE_LO)
        ao, lo_ = _attend_all_queries(q, chunks(k_scr, g, NK), chunks(v_scr, g, NK),
                                      masks=masks, sink=sinks[1], l_lane=ONES_LANE_HI)
        o = jnp.where(lane_lo_all, ae, ao) / jnp.where(lane_lo_all, le, lo_)
        att_ref[:, n * LANES:(n + 1) * LANES] = o.astype(BF16)

    @pl.when((step_id >= A_STEPS) & (step_id < 2 * A_STEPS))
    def _mixer_b():
        prepare_queries(True)
        for n in range(UNITS_PER_STEP):
            unit = (step_id - A_STEPS) * UNITS_PER_STEP + n
            pair_unit(n, kb_scr, vb_scr, lax.shift_right_logical(unit, 1))

    @pl.when(step_id >= 2 * A_STEPS)
    def _mixer_c():
        prepare_queries(False)
        qpos = lax.broadcasted_iota(jnp.int32, (DEC_SEQ, KCH), 0)
        kcol = lax.broadcasted_iota(jnp.int32, (DEC_SEQ, KCH), 1)
        masks = (None,) * (PAST_LEN // KCH) + tuple(
            jnp.abs(kcol + k0 - qpos) <= WINDOW for k0 in range(0, DEC_SEQ, KCH))
        for n in range(UNITS_PER_STEP):
            unit = (step_id - 2 * A_STEPS) * UNITS_PER_STEP + n
            pair_unit(n, kc_scr, vc_scr, lax.shift_right_logical(unit, 1), masks=masks,
                      sinks=(sink_ref[layer, 2 * unit] * LOG2E, sink_ref[layer, 2 * unit + 1] * LOG2E))


def _step_q_col(s):
    width = UNITS_PER_STEP * LANES
    return jnp.where(s < A_STEPS, s, jnp.where(s < 2 * A_STEPS, QB // width - A_STEPS + s,
                                               QC // width - 2 * A_STEPS + s))


def _att_lat(qkv, caches, cos, sin, sp, sink, layer, lam_init):
    t = qkv.shape[0]
    kv_blk = lambda col: pl.BlockSpec((DEC_SEQ, KV_BLK), lambda b, i: (b, col // KV_BLK))
    cache_blk = lambda r: pl.BlockSpec((None, None, r, LANES), lambda b, i: (b, layer, 0, 0))
    return pl.pallas_call(
        functools.partial(_att_lat_kernel, layer=layer, lam_init=lam_init),
        grid=(DEC_BATCH, N_UNITS // UNITS_PER_STEP),
        in_specs=[
            pl.BlockSpec((DEC_SEQ, UNITS_PER_STEP * LANES), lambda b, i: (b, _step_q_col(i))),
            kv_blk(KA), kv_blk(VA), kv_blk(KB), kv_blk(KC),
            cache_blk(PAST_LEN * A_HEADS), cache_blk(PAST_LEN * A_HEADS),
            cache_blk(PAST_LEN), cache_blk(PAST_LEN), cache_blk(PAST_LEN), cache_blk(PAST_LEN),
            pl.BlockSpec((DEC_SEQ, LANES), lambda b, i: (0, 0)),
            pl.BlockSpec((DEC_SEQ, LANES), lambda b, i: (0, 0)),
            pl.BlockSpec((None, 8, LANES), lambda b, i: (layer, 0, 0)),
            pl.BlockSpec(memory_space=pltpu.SMEM),
        ],
        out_specs=pl.BlockSpec((DEC_SEQ, UNITS_PER_STEP * LANES), lambda b, i: (b, i)),
        out_shape=jax.ShapeDtypeStruct((t, D_ATT), BF16),
        scratch_shapes=[
            pltpu.VMEM((A_HEADS, 2 * NK, LANES), BF16),
            pltpu.VMEM((A_HEADS, NK, LANES), BF16),
            pltpu.VMEM((2, 2 * NK, LANES), BF16),
            pltpu.VMEM((2, 2 * NK, LANES), BF16),
            pltpu.VMEM((2, 2 * NK, LANES), BF16),
            pltpu.VMEM((2, 2 * NK, LANES), BF16),
            pltpu.VMEM((UNITS_PER_STEP, DEC_SEQ, LANES), BF16),
        ],
        compiler_params=_cparams(("arbitrary", "arbitrary")),
        name="att_lat",
    )(qkv, qkv, qkv, qkv, qkv, *caches, cos, sin, sp, sink)


POST_TM = 512


def _post_kernel(att_ref, h1_ref, x_ref, mod_ref, g2_ref, wg_ref, wa_ref, wb_ref, wc_ref, wo_ref,
                 o_ref, h_ref, wg_scr, wbr_scr, wo_scr, *, tm, rows_per_mod, row0):
    @pl.when(pl.program_id(0) == 0)
    def _():
        wg_scr[...] = wg_ref[...].astype(BF16)
        for i, w_ref in enumerate((wa_ref, wb_ref, wc_ref)):
            wbr_scr[i] = w_ref[...].astype(BF16)
        wo_scr[...] = wo_ref[...].astype(BF16)

    h1 = h1_ref[...]
    merged = None
    for i in range(N_BRANCH):
        y = jnp.dot(att_ref[:, i * BRANCH:(i + 1) * BRANCH], wbr_scr[i], preferred_element_type=F32)
        gate = jnp.dot(h1, wg_scr[:, i * D_MODEL:(i + 1) * D_MODEL], preferred_element_type=F32)
        term = jax.nn.sigmoid(gate) * y
        merged = term if merged is None else merged + term
    mixed = jnp.dot(merged.astype(BF16), wo_scr[...], preferred_element_type=F32)
    row = row0 + lax.div(pl.program_id(0) * tm, rows_per_mod)
    x1 = x_ref[...] + _mod_block(mod_ref, row, M_G1) * mixed
    o_ref[...] = x1
    y = x1 * lax.rsqrt(jnp.mean(x1 * x1, axis=-1, keepdims=True) + EPS) * g2_ref[...]
    h_ref[...] = (y * (1.0 + _mod_block(mod_ref, row, M_SC2)) + _mod_block(mod_ref, row, M_SH2)).astype(BF16)


def _post(att, h1, x, mod, g2, w_in, wa, wb, wc, wo, layer, rows_per_mod, row0):
    t = x.shape[0]
    tm = POST_TM
    once = pl.Buffered(1)
    full = lambda r: pl.BlockSpec((None, r, D_MODEL), lambda m: (layer, 0, 0), pipeline_mode=once)
    return pl.pallas_call(
        functools.partial(_post_kernel, tm=tm, rows_per_mod=rows_per_mod, row0=row0),
        grid=(t // tm,),
        in_specs=[
            pl.BlockSpec((tm, D_ATT), lambda m: (m, 0)),
            pl.BlockSpec((tm, D_MODEL), lambda m: (m, 0)),
            pl.BlockSpec((tm, D_MODEL), lambda m: (m, 0)),
            _mod_spec(layer),
            pl.BlockSpec((None, 1, D_MODEL), lambda m: (layer, 0, 0)),
            pl.BlockSpec((None, D_MODEL, D_GATE), lambda m: (layer, 0, D_QKV // D_GATE), pipeline_mode=once),
            full(BRANCH), full(BRANCH), full(BRANCH), full(D_MODEL),
        ],
        out_specs=[pl.BlockSpec((tm, D_MODEL), lambda m: (m, 0)),
                   pl.BlockSpec((tm, D_MODEL), lambda m: (m, 0))],
        out_shape=[jax.ShapeDtypeStruct((t, D_MODEL), F32),
                   jax.ShapeDtypeStruct((t, D_MODEL), BF16)],
        scratch_shapes=[pltpu.VMEM((D_MODEL, D_GATE), BF16), pltpu.VMEM((N_BRANCH, BRANCH, D_MODEL), BF16),
                        pltpu.VMEM((D_MODEL, D_MODEL), BF16)],
        compiler_params=_cparams(("arbitrary",)),
        name="post",
    )(att, h1, x, mod, g2, w_in, wa, wb, wc, wo)


FFN_TM = 1024
FFN_TF = 256


FFN_GROUP = 2


def _ffn_kernel(x_ref, h_ref, mod_ref, fg_ref, wa_ref, wb_ref, wo_ref, o_ref,
                acc_scr, wab_scr, wo_scr, *, tm, rows_per_mod, row0, final_norm):
    grp = pl.program_id(0)
    f = pl.program_id(1)
    m = pl.program_id(2)

    @pl.when(f == 0)
    def _():
        acc_scr[m] = jnp.zeros((tm, D_MODEL), F32)

    @pl.when(m == 0)
    def _():
        wab_scr[:, 0:FFN_TF] = wa_ref[...].astype(BF16)
        wab_scr[:, FFN_TF:2 * FFN_TF] = wb_ref[...].astype(BF16)
        wo_scr[...] = wo_ref[...].astype(BF16)

    ab = jnp.dot(h_ref[pl.ds(pl.multiple_of(m * tm, tm), tm), :], wab_scr[...],
                 preferred_element_type=F32)
    a = ab[:, 0:FFN_TF]
    y = (a * jax.nn.sigmoid(a) * ab[:, FFN_TF:2 * FFN_TF]).astype(BF16)
    acc_scr[m] += jnp.dot(y, wo_scr[...], preferred_element_type=F32)

    @pl.when(f == pl.num_programs(1) - 1)
    def _():
        row = row0 + lax.div((grp * FFN_GROUP + m) * tm, rows_per_mod)
        y = x_ref[...] + _mod_block(mod_ref, row, M_G2) * acc_scr[m]
        if final_norm:
            y = y * lax.rsqrt(jnp.mean(y * y, axis=-1, keepdims=True) + EPS) * fg_ref[...]
        o_ref[...] = y


def _ffn(x, h, mod, final_g, w_in, w_out, layer, rows_per_mod, row0):
    t = x.shape[0]
    tm = FFN_TM
    assert rows_per_mod % tm == 0
    nf = D_FF // FFN_TF
    io_idx = lambda grp, f, m: (grp * FFN_GROUP + jnp.where(f == nf - 1, m, 0), 0)
    return pl.pallas_call(
        functools.partial(_ffn_kernel, tm=tm, rows_per_mod=rows_per_mod, row0=row0,
                          final_norm=layer == DEPTH - 1),
        grid=(t // (tm * FFN_GROUP), nf, FFN_GROUP),
        in_specs=[
            pl.BlockSpec((tm, D_MODEL), io_idx),
            pl.BlockSpec((FFN_GROUP * tm, D_MODEL), lambda grp, f, m: (grp, 0)),
            _mod_spec(layer),
            pl.BlockSpec((1, D_MODEL), lambda grp, f, m: (0, 0)),
            pl.BlockSpec((None, D_MODEL, FFN_TF), lambda grp, f, m: (layer, 0, f)),
            pl.BlockSpec((None, D_MODEL, FFN_TF), lambda grp, f, m: (layer, 0, f + nf)),
            pl.BlockSpec((None, FFN_TF, D_MODEL), lambda grp, f, m: (layer, f, 0)),
        ],
        out_specs=pl.BlockSpec((tm, D_MODEL), io_idx),
        out_shape=jax.ShapeDtypeStruct((t, D_MODEL), F32),
        scratch_shapes=[pltpu.VMEM((FFN_GROUP, tm, D_MODEL), F32),
                        pltpu.VMEM((D_MODEL, 2 * FFN_TF), BF16),
                        pltpu.VMEM((FFN_TF, D_MODEL), BF16)],
        compiler_params=_cparams(("arbitrary", "arbitrary", "arbitrary")),
        name="ffn",
    )(x, h, mod, final_g, w_in, w_in, w_out)


def _rope_tables():
    rows = DEC_SEQ // GRID_W
    row = jnp.repeat(jnp.arange(rows), GRID_W).astype(F32)
    col = jnp.tile(jnp.arange(GRID_W), rows).astype(F32)
    n = HEAD_DIM // 4
    inv = ROPE_THETA ** (-jnp.arange(n, dtype=F32) / n)
    ang = jnp.concatenate([row[:, None] * inv, col[:, None] * inv], axis=-1)
    cos, sin = jnp.cos(ang), jnp.sin(ang)
    cos_t = jnp.tile(cos, (1, 4))
    sin_t = jnp.tile(jnp.concatenate([-sin, sin], axis=-1), (1, 2))
    return cos_t, sin_t


def _pack_small(a_lam_q1, a_lam_k1, a_lam_q2, a_lam_k2, a_subln_g, b_qnorm_g, b_knorm_g):
    pad = lambda v: jnp.pad(v, ((0, 0), (0, LANES - HEAD_DIM)))
    rows = [pad(a_lam_q1), pad(a_lam_k1), pad(a_lam_q2), pad(a_lam_k2), a_subln_g,
            jnp.tile(b_qnorm_g, (1, 2)), jnp.tile(b_knorm_g, (1, 2)),
            jnp.zeros((DEPTH, LANES), F32)]
    return jnp.stack(rows, axis=1)


def kernel(x_prompt, x_sample, cache_a_k, cache_a_v, cache_b_k, cache_b_v, cache_c_k, cache_c_v, c, c_ctx, w_mod, b_mod, norm1_g, norm2_g, w_in, a_lam_q1, a_lam_k1, a_lam_q2, a_lam_k2, a_subln_g, b_qnorm_g, b_knorm_g, c_sink, w_br_a, w_br_b, w_br_c, w_out, w_ffn_in, w_ffn_out, final_g):
    t_ctx = BATCH * SEQ
    t_lat = DEC_BATCH * DEC_SEQ
    xp = x_prompt.reshape(t_ctx, D_MODEL)
    xs = x_sample.reshape(t_lat, D_MODEL)
    cv8 = jnp.concatenate([c_ctx[None, :], c, jnp.zeros((8 - 1 - DEC_BATCH, D_MODEL), F32)], axis=0)
    mod = _modulation(cv8, w_mod, b_mod)
    cos_t, sin_t = _rope_tables()
    sp = _pack_small(a_lam_q1, a_lam_k1, a_lam_q2, a_lam_k2, a_subln_g, b_qnorm_g, b_knorm_g)
    n1 = norm1_g.reshape(DEPTH, 1, D_MODEL)
    n2 = norm2_g.reshape(DEPTH, 1, D_MODEL)
    caches = (cache_a_k.reshape(DEC_BATCH, DEPTH, PAST_LEN * A_HEADS, LANES),
              cache_a_v.reshape(DEC_BATCH, DEPTH, PAST_LEN * A_HEADS, LANES),
              cache_b_k.reshape(DEC_BATCH, DEPTH, PAST_LEN, LANES),
              cache_b_v.reshape(DEC_BATCH, DEPTH, PAST_LEN, LANES),
              cache_c_k.reshape(DEC_BATCH, DEPTH, PAST_LEN, LANES),
              cache_c_v.reshape(DEC_BATCH, DEPTH, PAST_LEN, LANES))
    fg = final_g.reshape(1, D_MODEL)
    cache_out = None
    for l in range(DEPTH):
        lam_init = 0.8 - 0.6 * math.exp(-0.3 * l)

        qkv_c, h1_c = _proj(xp, mod, n1, w_in, l, t_ctx, 0)
        att_c, cache_out = _att_ctx(qkv_c, sp, c_sink, cache_out, l, lam_init)
        xp, hp = _post(att_c, h1_c, xp, mod, n2, w_in, w_br_a, w_br_b, w_br_c, w_out, l, t_ctx, 0)
        xp = _ffn(xp, hp, mod, fg, w_ffn_in, w_ffn_out, l, t_ctx, 0)

        qkv_s, h1_s = _proj(xs, mod, n1, w_in, l, DEC_SEQ, 1)
        att_s = _att_lat(qkv_s, caches, cos_t, sin_t, sp, c_sink, l, lam_init)
        xs, hs = _post(att_s, h1_s, xs, mod, n2, w_in, w_br_a, w_br_b, w_br_c, w_out, l, DEC_SEQ, 1)
        xs = _ffn(xs, hs, mod, fg, w_ffn_in, w_ffn_out, l, DEC_SEQ, 1)

    y_prompt = xp.reshape(BATCH, SEQ, D_MODEL)
    y_sample = xs.reshape(DEC_BATCH, DEC_SEQ, D_MODEL)
    ka_all, va_all, small_all = cache_out
    wide = tuple(a.reshape(BATCH, DEPTH, SEQ, A_HEADS, 2 * HEAD_DIM) for a in (ka_all, va_all))
    small = tuple(small_all[..., c:c + LANES].reshape(BATCH, DEPTH, SEQ, 2, HEAD_DIM)
                  for c in (SMALL_KB, SMALL_VB, SMALL_KC, SMALL_VC))
    return (y_prompt, y_sample) + wide + small
```

```python
import functools
import math

import jax
import jax.numpy as jnp
from jax import lax
from jax.experimental import pallas as pl
from jax.experimental.pallas import tpu as pltpu

D_MODEL = 1024
BATCH = 16
SEQ = 256
DEPTH = 4
DEC_BATCH = 4
DEC_SEQ = 1024
PAST_LEN = 512
GRID_W = 64
HEAD_DIM = 64
ROPE_THETA = 10000.0
EPS = 1e-6
NEG_INF = -1e30
A_HEADS = 4
WINDOW = 128
D_FF = -(-8 * D_MODEL // (3 * 256)) * 256
N_MOD = 6
N_BRANCH = 3
BRANCH = 512
D_ATT = N_BRANCH * BRANCH
D_GATE = N_BRANCH * D_MODEL
LANES = 128
HALF = HEAD_DIM // 2
LOG2E = math.log2(math.e)
QSCALE = HEAD_DIM ** -0.5 * LOG2E

_KV = 2 * HEAD_DIM
QA, KA, VA = 0, BRANCH, 2 * BRANCH
QB, KB, VB = 3 * BRANCH, 4 * BRANCH, 4 * BRANCH + _KV
QC, KC, VC = VB + _KV, VB + _KV + BRANCH, VB + 2 * _KV + BRANCH
D_QKV = VC + _KV
R_LQ1, R_LK1, R_LQ2, R_LK2, R_SUBG, R_BQG, R_BKG = range(7)
M_SH1, M_SC1, M_G1, M_SH2, M_SC2, M_G2 = range(6)

F32 = jnp.float32
BF16 = jnp.bfloat16
VMEM_LIMIT = 56 * 1024 * 1024


def _cparams(sem):
    return pltpu.CompilerParams(dimension_semantics=sem, vmem_limit_bytes=VMEM_LIMIT)


MOD_TN = 1536


def _mod_kernel(cv_ref, w_ref, b_ref, o_ref):
    cv = cv_ref[...]
    s = (cv * jax.nn.sigmoid(cv)).astype(BF16)
    o_ref[...] = jnp.dot(s, w_ref[...].astype(BF16), preferred_element_type=F32) + b_ref[...]


def _modulation(cv8, w_mod, b_mod):
    n = N_MOD * D_MODEL
    return pl.pallas_call(
        _mod_kernel,
        grid=(DEPTH, n // MOD_TN),
        in_specs=[
            pl.BlockSpec((8, D_MODEL), lambda l, j: (0, 0)),
            pl.BlockSpec((None, D_MODEL, MOD_TN), lambda l, j: (l, 0, j)),
            pl.BlockSpec((None, 1, MOD_TN), lambda l, j: (l, 0, j)),
        ],
        out_specs=pl.BlockSpec((None, 8, MOD_TN), lambda l, j: (l, 0, j)),
        out_shape=jax.ShapeDtypeStruct((DEPTH, 8, n), F32),
        compiler_params=_cparams(("arbitrary", "arbitrary")),
        name="modulation",
    )(cv8, w_mod, b_mod.reshape(DEPTH, 1, n))


NORM_CHUNK = 256


def _mod_block(mod_ref, row, blk):
    return mod_ref[pl.ds(row, 1), blk * D_MODEL:(blk + 1) * D_MODEL]


def _norm_mod_rows(x_ref, h_ref, g, mod_ref, sc_blk, sh_blk, rows, mod_row0, rows_per_mod):
    def body(i, carry):
        r = pl.ds(pl.multiple_of(i * NORM_CHUNK, NORM_CHUNK), NORM_CHUNK)
        row = mod_row0 + lax.div(i * NORM_CHUNK, rows_per_mod)
        x = x_ref[r, :]
        y = x * lax.rsqrt(jnp.mean(x * x, axis=-1, keepdims=True) + EPS) * g
        h_ref[r, :] = (y * (1.0 + _mod_block(mod_ref, row, sc_blk))
                       + _mod_block(mod_ref, row, sh_blk)).astype(BF16)
        return carry
    lax.fori_loop(0, rows // NORM_CHUNK, body, 0)


def _head_rmsnorm(x, g, lane_lo):
    x2 = x * x
    zero = jnp.zeros_like(x2)
    lo = jnp.sum(jnp.where(lane_lo, x2, zero), axis=-1, keepdims=True)
    hi = jnp.sum(jnp.where(lane_lo, zero, x2), axis=-1, keepdims=True)
    ms = jnp.where(lane_lo, lo, hi) * (1.0 / HEAD_DIM)
    return x * lax.rsqrt(ms + EPS) * g


def _rope(x, cos, sin, lane):
    partner = jnp.where((lane & HALF) == 0, pltpu.roll(x, LANES - HALF, 1), pltpu.roll(x, HALF, 1))
    return x * cos + partner * sin


ONES_LANE_LO, ONES_LANE_HI = HEAD_DIM, 0


def _lo_hi(x, lane, src_hi, ones_lane=False):
    lane_lo = lane < HEAD_DIM
    other = pltpu.roll(x, HEAD_DIM, 1)
    zero = jnp.zeros_like(x)
    lo = jnp.where(lane_lo, other if src_hi else x, zero)
    hi = jnp.where(lane_lo, zero, x if src_hi else other)
    if ones_lane:
        lo = jnp.where(lane == ONES_LANE_LO, 1.0, lo)
        hi = jnp.where(lane == ONES_LANE_HI, 1.0, hi)
    return lo, hi


def _split_heads(q, lane_lo):
    zero = jnp.zeros_like(q)
    return jnp.concatenate([jnp.where(lane_lo, q, zero), jnp.where(lane_lo, zero, q)], axis=0)


def _dot_nt(a, b):
    return lax.dot_general(a, b, (((1,), (1,)), ((), ())), preferred_element_type=F32)


def _probs(s, extra=None):
    m = jnp.max(s, axis=-1, keepdims=True)
    if extra is not None:
        m = jnp.maximum(m, extra)
    return jnp.exp2(s - m), m


def _lam(sp_ref, lam_init):
    dot1 = jnp.sum(sp_ref[R_LQ1:R_LQ1 + 1, :] * sp_ref[R_LK1:R_LK1 + 1, :], axis=-1, keepdims=True)
    dot2 = jnp.sum(sp_ref[R_LQ2:R_LQ2 + 1, :] * sp_ref[R_LK2:R_LK2 + 1, :], axis=-1, keepdims=True)
    return jnp.exp(dot1) - jnp.exp(dot2) + lam_init


def _subln(o, g, lam_init):
    return o * lax.rsqrt(jnp.mean(o * o, axis=-1, keepdims=True) + EPS) * g * (1.0 - lam_init)


def _mod_spec(layer):
    return pl.BlockSpec((None, 8, N_MOD * D_MODEL), lambda *_: (layer, 0, 0))


PROJ_TM = 2048
PROJ_TN = 512


def _proj_kernel(x_ref, mod_ref, g_ref, w_ref, qkv_ref, hout_ref, h_scr, w_scr, *, tm, rows_per_mod, row0):
    n = pl.program_id(0)
    m = pl.program_id(1)

    @pl.when(n == 0)
    def _():
        _norm_mod_rows(x_ref, h_scr.at[m], g_ref[...], mod_ref, M_SC1, M_SH1, tm,
                       row0 + lax.div(m * tm, rows_per_mod), rows_per_mod)
        hout_ref[...] = h_scr[m]

    @pl.when(m == 0)
    def _():
        w_scr[...] = w_ref[...].astype(BF16)

    qkv_ref[...] = jnp.dot(h_scr[m], w_scr[...], preferred_element_type=F32)


def _proj(x, mod, g, w_in, layer, rows_per_mod, row0):
    t = x.shape[0]
    tm = PROJ_TM
    nm = t // tm
    first_pass = lambda n, m: (jnp.where(n == 0, m, nm - 1), 0)
    return pl.pallas_call(
        functools.partial(_proj_kernel, tm=tm, rows_per_mod=rows_per_mod, row0=row0),
        grid=(D_QKV // PROJ_TN, nm),
        in_specs=[
            pl.BlockSpec((tm, D_MODEL), first_pass),
            _mod_spec(layer),
            pl.BlockSpec((None, 1, D_MODEL), lambda n, m: (layer, 0, 0)),
            pl.BlockSpec((None, D_MODEL, PROJ_TN), lambda n, m: (layer, 0, n)),
        ],
        out_specs=[pl.BlockSpec((tm, PROJ_TN), lambda n, m: (m, n)),
                   pl.BlockSpec((tm, D_MODEL), first_pass)],
        out_shape=[jax.ShapeDtypeStruct((t, D_QKV), F32),
                   jax.ShapeDtypeStruct((t, D_MODEL), BF16)],
        scratch_shapes=[pltpu.VMEM((nm, tm, D_MODEL), BF16), pltpu.VMEM((D_MODEL, PROJ_TN), BF16)],
        compiler_params=_cparams(("arbitrary", "arbitrary")),
        name="proj",
    )(x, mod, g, w_in)


CTX_NB = 1
SMALL_KB, SMALL_VB, SMALL_KC, SMALL_VC = (i * LANES for i in range(4))


def _att_ctx_kernel(qkv_ref, sp_ref, sink_ref, *refs, layer, lam_init):
    att_ref, ka_ref, va_ref, small_ref = refs[-4:]
    creates = layer == 0
    if creates:
        for ref in (ka_ref, va_ref, small_ref):
            ref[:, 1:] = jnp.zeros((CTX_NB, DEPTH - 1) + ref.shape[2:], F32)
    lane = lax.broadcasted_iota(jnp.int32, (SEQ, LANES), 1)
    lane_lo = lane < HEAD_DIM
    lam = _lam(sp_ref, lam_init)
    subg = sp_ref[R_SUBG:R_SUBG + 1, :]
    bqg = sp_ref[R_BQG:R_BQG + 1, :]
    bkg = sp_ref[R_BKG:R_BKG + 1, :]

    for i in range(CTX_NB):
        rows = slice(i * SEQ, (i + 1) * SEQ)
        tile = lambda c: qkv_ref[rows, c:c + LANES]
        ka_out, va_out, small_out = (ref.at[i, 0] if creates else ref.at[i]
                                     for ref in (ka_ref, va_ref, small_ref))

        for h in range(A_HEADS):
            q = _split_heads(tile(QA + h * LANES) * QSCALE, lane_lo).astype(BF16)
            k = tile(KA + h * LANES)
            v = tile(VA + h * LANES)
            ka_out[pl.ds(h, SEQ, stride=A_HEADS), :] = k
            va_out[pl.ds(h, SEQ, stride=A_HEADS), :] = v
            v = v.astype(BF16)
            s = _dot_nt(q, k.astype(BF16))
            p1, _ = _probs(s[:SEQ])
            p2, _ = _probs(s[SEQ:])
            o1 = jnp.dot(p1.astype(BF16), v, preferred_element_type=F32) / jnp.sum(p1, axis=-1, keepdims=True)
            o2 = jnp.dot(p2.astype(BF16), v, preferred_element_type=F32) / jnp.sum(p2, axis=-1, keepdims=True)
            o = _subln(o1 - lam * o2, subg, lam_init)
            att_ref[rows, h * LANES:(h + 1) * LANES] = o.astype(BF16)

        kb = _head_rmsnorm(tile(KB), bkg, lane_lo)
        vb, kc, vc = tile(VB), tile(KC), tile(VC)
        for x, c in ((kb, SMALL_KB), (vb, SMALL_VB), (kc, SMALL_KC), (vc, SMALL_VC)):
            small_out[:, c:c + LANES] = x
        for mixer, (q0, k_t, v_t, o0) in enumerate(((QB, kb, vb, BRANCH), (QC, kc, vc, 2 * BRANCH))):
            for g in range(2):
                k_lo, k_hi = _lo_hi(k_t, lane, g == 1)
                v_lo, v_hi = _lo_hi(v_t, lane, g == 1, ones_lane=True)
                k_both = (k_lo + k_hi).astype(BF16)
                v_lo = v_lo.astype(BF16)
                v_hi = v_hi.astype(BF16)
                for jj in range(2):
                    j = 2 * g + jj
                    q = tile(q0 + j * LANES)
                    if mixer == 0:
                        q = _head_rmsnorm(q, bqg, lane_lo)
                    s = _dot_nt(_split_heads(q * QSCALE, lane_lo).astype(BF16), k_both)
                    if mixer == 0:
                        pe, _ = _probs(s[:SEQ])
                        po, _ = _probs(s[SEQ:])
                    else:
                        sink_e = sink_ref[layer, 2 * j] * LOG2E
                        sink_o = sink_ref[layer, 2 * j + 1] * LOG2E
                        pe, me = _probs(s[:SEQ], sink_e)
                        po, mo = _probs(s[SEQ:], sink_o)
                    oe = jnp.dot(pe.astype(BF16), v_lo, preferred_element_type=F32)
                    oo = jnp.dot(po.astype(BF16), v_hi, preferred_element_type=F32)
                    le = oe[:, ONES_LANE_LO:ONES_LANE_LO + 1]
                    lo_ = oo[:, ONES_LANE_HI:ONES_LANE_HI + 1]
                    if mixer == 1:
                        le = le + jnp.exp2(sink_e - me)
                        lo_ = lo_ + jnp.exp2(sink_o - mo)
                    o = jnp.where(lane_lo, oe, oo) / jnp.where(lane_lo, le, lo_)
                    att_ref[rows, o0 + j * LANES:o0 + (j + 1) * LANES] = o.astype(BF16)


def _att_ctx(qkv, sp, sink, cache_out, layer, lam_init):
    t = qkv.shape[0]
    nb = t // SEQ
    rows = CTX_NB * SEQ
    creates = cache_out is None
    assert creates == (layer == 0)
    layers = DEPTH if creates else None
    at_layer = lambda b: (b, 0 if creates else layer, 0, 0)
    wide = pl.BlockSpec((CTX_NB, layers, SEQ * A_HEADS, LANES), at_layer)
    small = pl.BlockSpec((CTX_NB, layers, SEQ, 4 * LANES), at_layer)
    wide_shape = jax.ShapeDtypeStruct((nb, DEPTH, SEQ * A_HEADS, LANES), F32)
    small_shape = jax.ShapeDtypeStruct((nb, DEPTH, SEQ, 4 * LANES), F32)
    in_specs = [
        pl.BlockSpec((rows, D_QKV), lambda b: (b, 0)),
        pl.BlockSpec((None, 8, LANES), lambda b: (layer, 0, 0)),
        pl.BlockSpec(memory_space=pltpu.SMEM),
    ]
    operands = (qkv, sp, sink)
    aliases = {}
    if not creates:
        in_specs += [pl.BlockSpec(memory_space=pl.ANY)] * 3
        aliases = {len(operands) + k: 1 + k for k in range(3)}
        operands += tuple(cache_out)
    att, *cache_out = pl.pallas_call(
        functools.partial(_att_ctx_kernel, layer=layer, lam_init=lam_init),
        grid=(nb // CTX_NB,),
        in_specs=in_specs,
        out_specs=[pl.BlockSpec((rows, D_ATT), lambda b: (b, 0)), wide, wide, small],
        out_shape=[jax.ShapeDtypeStruct((t, D_ATT), BF16), wide_shape, wide_shape, small_shape],
        input_output_aliases=aliases,
        compiler_params=_cparams(("arbitrary",)),
        name="att_ctx",
    )(*operands)
    return att, cache_out


LAT_TQ_NORM = 256
NK = PAST_LEN + DEC_SEQ
KCH = 512
KV_BLK = 512
assert KA % KV_BLK == 0 and VA % KV_BLK == 0 and KB // KV_BLK == VB // KV_BLK and KC // KV_BLK == VC // KV_BLK
PREP_ROWS = 512
N_UNITS = 12
UNITS_PER_STEP = 2
A_STEPS = A_HEADS // UNITS_PER_STEP


def _attend_all_queries(q, k_chunks, v_chunks, masks=None, sink=None, l_lane=None):
    m = l = acc = None
    for c, (k_c, v_c) in enumerate(zip(k_chunks, v_chunks)):
        s = _dot_nt(q, k_c)
        if masks is not None and masks[c] is not None:
            s = jnp.where(masks[c], s, NEG_INF)
        mc = jnp.max(s, axis=-1, keepdims=True)
        if m is None:
            m_new = mc if sink is None else jnp.maximum(mc, sink)
            p = jnp.exp2(s - m_new)
            acc = jnp.dot(p.astype(BF16), v_c, preferred_element_type=F32)
            if l_lane is None:
                l = jnp.sum(p, axis=-1, keepdims=True)
        else:
            m_new = jnp.maximum(m, mc)
            alpha = jnp.exp2(m - m_new)
            p = jnp.exp2(s - m_new)
            acc = alpha * acc + jnp.dot(p.astype(BF16), v_c, preferred_element_type=F32)
            if l_lane is None:
                l = alpha * l + jnp.sum(p, axis=-1, keepdims=True)
        m = m_new
    if l_lane is not None:
        l = acc[:, l_lane:l_lane + 1]
    if sink is not None:
        l = l + jnp.exp2(sink - m)
    return acc, l


def _att_lat_kernel(q_ref, kva_ref, kvv_ref, kvb_ref, kvc_ref,
                    cak_ref, cav_ref, cbk_ref, cbv_ref, cck_ref, ccv_ref,
                    cos_ref, sin_ref, sp_ref, sink_ref,
                    att_ref,
                    ka_scr, va_scr, kb_scr, vb_scr, kc_scr, vc_scr, q_scr,
                    *, layer, lam_init):
    step_id = pl.program_id(1)
    subg = sp_ref[R_SUBG:R_SUBG + 1, :]
    bqg = sp_ref[R_BQG:R_BQG + 1, :]
    bkg = sp_ref[R_BKG:R_BKG + 1, :]

    @pl.when(step_id == 0)
    def _prepare_keys():
        lane = lax.broadcasted_iota(jnp.int32, (PREP_ROWS, LANES), 1)
        lane_lo = lane < HEAD_DIM
        def put(dst, a_k, a_v, pairs):
            lo_rows = pl.ds(pl.multiple_of(dst, PREP_ROWS), PREP_ROWS)
            hi_rows = pl.ds(pl.multiple_of(NK + dst, PREP_ROWS), PREP_ROWS)
            for h in range(A_HEADS):
                k = a_k[h]
                zero = jnp.zeros_like(k)
                ka_scr[h, lo_rows, :] = jnp.where(lane_lo, k, zero).astype(BF16)
                ka_scr[h, hi_rows, :] = jnp.where(lane_lo, zero, k).astype(BF16)
                va_scr[h, lo_rows, :] = a_v[h].astype(BF16)
            for x, scr, is_value in pairs:
                for g in range(2):
                    lo, hi = _lo_hi(x, lane, g == 1, ones_lane=is_value)
                    scr[g, lo_rows, :] = lo.astype(BF16)
                    scr[g, hi_rows, :] = hi.astype(BF16)

        def cached(i, carry):
            r = pl.ds(pl.multiple_of(i * PREP_ROWS, PREP_ROWS), PREP_ROWS)
            head_rows = lambda h: pl.ds(i * (PREP_ROWS * A_HEADS) + h, PREP_ROWS, stride=A_HEADS)
            put(i * PREP_ROWS,
                [cak_ref[head_rows(h), :] for h in range(A_HEADS)],
                [cav_ref[head_rows(h), :] for h in range(A_HEADS)],
                ((cbk_ref[r, :], kb_scr, False), (cbv_ref[r, :], vb_scr, True),
                 (cck_ref[r, :], kc_scr, False), (ccv_ref[r, :], vc_scr, True)))
            return carry
        lax.fori_loop(0, PAST_LEN // PREP_ROWS, cached, 0)

        def latent(i, carry):
            r = pl.ds(pl.multiple_of(i * PREP_ROWS, PREP_ROWS), PREP_ROWS)
            cos = cos_ref[r, :]
            sin = sin_ref[r, :]
            in_blk = lambda ref, col: ref[r, col % KV_BLK:col % KV_BLK + LANES]
            kb = _rope(_head_rmsnorm(in_blk(kvb_ref, KB), bkg, lane_lo), cos, sin, lane)
            kc = _rope(in_blk(kvc_ref, KC), cos, sin, lane)
            put(PAST_LEN + i * PREP_ROWS,
                [_rope(kva_ref[r, h * LANES:(h + 1) * LANES], cos, sin, lane) for h in range(A_HEADS)],
                [kvv_ref[r, h * LANES:(h + 1) * LANES] for h in range(A_HEADS)],
                ((kb, kb_scr, False), (in_blk(kvb_ref, VB), vb_scr, True),
                 (kc, kc_scr, False), (in_blk(kvc_ref, VC), vc_scr, True)))
            return carry
        lax.fori_loop(0, DEC_SEQ // PREP_ROWS, latent, 0)

    lam = _lam(sp_ref, lam_init)
    lane_lo_all = lax.broadcasted_iota(jnp.int32, (DEC_SEQ, LANES), 1) < HEAD_DIM
    chunks = lambda scr, i, base: [scr[i, base + c * KCH:base + (c + 1) * KCH, :] for c in range(NK // KCH)]

    def prepare_queries(normalise):
        tq = LAT_TQ_NORM if normalise else DEC_SEQ
        lane = lax.broadcasted_iota(jnp.int32, (tq, LANES), 1)
        lane_lo = lane < HEAD_DIM

        def step(i, carry):
            r = pl.ds(pl.multiple_of(i * tq, tq), tq)
            for n in range(UNITS_PER_STEP):
                q = q_ref[r, n * LANES:(n + 1) * LANES]
                if normalise:
                    q = _head_rmsnorm(q, bqg, lane_lo)
                q_scr[n, r, :] = (_rope(q, cos_ref[r, :], sin_ref[r, :], lane) * QSCALE).astype(BF16)
            return carry
        lax.fori_loop(0, DEC_SEQ // tq, step, 0)

    @pl.when(step_id < A_STEPS)
    def _mixer_a():
        prepare_queries(False)
        for n in range(UNITS_PER_STEP):
            h = step_id * UNITS_PER_STEP + n
            q = q_scr[n]
            v = chunks(va_scr, h, 0)
            a1, l1 = _attend_all_queries(q, chunks(ka_scr, h, 0), v)
            a2, l2 = _attend_all_queries(q, chunks(ka_scr, h, NK), v)
            att_ref[:, n * LANES:(n + 1) * LANES] = _subln(a1 / l1 - lam * (a2 / l2), subg,
                                                          lam_init).astype(BF16)

    def pair_unit(n, k_scr, v_scr, g, masks=None, sinks=(None, None)):
        q = q_scr[n]
        ae, le = _attend_all_queries(q, chunks(k_scr, g, 0), chunks(v_scr, g, 0),
                                     masks=masks, sink=sinks[0], l_lane=ONES_LANE_LO)
        ao, lo_ = _attend_all_queries(q, chunks(k_scr, g, NK), chunks(v_scr, g, NK),
                                      masks=masks, sink=sinks[1], l_lane=ONES_LANE_HI)
        o = jnp.where(lane_lo_all, ae, ao) / jnp.where(lane_lo_all, le, lo_)
        att_ref[:, n * LANES:(n + 1) * LANES] = o.astype(BF16)

    @pl.when((step_id >= A_STEPS) & (step_id < 2 * A_STEPS))
    def _mixer_b():
        prepare_queries(True)
        for n in range(UNITS_PER_STEP):
            unit = (step_id - A_STEPS) * UNITS_PER_STEP + n
            pair_unit(n, kb_scr, vb_scr, lax.shift_right_logical(unit, 1))

    @pl.when(step_id >= 2 * A_STEPS)
    def _mixer_c():
        prepare_queries(False)
        qpos = lax.broadcasted_iota(jnp.int32, (DEC_SEQ, KCH), 0)
        kcol = lax.broadcasted_iota(jnp.int32, (DEC_SEQ, KCH), 1)
        masks = (None,) * (PAST_LEN // KCH) + tuple(
            jnp.abs(kcol + k0 - qpos) <= WINDOW for k0 in range(0, DEC_SEQ, KCH))
        for n in range(UNITS_PER_STEP):
            unit = (step_id - 2 * A_STEPS) * UNITS_PER_STEP + n
            pair_unit(n, kc_scr, vc_scr, lax.shift_right_logical(unit, 1), masks=masks,
                      sinks=(sink_ref[layer, 2 * unit] * LOG2E, sink_ref[layer, 2 * unit + 1] * LOG2E))


def _step_q_col(s):
    width = UNITS_PER_STEP * LANES
    return jnp.where(s < A_STEPS, s, jnp.where(s < 2 * A_STEPS, QB // width - A_STEPS + s,
                                               QC // width - 2 * A_STEPS + s))


def _att_lat(qkv, caches, cos, sin, sp, sink, layer, lam_init):
    t = qkv.shape[0]
    kv_blk = lambda col: pl.BlockSpec((DEC_SEQ, KV_BLK), lambda b, i: (b, col // KV_BLK))
    cache_blk = lambda r: pl.BlockSpec((None, None, r, LANES), lambda b, i: (b, layer, 0, 0))
    return pl.pallas_call(
        functools.partial(_att_lat_kernel, layer=layer, lam_init=lam_init),
        grid=(DEC_BATCH, N_UNITS // UNITS_PER_STEP),
        in_specs=[
            pl.BlockSpec((DEC_SEQ, UNITS_PER_STEP * LANES), lambda b, i: (b, _step_q_col(i))),
            kv_blk(KA), kv_blk(VA), kv_blk(KB), kv_blk(KC),
            cache_blk(PAST_LEN * A_HEADS), cache_blk(PAST_LEN * A_HEADS),
            cache_blk(PAST_LEN), cache_blk(PAST_LEN), cache_blk(PAST_LEN), cache_blk(PAST_LEN),
            pl.BlockSpec((DEC_SEQ, LANES), lambda b, i: (0, 0)),
            pl.BlockSpec((DEC_SEQ, LANES), lambda b, i: (0, 0)),
            pl.BlockSpec((None, 8, LANES), lambda b, i: (layer, 0, 0)),
            pl.BlockSpec(memory_space=pltpu.SMEM),
        ],
        out_specs=pl.BlockSpec((DEC_SEQ, UNITS_PER_STEP * LANES), lambda b, i: (b, i)),
        out_shape=jax.ShapeDtypeStruct((t, D_ATT), BF16),
        scratch_shapes=[
            pltpu.VMEM((A_HEADS, 2 * NK, LANES), BF16),
            pltpu.VMEM((A_HEADS, NK, LANES), BF16),
            pltpu.VMEM((2, 2 * NK, LANES), BF16),
            pltpu.VMEM((2, 2 * NK, LANES), BF16),
            pltpu.VMEM((2, 2 * NK, LANES), BF16),
            pltpu.VMEM((2, 2 * NK, LANES), BF16),
            pltpu.VMEM((UNITS_PER_STEP, DEC_SEQ, LANES), BF16),
        ],
        compiler_params=_cparams(("arbitrary", "arbitrary")),
        name="att_lat",
    )(qkv, qkv, qkv, qkv, qkv, *caches, cos, sin, sp, sink)


POST_TM = 512


N_POST_COPIES = 2 * N_BRANCH + 1


def _post_kernel(att_ref, h1_ref, x_ref, mod_ref, g2_ref, w_in_hbm, wa_hbm, wb_hbm, wc_hbm, wo_hbm,
                 o_ref, h_ref, wg_stage, wbr_stage, wo_stage, wg_scr, wbr_scr, wo_scr, sems,
                 *, layer, tm, rows_per_mod, row0):
    def gate_copy(i):
        src = w_in_hbm.at[layer, :, pl.ds(D_QKV + i * D_MODEL, D_MODEL)]
        return pltpu.make_async_copy(src, wg_stage.at[i], sems.at[i])

    def branch_copy(i):
        return pltpu.make_async_copy((wa_hbm, wb_hbm, wc_hbm)[i].at[layer], wbr_stage.at[i],
                                     sems.at[N_BRANCH + i])

    def out_copy():
        return pltpu.make_async_copy(wo_hbm.at[layer], wo_stage, sems.at[2 * N_BRANCH])

    def compute(fetch):
        h1 = h1_ref[...]
        merged = None
        for i in range(N_BRANCH):
            if fetch:
                branch_copy(i).wait()
                wbr_scr[i] = wbr_stage[i].astype(BF16)
                gate_copy(i).wait()
                wg_scr[i] = wg_stage[i].astype(BF16)
            y = jnp.dot(att_ref[:, i * BRANCH:(i + 1) * BRANCH], wbr_scr[i], preferred_element_type=F32)
            gate = jnp.dot(h1, wg_scr[i], preferred_element_type=F32)
            term = jax.nn.sigmoid(gate) * y
            merged = term if merged is None else merged + term
        if fetch:
            out_copy().wait()
            wo_scr[...] = wo_stage[...].astype(BF16)
        mixed = jnp.dot(merged.astype(BF16), wo_scr[...], preferred_element_type=F32)
        row = row0 + lax.div(pl.program_id(0) * tm, rows_per_mod)
        x1 = x_ref[...] + _mod_block(mod_ref, row, M_G1) * mixed
        o_ref[...] = x1
        y = x1 * lax.rsqrt(jnp.mean(x1 * x1, axis=-1, keepdims=True) + EPS) * g2_ref[...]
        h_ref[...] = (y * (1.0 + _mod_block(mod_ref, row, M_SC2))
                      + _mod_block(mod_ref, row, M_SH2)).astype(BF16)

    first = pl.program_id(0) == 0

    @pl.when(first)
    def _():
        for i in range(N_BRANCH):
            branch_copy(i).start()
            gate_copy(i).start()
        out_copy().start()
        compute(True)

    @pl.when(jnp.logical_not(first))
    def _():
        compute(False)


def _post(att, h1, x, mod, g2, w_in, wa, wb, wc, wo, layer, rows_per_mod, row0):
    t = x.shape[0]
    tm = POST_TM
    in_hbm = pl.BlockSpec(memory_space=pl.ANY)
    return pl.pallas_call(
        functools.partial(_post_kernel, layer=layer, tm=tm, rows_per_mod=rows_per_mod, row0=row0),
        grid=(t // tm,),
        in_specs=[
            pl.BlockSpec((tm, D_ATT), lambda m: (m, 0)),
            pl.BlockSpec((tm, D_MODEL), lambda m: (m, 0)),
            pl.BlockSpec((tm, D_MODEL), lambda m: (m, 0)),
            _mod_spec(layer),
            pl.BlockSpec((None, 1, D_MODEL), lambda m: (layer, 0, 0)),
            in_hbm, in_hbm, in_hbm, in_hbm, in_hbm,
        ],
        out_specs=[pl.BlockSpec((tm, D_MODEL), lambda m: (m, 0)),
                   pl.BlockSpec((tm, D_MODEL), lambda m: (m, 0))],
        out_shape=[jax.ShapeDtypeStruct((t, D_MODEL), F32),
                   jax.ShapeDtypeStruct((t, D_MODEL), BF16)],
        scratch_shapes=[pltpu.VMEM((N_BRANCH, D_MODEL, D_MODEL), F32),
                        pltpu.VMEM((N_BRANCH, BRANCH, D_MODEL), F32),
                        pltpu.VMEM((D_MODEL, D_MODEL), F32),
                        pltpu.VMEM((N_BRANCH, D_MODEL, D_MODEL), BF16),
                        pltpu.VMEM((N_BRANCH, BRANCH, D_MODEL), BF16),
                        pltpu.VMEM((D_MODEL, D_MODEL), BF16),
                        pltpu.SemaphoreType.DMA((N_POST_COPIES,))],
        compiler_params=_cparams(("arbitrary",)),
        name="post",
    )(att, h1, x, mod, g2, w_in, wa, wb, wc, wo)


FFN_TM = 1024
FFN_TF = 256


FFN_GROUP = 2


def _ffn_kernel(x_ref, h_ref, mod_ref, fg_ref, wa_ref, wb_ref, wo_ref, o_ref,
                acc_scr, wab_scr, wo_scr, *, tm, rows_per_mod, row0, final_norm):
    grp = pl.program_id(0)
    f = pl.program_id(1)
    m = pl.program_id(2)

    @pl.when(f == 0)
    def _():
        acc_scr[m] = jnp.zeros((tm, D_MODEL), F32)

    @pl.when(m == 0)
    def _():
        wab_scr[:, 0:FFN_TF] = wa_ref[...].astype(BF16)
        wab_scr[:, FFN_TF:2 * FFN_TF] = wb_ref[...].astype(BF16)
        wo_scr[...] = wo_ref[...].astype(BF16)

    ab = jnp.dot(h_ref[pl.ds(pl.multiple_of(m * tm, tm), tm), :], wab_scr[...],
                 preferred_element_type=F32)
    a = ab[:, 0:FFN_TF]
    y = (a * jax.nn.sigmoid(a) * ab[:, FFN_TF:2 * FFN_TF]).astype(BF16)
    acc_scr[m] += jnp.dot(y, wo_scr[...], preferred_element_type=F32)

    @pl.when(f == pl.num_programs(1) - 1)
    def _():
        row = row0 + lax.div((grp * FFN_GROUP + m) * tm, rows_per_mod)
        y = x_ref[...] + _mod_block(mod_ref, row, M_G2) * acc_scr[m]
        if final_norm:
            y = y * lax.rsqrt(jnp.mean(y * y, axis=-1, keepdims=True) + EPS) * fg_ref[...]
        o_ref[...] = y


def _ffn(x, h, mod, final_g, w_in, w_out, layer, rows_per_mod, row0):
    t = x.shape[0]
    tm = FFN_TM
    assert rows_per_mod % tm == 0
    nf = D_FF // FFN_TF
    io_idx = lambda grp, f, m: (grp * FFN_GROUP + jnp.where(f == nf - 1, m, 0), 0)
    return pl.pallas_call(
        functools.partial(_ffn_kernel, tm=tm, rows_per_mod=rows_per_mod, row0=row0,
                          final_norm=layer == DEPTH - 1),
        grid=(t // (tm * FFN_GROUP), nf, FFN_GROUP),
        in_specs=[
            pl.BlockSpec((tm, D_MODEL), io_idx),
            pl.BlockSpec((FFN_GROUP * tm, D_MODEL), lambda grp, f, m: (grp, 0)),
            _mod_spec(layer),
            pl.BlockSpec((1, D_MODEL), lambda grp, f, m: (0, 0)),
            pl.BlockSpec((None, D_MODEL, FFN_TF), lambda grp, f, m: (layer, 0, f)),
            pl.BlockSpec((None, D_MODEL, FFN_TF), lambda grp, f, m: (layer, 0, f + nf)),
            pl.BlockSpec((None, FFN_TF, D_MODEL), lambda grp, f, m: (layer, f, 0)),
        ],
        out_specs=pl.BlockSpec((tm, D_MODEL), io_idx),
        out_shape=jax.ShapeDtypeStruct((t, D_MODEL), F32),
        scratch_shapes=[pltpu.VMEM((FFN_GROUP, tm, D_MODEL), F32),
                        pltpu.VMEM((D_MODEL, 2 * FFN_TF), BF16),
                        pltpu.VMEM((FFN_TF, D_MODEL), BF16)],
        compiler_params=_cparams(("arbitrary", "arbitrary", "arbitrary")),
        name="ffn",
    )(x, h, mod, final_g, w_in, w_in, w_out)


def _rope_tables():
    rows = DEC_SEQ // GRID_W
    row = jnp.repeat(jnp.arange(rows), GRID_W).astype(F32)
    col = jnp.tile(jnp.arange(GRID_W), rows).astype(F32)
    n = HEAD_DIM // 4
    inv = ROPE_THETA ** (-jnp.arange(n, dtype=F32) / n)
    ang = jnp.concatenate([row[:, None] * inv, col[:, None] * inv], axis=-1)
    cos, sin = jnp.cos(ang), jnp.sin(ang)
    cos_t = jnp.tile(cos, (1, 4))
    sin_t = jnp.tile(jnp.concatenate([-sin, sin], axis=-1), (1, 2))
    return cos_t, sin_t


def _pack_small(a_lam_q1, a_lam_k1, a_lam_q2, a_lam_k2, a_subln_g, b_qnorm_g, b_knorm_g):
    pad = lambda v: jnp.pad(v, ((0, 0), (0, LANES - HEAD_DIM)))
    rows = [pad(a_lam_q1), pad(a_lam_k1), pad(a_lam_q2), pad(a_lam_k2), a_subln_g,
            jnp.tile(b_qnorm_g, (1, 2)), jnp.tile(b_knorm_g, (1, 2)),
            jnp.zeros((DEPTH, LANES), F32)]
    return jnp.stack(rows, axis=1)


def kernel(x_prompt, x_sample, cache_a_k, cache_a_v, cache_b_k, cache_b_v, cache_c_k, cache_c_v, c, c_ctx, w_mod, b_mod, norm1_g, norm2_g, w_in, a_lam_q1, a_lam_k1, a_lam_q2, a_lam_k2, a_subln_g, b_qnorm_g, b_knorm_g, c_sink, w_br_a, w_br_b, w_br_c, w_out, w_ffn_in, w_ffn_out, final_g):
    t_ctx = BATCH * SEQ
    t_lat = DEC_BATCH * DEC_SEQ
    xp = x_prompt.reshape(t_ctx, D_MODEL)
    xs = x_sample.reshape(t_lat, D_MODEL)
    cv8 = jnp.concatenate([c_ctx[None, :], c, jnp.zeros((8 - 1 - DEC_BATCH, D_MODEL), F32)], axis=0)
    mod = _modulation(cv8, w_mod, b_mod)
    cos_t, sin_t = _rope_tables()
    sp = _pack_small(a_lam_q1, a_lam_k1, a_lam_q2, a_lam_k2, a_subln_g, b_qnorm_g, b_knorm_g)
    n1 = norm1_g.reshape(DEPTH, 1, D_MODEL)
    n2 = norm2_g.reshape(DEPTH, 1, D_MODEL)
    caches = (cache_a_k.reshape(DEC_BATCH, DEPTH, PAST_LEN * A_HEADS, LANES),
              cache_a_v.reshape(DEC_BATCH, DEPTH, PAST_LEN * A_HEADS, LANES),
              cache_b_k.reshape(DEC_BATCH, DEPTH, PAST_LEN, LANES),
              cache_b_v.reshape(DEC_BATCH, DEPTH, PAST_LEN, LANES),
              cache_c_k.reshape(DEC_BATCH, DEPTH, PAST_LEN, LANES),
              cache_c_v.reshape(DEC_BATCH, DEPTH, PAST_LEN, LANES))
    fg = final_g.reshape(1, D_MODEL)
    cache_out = None
    for l in range(DEPTH):
        lam_init = 0.8 - 0.6 * math.exp(-0.3 * l)

        qkv_c, h1_c = _proj(xp, mod, n1, w_in, l, t_ctx, 0)
        att_c, cache_out = _att_ctx(qkv_c, sp, c_sink, cache_out, l, lam_init)
        xp, hp = _post(att_c, h1_c, xp, mod, n2, w_in, w_br_a, w_br_b, w_br_c, w_out, l, t_ctx, 0)
        xp = _ffn(xp, hp, mod, fg, w_ffn_in, w_ffn_out, l, t_ctx, 0)

        qkv_s, h1_s = _proj(xs, mod, n1, w_in, l, DEC_SEQ, 1)
        att_s = _att_lat(qkv_s, caches, cos_t, sin_t, sp, c_sink, l, lam_init)
        xs, hs = _post(att_s, h1_s, xs, mod, n2, w_in, w_br_a, w_br_b, w_br_c, w_out, l, DEC_SEQ, 1)
        xs = _ffn(xs, hs, mod, fg, w_ffn_in, w_ffn_out, l, DEC_SEQ, 1)

    y_prompt = xp.reshape(BATCH, SEQ, D_MODEL)
    y_sample = xs.reshape(DEC_BATCH, DEC_SEQ, D_MODEL)
    ka_all, va_all, small_all = cache_out
    wide = tuple(a.reshape(BATCH, DEPTH, SEQ, A_HEADS, 2 * HEAD_DIM) for a in (ka_all, va_all))
    small = tuple(small_all[..., c:c + LANES].reshape(BATCH, DEPTH, SEQ, 2, HEAD_DIM)
                  for c in (SMALL_KB, SMALL_VB, SMALL_KC, SMALL_VC))
    return (y_prompt, y_sample) + wide + small
```
